```python
import math, functools
import jax, jax.numpy as jnp
from jax import lax
import numpy as np

D_MODEL = 2048
BATCH = 4
SEQ = 2048
DEPTH = 1
DEC_BATCH = 128
DEC_SEQ = 8
PAST_LEN = 16384
PAGE_SIZE = 128

GLA_HEADS = 4
GLA_DK = D_MODEL // 2 // GLA_HEADS
GLA_DV = D_MODEL // GLA_HEADS
GLA_RANK = 16
GLA_TAU = 16.0
GDN_HEADS = 16
GDN_DK = D_MODEL // GDN_HEADS
GDN_DV = D_MODEL // GDN_HEADS
CONV_W = 4
CHUNK = 64
EPS = 1e-6

GLA_QK = GLA_HEADS * GLA_DK
GLA_V = GLA_HEADS * GLA_DV
GDN_QK = GDN_HEADS * GDN_DK
GDN_V = GDN_HEADS * GDN_DV
CONV_CH = 2 * GDN_QK + GDN_V
SPLIT_SIZES = (GLA_QK, GLA_QK, GLA_V, GLA_RANK, GLA_V,
               CONV_CH, GDN_HEADS, GDN_HEADS, GDN_V,
               D_MODEL, D_MODEL)
SPLIT_POINTS = tuple(int(s) for s in np.cumsum(SPLIT_SIZES)[:-1])
D_IN = int(sum(SPLIT_SIZES))

kernel_name = "hybrid_gla_gdn_parallel_gated_step"


def _rmsnorm(x, g):
    xf = x.astype(jnp.float32)
    inv = lax.rsqrt(jnp.mean(xf * xf, axis=-1, keepdims=True) + EPS)
    return (xf * inv * g.astype(jnp.float32)).astype(x.dtype)


def _l2norm(x):
    xf = x.astype(jnp.float32)
    return xf * lax.rsqrt(jnp.sum(xf * xf, axis=-1, keepdims=True) + EPS)


def _to_chunks(a, c, n):
    t = a.shape[1]
    a = jnp.pad(a, [(0, 0), (0, n * c - t)] + [(0, 0)] * (a.ndim - 2))
    a = a.reshape((a.shape[0], n, c) + a.shape[2:])
    return jnp.moveaxis(a, 1, 0)


def _from_chunks(o, t):
    o = jnp.moveaxis(o, 0, 1)
    o = o.reshape((o.shape[0], o.shape[1] * o.shape[2]) + o.shape[3:])
    return o[:, :t]


def _gla_chunked(q, k, v, log_a, s0):
    t = q.shape[1]
    c = min(CHUNK, t)
    n = -(-t // c)
    xs = tuple(_to_chunks(a, c, n) for a in (q, k, v, log_a))
    mask = jnp.tril(jnp.ones((c, c), dtype=bool))

    def step(s, inp):
        qc, kc, vc, lac = inp
        b = jnp.cumsum(lac, axis=1)
        b_last = b[:, -1]
        q_d = qc * jnp.exp(b)
        k_d = kc * jnp.exp(-b)
        att = jnp.where(mask, jnp.einsum('bihd,bjhd->bhij', q_d, k_d), 0.0)
        o = (jnp.einsum('bhij,bjhv->bihv', att, vc)
             + jnp.einsum('bihd,bhdv->bihv', q_d, s))
        k_end = kc * jnp.exp(b_last[:, None] - b)
        s = s * jnp.exp(b_last)[..., None] + jnp.einsum('bjhd,bjhv->bhdv', k_end, vc)
        return s, o

    s, o = lax.scan(step, s0, xs)
    return _from_chunks(o, t), s


def _gdn_chunked(q, k, v, g, beta, s0):
    t = q.shape[1]
    c = min(CHUNK, t)
    n = -(-t // c)
    xs = tuple(_to_chunks(a, c, n) for a in (q, k, v, g, beta))
    incl = jnp.tril(jnp.ones((c, c), dtype=bool))
    strict = jnp.tril(jnp.ones((c, c), dtype=bool), k=-1)
    eye = jnp.eye(c, dtype=jnp.float32)

    def step(s, inp):
        qc, kc, vc, gc, bc = inp
        qh = jnp.moveaxis(qc, 1, 2)
        kh = jnp.moveaxis(kc, 1, 2)
        vh = jnp.moveaxis(vc, 1, 2)
        gh = jnp.cumsum(jnp.moveaxis(gc, 1, 2), axis=-1)
        bh = jnp.moveaxis(bc, 1, 2)
        diff = gh[..., :, None] - gh[..., None, :]
        decay = jnp.exp(jnp.where(incl, diff, -jnp.inf))
        kk = jnp.einsum('bhid,bhjd->bhij', kh, kh)
        m = jnp.where(strict, bh[..., :, None] * kk * decay, 0.0)
        rhs = jnp.concatenate([vh * bh[..., None],
                               kh * (bh * jnp.exp(gh))[..., None]], axis=-1)
        sol = lax.linalg.triangular_solve(eye + m, rhs, left_side=True, lower=True,
                                          unit_diagonal=True)
        u, w = sol[..., :GDN_DV], sol[..., GDN_DV:]
        v_new = u - jnp.einsum('bhcd,bhdv->bhcv', w, s)
        qk = jnp.einsum('bhid,bhjd->bhij', qh, kh) * decay
        o = (jnp.einsum('bhid,bhdv->bhiv', qh * jnp.exp(gh)[..., None], s)
             + jnp.einsum('bhij,bhjv->bhiv', qk, v_new))
        g_last = gh[..., -1]
        k_end = kh * jnp.exp(g_last[..., None] - gh)[..., None]
        s = s * jnp.exp(g_last)[..., None, None] + jnp.einsum('bhcd,bhcv->bhdv', k_end, v_new)
        return s, jnp.moveaxis(o, 1, 2)

    s, o = lax.scan(step, s0, xs)
    return _from_chunks(o, t), s


def _causal_conv(u, buf, w):
    t = u.shape[1]
    ext = jnp.concatenate([buf.astype(u.dtype), u], axis=1)
    out = sum(ext[:, i:i + t] * w[i] for i in range(CONV_W))
    return out, ext[:, -(CONV_W - 1):]


def _layer(x, s_gla, s_gdn, conv_buf, ln_g, w_in, w_alpha2, b_alpha, conv_w, a_log,
           dt_bias, gla_norm_g, gdn_norm_g, w_br_a, w_br_b, w_out):
    bsz, t = x.shape[0], x.shape[1]
    f32 = jnp.float32
    h = _rmsnorm(x, ln_g)
    proj = h @ w_in
    (q_a, k_a, v_a, lr_a, gate_a, qkv_b, beta_b, dec_b, gate_b, m_a, m_b) = jnp.split(
        proj, SPLIT_POINTS, axis=-1)

    q_a = q_a.reshape(bsz, t, GLA_HEADS, GLA_DK).astype(f32) * (GLA_DK ** -0.5)
    k_a = k_a.reshape(bsz, t, GLA_HEADS, GLA_DK).astype(f32)
    v_a = v_a.reshape(bsz, t, GLA_HEADS, GLA_DV).astype(f32)
    log_alpha = jax.nn.log_sigmoid((lr_a @ w_alpha2 + b_alpha).astype(f32)) / GLA_TAU
    log_alpha = log_alpha.reshape(bsz, t, GLA_HEADS, GLA_DK)
    o_a, s_gla_new = _gla_chunked(q_a, k_a, v_a, log_alpha, s_gla.astype(f32))
    o_a = _rmsnorm(o_a, gla_norm_g).reshape(bsz, t, GLA_V).astype(x.dtype) * jax.nn.silu(gate_a)
    y_a = o_a @ w_br_a

    qkv, conv_new = _causal_conv(qkv_b, conv_buf, conv_w)
    qkv = jax.nn.silu(qkv)
    q_b, k_b, v_b = jnp.split(qkv, [GDN_QK, 2 * GDN_QK], axis=-1)
    q_b = _l2norm(q_b.reshape(bsz, t, GDN_HEADS, GDN_DK)) * (GDN_DK ** -0.5)
    k_b = _l2norm(k_b.reshape(bsz, t, GDN_HEADS, GDN_DK))
    v_b = v_b.reshape(bsz, t, GDN_HEADS, GDN_DV).astype(f32)
    beta = jax.nn.sigmoid(beta_b.astype(f32))
    g = -jnp.exp(a_log.astype(f32)) * jax.nn.softplus(dec_b.astype(f32) + dt_bias.astype(f32))
    o_b, s_gdn_new = _gdn_chunked(q_b, k_b, v_b, g, beta, s_gdn.astype(f32))
    o_b = _rmsnorm(o_b, gdn_norm_g).reshape(bsz, t, GDN_V).astype(x.dtype) * jax.nn.silu(gate_b)
    y_b = o_b @ w_br_b

    merged = jax.nn.sigmoid(m_a) * y_a + jax.nn.sigmoid(m_b) * y_b
    out = x + merged @ w_out
    return out, s_gla_new.astype(x.dtype), s_gdn_new.astype(x.dtype), conv_new.astype(x.dtype)


def setup_inputs(seed: int = 0) -> dict:
    key = jax.random.key(seed)
    ks = jax.random.split(key, 20)
    f = jnp.float32
    x_prompt = jax.random.normal(ks[0], (BATCH, SEQ, D_MODEL), f)
    x_sample = jax.random.normal(ks[1], (DEC_BATCH, DEC_SEQ, D_MODEL), f)
    state_gla = jax.random.normal(ks[2], (DEPTH, DEC_BATCH, GLA_HEADS, GLA_DK, GLA_DV), f)
    state_gdn = 0.5 * jax.random.normal(ks[3], (DEPTH, DEC_BATCH, GDN_HEADS, GDN_DK, GDN_DV), f)
    state_conv = jax.random.normal(ks[4], (DEPTH, DEC_BATCH, CONV_W - 1, CONV_CH), f)
    ln_in_g = 1.0 + 0.02 * jax.random.normal(ks[5], (DEPTH, D_MODEL), f)
    w_in = jax.random.normal(ks[6], (DEPTH, D_MODEL, D_IN), f) * D_MODEL ** -0.5
    w_alpha2 = jax.random.normal(ks[7], (DEPTH, GLA_RANK, GLA_QK), f) * GLA_RANK ** -0.5
    b_alpha = 0.1 * jax.random.normal(ks[8], (DEPTH, GLA_QK), f)
    conv_w = jax.random.normal(ks[9], (DEPTH, CONV_W, CONV_CH), f) * CONV_W ** -0.5
    a_log = jnp.log(jax.random.uniform(ks[10], (DEPTH, GDN_HEADS), f, 1.0, 16.0))
    dt = jnp.exp(jax.random.uniform(ks[11], (DEPTH, GDN_HEADS), f,
                                    math.log(1e-3), math.log(1e-1)))
    dt_bias = dt + jnp.log(-jnp.expm1(-dt))
    gla_norm_g = 1.0 + 0.02 * jax.random.normal(ks[12], (DEPTH, GLA_DV), f)
    gdn_norm_g = 1.0 + 0.02 * jax.random.normal(ks[13], (DEPTH, GDN_DV), f)
    w_br_a = jax.random.normal(ks[14], (DEPTH, GLA_V, D_MODEL), f) * GLA_V ** -0.5
    w_br_b = jax.random.normal(ks[15], (DEPTH, GDN_V, D_MODEL), f) * GDN_V ** -0.5
    w_out = jax.random.normal(ks[16], (DEPTH, D_MODEL, D_MODEL), f) * D_MODEL ** -0.5
    final_norm_g = 1.0 + 0.02 * jax.random.normal(ks[17], (D_MODEL,), f)
    return {"x_prompt": x_prompt, "x_sample": x_sample,
            "state_gla": state_gla, "state_gdn": state_gdn, "state_conv": state_conv,
            "ln_in_g": ln_in_g, "w_in": w_in, "w_alpha2": w_alpha2, "b_alpha": b_alpha,
            "conv_w": conv_w, "a_log": a_log, "dt_bias": dt_bias,
            "gla_norm_g": gla_norm_g, "gdn_norm_g": gdn_norm_g,
            "w_br_a": w_br_a, "w_br_b": w_br_b, "w_out": w_out,
            "final_norm_g": final_norm_g}


def reference(x_prompt, x_sample, state_gla, state_gdn, state_conv, ln_in_g, w_in, w_alpha2,
              b_alpha, conv_w, a_log, dt_bias, gla_norm_g, gdn_norm_g, w_br_a, w_br_b, w_out,
              final_norm_g):
    bp = x_prompt.shape[0]
    dt_ = x_prompt.dtype
    hp, hs = x_prompt, x_sample
    gla_p, gdn_p, conv_p, gla_s, gdn_s, conv_s = [], [], [], [], [], []
    for l in range(DEPTH):
        params = (ln_in_g[l], w_in[l], w_alpha2[l], b_alpha[l], conv_w[l], a_log[l], dt_bias[l],
                  gla_norm_g[l], gdn_norm_g[l], w_br_a[l], w_br_b[l], w_out[l])
        z_gla = jnp.zeros((bp, GLA_HEADS, GLA_DK, GLA_DV), dt_)
        z_gdn = jnp.zeros((bp, GDN_HEADS, GDN_DK, GDN_DV), dt_)
        z_conv = jnp.zeros((bp, CONV_W - 1, CONV_CH), dt_)
        hp, sg, sd, sc = _layer(hp, z_gla, z_gdn, z_conv, *params)
        gla_p.append(sg); gdn_p.append(sd); conv_p.append(sc)
        hs, sg, sd, sc = _layer(hs, state_gla[l], state_gdn[l], state_conv[l], *params)
        gla_s.append(sg); gdn_s.append(sd); conv_s.append(sc)
    y_prompt = _rmsnorm(hp, final_norm_g)
    y_sample = _rmsnorm(hs, final_norm_g)
    new_gla_prompt = jnp.stack(gla_p)
    new_gdn_prompt = jnp.stack(gdn_p)
    new_conv_prompt = jnp.stack(conv_p)
    new_gla_sample = jnp.stack(gla_s)
    new_gdn_sample = jnp.stack(gdn_s)
    new_conv_sample = jnp.stack(conv_s)
    return (y_prompt, y_sample, new_gla_prompt, new_gdn_prompt, new_conv_prompt,
            new_gla_sample, new_gdn_sample, new_conv_sample)
```

```python
import functools

import jax
import jax.numpy as jnp
from jax import lax
from jax.experimental import pallas as pl
from jax.experimental.pallas import tpu as pltpu

F32 = jnp.float32
BF16 = jnp.bfloat16

D_MODEL = 2048
GLA_HEADS = 4
GLA_DK = 256
GLA_DV = 512
GLA_RANK = 16
GLA_TAU = 16.0
GDN_HEADS = 16
GDN_DK = 128
GDN_DV = 128
CONV_W = 4
CHUNK = 64
EPS = 1e-6

GLA_QK = GLA_HEADS * GLA_DK
GLA_V = GLA_HEADS * GLA_DV
GDN_QK = GDN_HEADS * GDN_DK
GDN_V = GDN_HEADS * GDN_DV
CONV_CH = 2 * GDN_QK + GDN_V

LANES = 128
COL_QA = 0
COL_KA = COL_QA + GLA_QK
COL_VA = COL_KA + GLA_QK
COL_GATE_A = COL_VA + GLA_V
COL_QKV_B = COL_GATE_A + GLA_V
COL_GATE_B = COL_QKV_B + CONV_CH
COL_MA = COL_GATE_B + GDN_V
COL_MB = COL_MA + D_MODEL
N_MAIN = COL_MB + D_MODEL
SM_LR = 0
SM_BETA = GLA_RANK
SM_DEC = SM_BETA + GDN_HEADS

VMEM_LIMIT = 56 * 1024 * 1024


def _bdot(a, b):
    return jnp.dot(a.astype(BF16), b.astype(BF16), preferred_element_type=F32)


def _bdot_nt(a, b):
    return lax.dot_general(a.astype(BF16), b.astype(BF16), (((1,), (1,)), ((), ())),
                           preferred_element_type=F32)


def _bdot_tn(a, b):
    return lax.dot_general(a.astype(BF16), b.astype(BF16), (((0,), (0,)), ((), ())),
                           preferred_element_type=F32)


def _hdot(a, b):
    return jnp.dot(a, b, precision=lax.Precision.HIGHEST, preferred_element_type=F32)


def _sigmoid(x):
    return 1.0 / (1.0 + jnp.exp(-x))


def _silu(x):
    return x * _sigmoid(x)


def _softplus(x):
    return jnp.maximum(x, 0.0) + jnp.log1p(jnp.exp(-jnp.abs(x)))


def _tri_incl(c):
    r = lax.broadcasted_iota(jnp.int32, (c, c), 0)
    col = lax.broadcasted_iota(jnp.int32, (c, c), 1)
    return r >= col, r > col


def _in_proj_kernel(x_ref, g_ref, w_ref, ws_ref, o_ref, os_ref, h_ref):
    @pl.when(pl.program_id(1) == 0)
    def _():
        x = x_ref[...]
        inv = lax.rsqrt(jnp.mean(x * x, axis=-1, keepdims=True) + EPS)
        h = (x * inv * g_ref[...]).astype(BF16)
        h_ref[...] = h
        os_ref[...] = jnp.dot(h, ws_ref[...], preferred_element_type=F32)

    o_ref[...] = jnp.dot(h_ref[...], w_ref[...], preferred_element_type=F32)


def _in_proj(x2d, ln_g, w_main, w_small):
    n_tok, d = x2d.shape
    tm = min(1024, n_tok)
    tn = 1024
    return pl.pallas_call(
        _in_proj_kernel,
        grid=(n_tok // tm, N_MAIN // tn),
        in_specs=[
            pl.BlockSpec((tm, d), lambda i, j: (i, 0)),
            pl.BlockSpec((1, d), lambda i, j: (0, 0)),
            pl.BlockSpec((d, tn), lambda i, j: (0, j)),
            pl.BlockSpec((d, LANES), lambda i, j: (0, 0)),
        ],
        out_specs=[
            pl.BlockSpec((tm, tn), lambda i, j: (i, j)),
            pl.BlockSpec((tm, LANES), lambda i, j: (i, 0)),
        ],
        out_shape=[jax.ShapeDtypeStruct((n_tok, N_MAIN), F32),
                   jax.ShapeDtypeStruct((n_tok, LANES), F32)],
        scratch_shapes=[pltpu.VMEM((tm, d), BF16)],
        compiler_params=pltpu.CompilerParams(
            dimension_semantics=("arbitrary", "arbitrary"), vmem_limit_bytes=VMEM_LIMIT),
        name="in_proj",
    )(x2d, ln_g, w_main, w_small)


def _gla_kernel(q_ref, k_ref, v_ref, gate_ref, sm_ref, wal_ref, bal_ref, ng_ref, s0_ref,
                o_ref, so_ref, s_ref, *, c, nc):
    n = pl.program_id(2)

    @pl.when(n == 0)
    def _():
        s_ref[...] = s0_ref[...]

    incl, _ = _tri_incl(c)
    tri = incl.astype(F32)
    for ci in range(nc):
        rows = slice(ci * c, (ci + 1) * c)
        q = q_ref[rows, :] * (GLA_DK ** -0.5)
        k = k_ref[rows, :]
        v = v_ref[rows, :]
        lr = sm_ref[rows, SM_LR:SM_LR + GLA_RANK]
        z = _bdot(lr, wal_ref[...]) + bal_ref[...]
        la = -_softplus(-z) / GLA_TAU
        b = _hdot(tri, la)
        b_last = b[c - 1:c, :]
        q_d = q * jnp.exp(b)
        k_d = k * jnp.exp(-b)
        att = jnp.where(incl, _bdot_nt(q_d, k_d), 0.0)
        s = s_ref[...]
        o = _bdot(att, v) + _bdot(q_d, s)
        k_end = k * jnp.exp(b_last - b)
        kv = _bdot_tn(k_end, v)
        col = jnp.transpose(jnp.broadcast_to(jnp.exp(b_last), (LANES, GLA_DK)))
        for jb in range(GLA_DV // LANES):
            ls = slice(jb * LANES, (jb + 1) * LANES)
            s_ref[:, ls] = s[:, ls] * col + kv[:, ls]
        inv = lax.rsqrt(jnp.mean(o * o, axis=-1, keepdims=True) + EPS)
        on = o * inv * ng_ref[...]
        o_ref[rows, :] = (on * _silu(gate_ref[rows, :])).astype(o_ref.dtype)

    @pl.when(n == pl.num_programs(2) - 1)
    def _():
        so_ref[...] = s_ref[...]


def _gla(proj, small, w_alpha2, b_alpha, norm_g, s0, *, bsz, t, c, nc, out_dtype):
    r = c * nc
    nsteps = t // r
    n_tok = proj.shape[0]
    qb, kb = COL_QA // GLA_DK, COL_KA // GLA_DK
    vb, gb = COL_VA // GLA_DV, COL_GATE_A // GLA_DV
    row = lambda b, h, n: b * nsteps + n
    return pl.pallas_call(
        functools.partial(_gla_kernel, c=c, nc=nc),
        grid=(bsz, GLA_HEADS, nsteps),
        in_specs=[
            pl.BlockSpec((r, GLA_DK), lambda b, h, n: (row(b, h, n), qb + h)),
            pl.BlockSpec((r, GLA_DK), lambda b, h, n: (row(b, h, n), kb + h)),
            pl.BlockSpec((r, GLA_DV), lambda b, h, n: (row(b, h, n), vb + h)),
            pl.BlockSpec((r, GLA_DV), lambda b, h, n: (row(b, h, n), gb + h)),
            pl.BlockSpec((r, LANES), lambda b, h, n: (row(b, h, n), 0)),
            pl.BlockSpec((GLA_RANK, GLA_DK), lambda b, h, n: (0, h)),
            pl.BlockSpec((1, GLA_DK), lambda b, h, n: (0, h)),
            pl.BlockSpec((1, GLA_DV), lambda b, h, n: (0, 0)),
            pl.BlockSpec((None, None, GLA_DK, GLA_DV), lambda b, h, n: (b, h, 0, 0)),
        ],
        out_specs=[
            pl.BlockSpec((r, GLA_DV), lambda b, h, n: (row(b, h, n), h)),
            pl.BlockSpec((None, None, GLA_DK, GLA_DV), lambda b, h, n: (b, h, 0, 0)),
        ],
        out_shape=[jax.ShapeDtypeStruct((n_tok, GLA_V), out_dtype),
                   jax.ShapeDtypeStruct((bsz, GLA_HEADS, GLA_DK, GLA_DV), F32)],
        scratch_shapes=[pltpu.VMEM((GLA_DK, GLA_DV), F32)],
        compiler_params=pltpu.CompilerParams(
            dimension_semantics=("arbitrary", "arbitrary", "arbitrary"),
            vmem_limit_bytes=VMEM_LIMIT),
        name="gla",
    )(proj, proj, proj, proj, small, w_alpha2, b_alpha, norm_g, s0)


def _solve_unit_lower(m, rhs, c):
    nm = -m
    x = rhs + _hdot(nm, rhs)
    p = 1
    while 2 * p < c:
        nm = _hdot(nm, nm)
        p *= 2
        x = x + _hdot(nm, x)
    return x


def _gdn_kernel(q_ref, k_ref, v_ref, gate_ref, sm_ref, cwq_ref, cwk_ref, cwv_ref, alog_ref,
                dtb_ref, ng_ref, cq0_ref, ck0_ref, cv0_ref, s0_ref,
                o_ref, so_ref,
                s_ref, eq_ref, ek_ref, ev_ref, cq_ref, ck_ref, cv_ref, *, c, nc):
    n = pl.program_id(1)
    r = c * nc
    tail = CONV_W - 1
    pad = 8

    @pl.when(n == 0)
    def _():
        s_ref[...] = s0_ref[...]
        eq_ref[pad - tail:pad, :] = cq0_ref[...]
        ek_ref[pad - tail:pad, :] = ck0_ref[...]
        ev_ref[pad - tail:pad, :] = cv0_ref[...]

    def conv(e_ref, x_ref, w_ref, out_ref):
        e_ref[pad:pad + r, :] = x_ref[...]
        acc = e_ref[pad - tail:pad - tail + r, :] * w_ref[0:1, :]
        for i in range(1, CONV_W):
            acc = acc + e_ref[pad - tail + i:pad - tail + i + r, :] * w_ref[i:i + 1, :]
        e_ref[pad - tail:pad, :] = x_ref[r - tail:r, :]
        out_ref[...] = _silu(acc)

    conv(eq_ref, q_ref, cwq_ref, cq_ref)
    conv(ek_ref, k_ref, cwk_ref, ck_ref)
    conv(ev_ref, v_ref, cwv_ref, cv_ref)

    incl, strict = _tri_incl(c)
    tri = incl.astype(F32)
    for ci in range(nc):
        rows = slice(ci * c, (ci + 1) * c)
        sm = sm_ref[rows, :]
        beta_all = _sigmoid(sm)
        g_all = -jnp.exp(alog_ref[...]) * _softplus(sm + dtb_ref[...])
        gh_all = _hdot(tri, g_all)
        gh_pad = jnp.concatenate([gh_all, jnp.zeros((LANES - c, LANES), F32)], axis=0)
        gh_t = jnp.transpose(gh_pad)
        for h in range(GDN_HEADS):
            ls = slice(h * GDN_DK, (h + 1) * GDN_DK)
            q = cq_ref[rows, ls]
            k = ck_ref[rows, ls]
            v = cv_ref[rows, ls]
            q = q * lax.rsqrt(jnp.sum(q * q, axis=-1, keepdims=True) + EPS) * (GDN_DK ** -0.5)
            k = k * lax.rsqrt(jnp.sum(k * k, axis=-1, keepdims=True) + EPS)
            beta = beta_all[:, SM_BETA + h:SM_BETA + h + 1]
            gh = gh_all[:, SM_DEC + h:SM_DEC + h + 1]
            gh_row = gh_t[SM_DEC + h:SM_DEC + h + 1, 0:c]
            g_last = gh[c - 1:c, :]
            decay = jnp.where(incl, jnp.exp(gh - gh_row), 0.0)
            kk = _bdot_nt(k, k)
            m = jnp.where(strict, beta * kk * decay, 0.0)
            egh = jnp.exp(gh)
            rhs = jnp.concatenate([v * beta, k * (beta * egh)], axis=1)
            sol = _solve_unit_lower(m, rhs, c)
            u, w = sol[:, :GDN_DV], sol[:, GDN_DV:]
            s = s_ref[h]
            v_new = u - _bdot(w, s)
            qk = _bdot_nt(q, k) * decay
            o = _bdot(q * egh, s) + _bdot(qk, v_new)
            k_end = k * jnp.exp(g_last - gh)
            s_ref[h] = s * jnp.exp(g_last) + _bdot_tn(k_end, v_new)
            inv = lax.rsqrt(jnp.mean(o * o, axis=-1, keepdims=True) + EPS)
            on = o * inv * ng_ref[...]
            o_ref[rows, ls] = (on * _silu(gate_ref[rows, ls])).astype(o_ref.dtype)

    @pl.when(n == pl.num_programs(1) - 1)
    def _():
        so_ref[...] = s_ref[...]


def _gdn(proj, small, conv_w, alog_pad, dtb_pad, norm_g, conv0, s0, *, bsz, t, c, nc,
         out_dtype):
    r = c * nc
    nsteps = t // r
    n_tok = proj.shape[0]
    qb = COL_QKV_B // GDN_QK
    gb = COL_GATE_B // GDN_V
    tail = CONV_W - 1
    row = lambda b, n: b * nsteps + n
    rowspec = lambda cb: pl.BlockSpec((r, GDN_QK), lambda b, n: (row(b, n), cb))
    cwspec = lambda cb: pl.BlockSpec((CONV_W, GDN_QK), lambda b, n: (0, cb))
    c0spec = lambda cb: pl.BlockSpec((None, tail, GDN_QK), lambda b, n: (b, 0, cb))
    vec = pl.BlockSpec((1, LANES), lambda b, n: (0, 0))
    state = pl.BlockSpec((None, GDN_HEADS, GDN_DK, GDN_DV), lambda b, n: (b, 0, 0, 0))
    return pl.pallas_call(
        functools.partial(_gdn_kernel, c=c, nc=nc),
        grid=(bsz, nsteps),
        in_specs=[rowspec(qb), rowspec(qb + 1), rowspec(qb + 2), rowspec(gb),
                  pl.BlockSpec((r, LANES), lambda b, n: (row(b, n), 0)),
                  cwspec(0), cwspec(1), cwspec(2), vec, vec, vec,
                  c0spec(0), c0spec(1), c0spec(2), state],
        out_specs=[pl.BlockSpec((r, GDN_V), lambda b, n: (row(b, n), 0)), state],
        out_shape=[jax.ShapeDtypeStruct((n_tok, GDN_V), out_dtype),
                   jax.ShapeDtypeStruct((bsz, GDN_HEADS, GDN_DK, GDN_DV), F32)],
        scratch_shapes=[pltpu.VMEM((GDN_HEADS, GDN_DK, GDN_DV), F32)]
        + [pltpu.VMEM((8 + r, GDN_QK), F32)] * 3
        + [pltpu.VMEM((r, GDN_QK), F32)] * 3,
        compiler_params=pltpu.CompilerParams(
            dimension_semantics=("arbitrary", "arbitrary"), vmem_limit_bytes=VMEM_LIMIT),
        name="gdn",
    )(proj, proj, proj, proj, small, conv_w, conv_w, conv_w, alog_pad, dtb_pad, norm_g,
      conv0, conv0, conv0, s0)


def _merge_kernel(oa_ref, ob_ref, wa_ref, wb_ref, ma_ref, mb_ref, o_ref):
    ya = jnp.dot(oa_ref[...].astype(BF16), wa_ref[...], preferred_element_type=F32)
    yb = jnp.dot(ob_ref[...].astype(BF16), wb_ref[...], preferred_element_type=F32)
    merged = _sigmoid(ma_ref[...]) * ya + _sigmoid(mb_ref[...]) * yb
    o_ref[...] = merged.astype(o_ref.dtype)


def _merge(o_a, o_b, w_br_a, w_br_b, proj):
    n_tok = o_a.shape[0]
    tm = min(1024, n_tok)
    tn = 512
    ma, mb = COL_MA // tn, COL_MB // tn
    return pl.pallas_call(
        _merge_kernel,
        grid=(n_tok // tm, D_MODEL // tn),
        in_specs=[
            pl.BlockSpec((tm, GLA_V), lambda i, j: (i, 0)),
            pl.BlockSpec((tm, GDN_V), lambda i, j: (i, 0)),
            pl.BlockSpec((GLA_V, tn), lambda i, j: (0, j)),
            pl.BlockSpec((GDN_V, tn), lambda i, j: (0, j)),
            pl.BlockSpec((tm, tn), lambda i, j: (i, ma + j)),
            pl.BlockSpec((tm, tn), lambda i, j: (i, mb + j)),
        ],
        out_specs=pl.BlockSpec((tm, tn), lambda i, j: (i, j)),
        out_shape=jax.ShapeDtypeStruct((n_tok, D_MODEL), BF16),
        compiler_params=pltpu.CompilerParams(
            dimension_semantics=("arbitrary", "arbitrary"), vmem_limit_bytes=VMEM_LIMIT),
        name="merge",
    )(o_a, o_b, w_br_a, w_br_b, proj, proj)


def _out_kernel(m_ref, w_ref, x_ref, g_ref, y_ref):
    out = x_ref[...] + jnp.dot(m_ref[...], w_ref[...], preferred_element_type=F32)
    inv = lax.rsqrt(jnp.mean(out * out, axis=-1, keepdims=True) + EPS)
    y_ref[...] = out * inv * g_ref[...]


def _out_proj(merged, w_out, x2d, final_g):
    n_tok = merged.shape[0]
    tm = min(512, n_tok)
    return pl.pallas_call(
        _out_kernel,
        grid=(n_tok // tm,),
        in_specs=[
            pl.BlockSpec((tm, D_MODEL), lambda i: (i, 0)),
            pl.BlockSpec((D_MODEL, D_MODEL), lambda i: (0, 0)),
            pl.BlockSpec((tm, D_MODEL), lambda i: (i, 0)),
            pl.BlockSpec((1, D_MODEL), lambda i: (0, 0)),
        ],
        out_specs=pl.BlockSpec((tm, D_MODEL), lambda i: (i, 0)),
        out_shape=jax.ShapeDtypeStruct((n_tok, D_MODEL), F32),
        compiler_params=pltpu.CompilerParams(
            dimension_semantics=("arbitrary",), vmem_limit_bytes=VMEM_LIMIT),
        name="out_proj",
    )(merged, w_out, x2d, final_g)


def _layer(x, s_gla, s_gdn, conv_buf, p, *, c, nc_gla, nc_gdn, act_dtype):
    bsz, t, d = x.shape
    x2d = x.reshape(bsz * t, d)
    proj, small = _in_proj(x2d, p["ln_g"], p["w_main"], p["w_small"])
    o_a, s_gla_new = _gla(proj, small, p["w_alpha2"], p["b_alpha"], p["gla_norm_g"], s_gla,
                          bsz=bsz, t=t, c=c, nc=nc_gla, out_dtype=act_dtype)
    o_b, s_gdn_new = _gdn(proj, small, p["conv_w"], p["alog_pad"], p["dtb_pad"],
                          p["gdn_norm_g"], conv_buf, s_gdn,
                          bsz=bsz, t=t, c=c, nc=nc_gdn, out_dtype=act_dtype)
    merged = _merge(o_a, o_b, p["w_br_a"], p["w_br_b"], proj)
    y = _out_proj(merged, p["w_out"], x2d, p["final_g"])
    tail = CONV_W - 1
    qkv = proj.reshape(bsz, t, N_MAIN)[:, t - tail:, COL_QKV_B:COL_QKV_B + CONV_CH]
    return y.reshape(bsz, t, d), s_gla_new, s_gdn_new, qkv


def _prep_params(ln_in_g, w_in, w_alpha2, b_alpha, conv_w, a_log, dt_bias, gla_norm_g,
                 gdn_norm_g, w_br_a, w_br_b, w_out, final_norm_g):
    sizes = (GLA_QK, GLA_QK, GLA_V, GLA_RANK, GLA_V, CONV_CH, GDN_HEADS, GDN_HEADS, GDN_V,
             D_MODEL, D_MODEL)
    offs = [0]
    for s in sizes:
        offs.append(offs[-1] + s)
    seg = lambda i: w_in[:, offs[i]:offs[i + 1]]
    w_main = jnp.concatenate([seg(0), seg(1), seg(2), seg(4), seg(5), seg(8), seg(9), seg(10)],
                             axis=1).astype(BF16)
    n_small = GLA_RANK + 2 * GDN_HEADS
    w_small = jnp.concatenate(
        [seg(3), seg(6), seg(7), jnp.zeros((D_MODEL, LANES - n_small), F32)],
        axis=1).astype(BF16)
    lane_pad = lambda v: jnp.zeros((1, LANES), F32).at[0, SM_DEC:SM_DEC + GDN_HEADS].set(v)
    return {
        "ln_g": ln_in_g.reshape(1, D_MODEL),
        "w_main": w_main,
        "w_small": w_small,
        "w_alpha2": w_alpha2,
        "b_alpha": b_alpha.reshape(1, GLA_QK),
        "conv_w": conv_w,
        "alog_pad": lane_pad(a_log),
        "dtb_pad": lane_pad(dt_bias),
        "gla_norm_g": gla_norm_g.reshape(1, GLA_DV),
        "gdn_norm_g": gdn_norm_g.reshape(1, GDN_DV),
        "w_br_a": w_br_a.astype(BF16),
        "w_br_b": w_br_b.astype(BF16),
        "w_out": w_out.astype(BF16),
        "final_g": final_norm_g.reshape(1, D_MODEL),
    }


def kernel(x_prompt, x_sample, state_gla, state_gdn, state_conv, ln_in_g, w_in, w_alpha2,
           b_alpha, conv_w, a_log, dt_bias, gla_norm_g, gdn_norm_g, w_br_a, w_br_b, w_out,
           final_norm_g):
    assert ln_in_g.shape[0] == 1, "single layer"
    p = _prep_params(ln_in_g[0], w_in[0], w_alpha2[0], b_alpha[0], conv_w[0], a_log[0],
                     dt_bias[0], gla_norm_g[0], gdn_norm_g[0], w_br_a[0], w_br_b[0], w_out[0],
                     final_norm_g)
    bp, tp, _ = x_prompt.shape
    bs, ts, _ = x_sample.shape
    tail = CONV_W - 1
    z_gla = jnp.zeros((bp, GLA_HEADS, GLA_DK, GLA_DV), F32)
    z_gdn = jnp.zeros((bp, GDN_HEADS, GDN_DK, GDN_DV), F32)
    z_conv = jnp.zeros((bp, tail, CONV_CH), F32)
    cp = min(CHUNK, tp)
    yp, gla_p, gdn_p, conv_p = _layer(x_prompt, z_gla, z_gdn, z_conv, p,
                                      c=cp, nc_gla=4, nc_gdn=1, act_dtype=BF16)
    cs = min(CHUNK, ts)
    ys, gla_s, gdn_s, conv_s = _layer(x_sample, state_gla[0], state_gdn[0], state_conv[0], p,
                                      c=cs, nc_gla=1, nc_gdn=1, act_dtype=F32)
    return (yp, ys, gla_p[None], gdn_p[None], conv_p[None], gla_s[None], gdn_s[None],
            conv_s[None])
```

```python
import functools

import jax
import jax.numpy as jnp
from jax import lax
from jax.experimental import pallas as pl
from jax.experimental.pallas import tpu as pltpu

F32 = jnp.float32
BF16 = jnp.bfloat16

D_MODEL = 2048
GLA_HEADS = 4
GLA_DK = 256
GLA_DV = 512
GLA_RANK = 16
GLA_TAU = 16.0
GDN_HEADS = 16
GDN_DK = 128
GDN_DV = 128
CONV_W = 4
CHUNK = 64
EPS = 1e-6

GLA_QK = GLA_HEADS * GLA_DK
GLA_V = GLA_HEADS * GLA_DV
GDN_QK = GDN_HEADS * GDN_DK
GDN_V = GDN_HEADS * GDN_DV
CONV_CH = 2 * GDN_QK + GDN_V

LANES = 128
COL_QA = 0
COL_KA = COL_QA + GLA_QK
COL_VA = COL_KA + GLA_QK
COL_GATE_A = COL_VA + GLA_V
COL_QKV_B = COL_GATE_A + GLA_V
COL_GATE_B = COL_QKV_B + CONV_CH
COL_MA = COL_GATE_B + GDN_V
COL_MB = COL_MA + D_MODEL
N_MAIN = COL_MB + D_MODEL
SM_LR = 0
SM_BETA = GLA_RANK
SM_DEC = SM_BETA + GDN_HEADS

VMEM_LIMIT = 56 * 1024 * 1024


def _bdot(a, b):
    return jnp.dot(a.astype(BF16), b.astype(BF16), preferred_element_type=F32)


def _bdot_nt(a, b):
    return lax.dot_general(a.astype(BF16), b.astype(BF16), (((1,), (1,)), ((), ())),
                           preferred_element_type=F32)


def _bdot_tn(a, b):
    return lax.dot_general(a.astype(BF16), b.astype(BF16), (((0,), (0,)), ((), ())),
                           preferred_element_type=F32)


def _hdot(a, b):
    return jnp.dot(a, b, precision=lax.Precision.HIGHEST, preferred_element_type=F32)


def _dot(a, b):
    return jnp.dot(a, b, preferred_element_type=F32)


def _sigmoid(x):
    return 1.0 / (1.0 + jnp.exp(-x))


def _silu(x):
    return x * _sigmoid(x)


def _softplus(x):
    return jnp.maximum(x, 0.0) + jnp.log1p(jnp.exp(-jnp.abs(x)))


def _tri_incl(c):
    r = lax.broadcasted_iota(jnp.int32, (c, c), 0)
    col = lax.broadcasted_iota(jnp.int32, (c, c), 1)
    return r >= col, r > col


def _in_proj_kernel(x_ref, g_ref, w_ref, ws_ref, o_ref, os_ref, h_ref):
    @pl.when(pl.program_id(1) == 0)
    def _():
        x = x_ref[...]
        inv = lax.rsqrt(jnp.mean(x * x, axis=-1, keepdims=True) + EPS)
        h = (x * inv * g_ref[...]).astype(BF16)
        h_ref[...] = h
        os_ref[...] = jnp.dot(h, ws_ref[...], preferred_element_type=F32)

    o_ref[...] = jnp.dot(h_ref[...], w_ref[...], preferred_element_type=F32)


def _in_proj(x2d, ln_g, w_main, w_small):
    n_tok, d = x2d.shape
    tm = min(1024, n_tok)
    tn = 1024
    return pl.pallas_call(
        _in_proj_kernel,
        grid=(n_tok // tm, N_MAIN // tn),
        in_specs=[
            pl.BlockSpec((tm, d), lambda i, j: (i, 0)),
            pl.BlockSpec((1, d), lambda i, j: (0, 0)),
            pl.BlockSpec((d, tn), lambda i, j: (0, j)),
            pl.BlockSpec((d, LANES), lambda i, j: (0, 0)),
        ],
        out_specs=[
            pl.BlockSpec((tm, tn), lambda i, j: (i, j)),
            pl.BlockSpec((tm, LANES), lambda i, j: (i, 0)),
        ],
        out_shape=[jax.ShapeDtypeStruct((n_tok, N_MAIN), F32),
                   jax.ShapeDtypeStruct((n_tok, LANES), F32)],
        scratch_shapes=[pltpu.VMEM((tm, d), BF16)],
        compiler_params=pltpu.CompilerParams(
            dimension_semantics=("arbitrary", "arbitrary"), vmem_limit_bytes=VMEM_LIMIT),
        name="in_proj",
    )(x2d, ln_g, w_main, w_small)


def _gla_kernel(q_ref, k_ref, v_ref, gate_ref, sm_ref, wal_ref, bal_ref, ng_ref, s0_ref,
                o_ref, so_ref, s_ref, *, c, nc):
    n = pl.program_id(2)

    @pl.when(n == 0)
    def _():
        s_ref[...] = s0_ref[...]

    incl, _ = _tri_incl(c)
    tri = incl.astype(F32)
    for ci in range(nc):
        rows = slice(ci * c, (ci + 1) * c)
        q = q_ref[rows, :] * (GLA_DK ** -0.5)
        k = k_ref[rows, :]
        v = v_ref[rows, :]
        lr = sm_ref[rows, SM_LR:SM_LR + GLA_RANK]
        z = _bdot(lr, wal_ref[...]) + bal_ref[...]
        la = -_softplus(-z) / GLA_TAU
        b = _hdot(tri, la)
        b_last = b[c - 1:c, :]
        q_d = q * jnp.exp(b)
        k_d = k * jnp.exp(-b)
        att = jnp.where(incl, _bdot_nt(q_d, k_d), 0.0)
        s = s_ref[...]
        o = _bdot(att, v) + _bdot(q_d, s)
        k_end = k * jnp.exp(b_last - b)
        kv = _bdot_tn(k_end, v)
        col = jnp.transpose(jnp.broadcast_to(jnp.exp(b_last), (LANES, GLA_DK)))
        for jb in range(GLA_DV // LANES):
            ls = slice(jb * LANES, (jb + 1) * LANES)
            s_ref[:, ls] = s[:, ls] * col + kv[:, ls]
        inv = lax.rsqrt(jnp.mean(o * o, axis=-1, keepdims=True) + EPS)
        on = o * inv * ng_ref[...]
        o_ref[rows, :] = (on * _silu(gate_ref[rows, :])).astype(o_ref.dtype)

    @pl.when(n == pl.num_programs(2) - 1)
    def _():
        so_ref[...] = s_ref[...]


def _gla(proj, small, w_alpha2, b_alpha, norm_g, s0, *, bsz, t, c, nc, out_dtype):
    r = c * nc
    nsteps = t // r
    n_tok = proj.shape[0]
    qb, kb = COL_QA // GLA_DK, COL_KA // GLA_DK
    vb, gb = COL_VA // GLA_DV, COL_GATE_A // GLA_DV
    row = lambda b, h, n: b * nsteps + n
    return pl.pallas_call(
        functools.partial(_gla_kernel, c=c, nc=nc),
        grid=(bsz, GLA_HEADS, nsteps),
        in_specs=[
            pl.BlockSpec((r, GLA_DK), lambda b, h, n: (row(b, h, n), qb + h)),
            pl.BlockSpec((r, GLA_DK), lambda b, h, n: (row(b, h, n), kb + h)),
            pl.BlockSpec((r, GLA_DV), lambda b, h, n: (row(b, h, n), vb + h)),
            pl.BlockSpec((r, GLA_DV), lambda b, h, n: (row(b, h, n), gb + h)),
            pl.BlockSpec((r, LANES), lambda b, h, n: (row(b, h, n), 0)),
            pl.BlockSpec((GLA_RANK, GLA_DK), lambda b, h, n: (0, h)),
            pl.BlockSpec((1, GLA_DK), lambda b, h, n: (0, h)),
            pl.BlockSpec((1, GLA_DV), lambda b, h, n: (0, 0)),
            pl.BlockSpec((None, None, GLA_DK, GLA_DV), lambda b, h, n: (b, h, 0, 0)),
        ],
        out_specs=[
            pl.BlockSpec((r, GLA_DV), lambda b, h, n: (row(b, h, n), h)),
            pl.BlockSpec((None, None, GLA_DK, GLA_DV), lambda b, h, n: (b, h, 0, 0)),
        ],
        out_shape=[jax.ShapeDtypeStruct((n_tok, GLA_V), out_dtype),
                   jax.ShapeDtypeStruct((bsz, GLA_HEADS, GLA_DK, GLA_DV), F32)],
        scratch_shapes=[pltpu.VMEM((GLA_DK, GLA_DV), F32)],
        compiler_params=pltpu.CompilerParams(
            dimension_semantics=("arbitrary", "arbitrary", "arbitrary"),
            vmem_limit_bytes=VMEM_LIMIT),
        name="gla",
    )(proj, proj, proj, proj, small, w_alpha2, b_alpha, norm_g, s0)


GDN_ROWS = 64
HALF = LANES // 2


def _split_f32(x):
    hi = x.astype(BF16)
    return hi, x - hi.astype(F32)


def _lhs3(x, low_half):
    hi, lo = _split_f32(x)
    t0 = jnp.where(low_half, x, lo).astype(BF16)
    t1 = jnp.where(low_half, hi, jnp.zeros_like(hi))
    return jnp.concatenate([t0, t1], axis=1)


def _rhs3(hi, lo):
    return jnp.concatenate([hi, hi, lo, lo], axis=0)


def _gdn_kernel(*refs, seq_len, carry):
    n_in = 11 if carry else 15
    (q_ref, k_ref, v_ref, gate_ref, sm_ref, cwq_ref, cwk_ref, cwv_ref, alog_ref, dtb_ref,
     ng_ref) = refs[:11]
    cq0_ref, ck0_ref, cv0_ref, s0_ref = (None,) * 4 if carry else refs[11:15]
    o_ref, so_ref = refs[n_in:n_in + 2]
    (eq_ref, ek_ref, ev_ref, cq_ref, ck_ref, cv_ref, nhi_ref, nlo_ref, nl0_ref, p_ref, att_ref,
     rhi_ref, rlo_ref, qg_ref, ke_ref, u_ref, w_ref, os_ref) = refs[n_in + 2:n_in + 20]
    if carry:
        (s_ref,) = refs[n_in + 20:]
        hq_ref = hk_ref = hv_ref = None
    else:
        hq_ref, hk_ref, hv_ref = refs[n_in + 20:]
        s_ref = None
    r = GDN_ROWS
    L = seq_len
    nseq = r // L
    tail = CONV_W - 1
    pad = 8

    if carry:
        @pl.when(pl.program_id(1) == 0)
        def _():
            s_ref[...] = jnp.zeros_like(s_ref)
            for e_ref in (eq_ref, ek_ref, ev_ref):
                e_ref[0:pad, :] = jnp.zeros((pad, GDN_QK), F32)
    else:
        @pl.when(pl.program_id(0) == 0)
        def _():
            for e_ref in (eq_ref, ek_ref, ev_ref, hq_ref, hk_ref, hv_ref):
                e_ref[...] = jnp.zeros_like(e_ref)

    row_t = lax.broadcasted_iota(jnp.int32, (r, 1), 0) % L

    def conv(e_ref, x_ref, w_ref, out_ref, h_ref, c0_ref):
        e_ref[pad:pad + r, :] = x_ref[...]
        if not carry:
            for qi in range(nseq):
                h_ref[qi * L + pad - tail:qi * L + pad, :] = c0_ref[qi]
        acc = None
        for i in range(CONV_W):
            sh = tail - i
            xs = e_ref[pad - sh:pad - sh + r, :]
            if not carry and sh > 0:
                xs = jnp.where(row_t >= sh, xs, h_ref[pad - sh:pad - sh + r, :])
            term = xs * w_ref[i:i + 1, :]
            acc = term if acc is None else acc + term
        if carry:
            e_ref[pad - tail:pad, :] = x_ref[r - tail:r, :]
        out_ref[...] = _silu(acc)

    conv(eq_ref, q_ref, cwq_ref, cq_ref, hq_ref, cq0_ref)
    conv(ek_ref, k_ref, cwk_ref, ck_ref, hk_ref, ck0_ref)
    conv(ev_ref, v_ref, cwv_ref, cv_ref, hv_ref, cv0_ref)

    ri = lax.broadcasted_iota(jnp.int32, (r, LANES), 0)
    li = lax.broadcasted_iota(jnp.int32, (r, LANES), 1)
    cj = li % HALF
    low_half = li < HALF
    same = (ri // L) == (cj // L)
    incl = same & (ri >= cj)
    strict = same & (ri > cj)
    eye = (ri == cj).astype(F32)

    sm = sm_ref[...]
    beta_all = _sigmoid(sm)
    g_all = -jnp.exp(alog_ref[...]) * _softplus(sm + dtb_ref[...])
    g_hi, g_r1 = _split_f32(g_all)
    g_mid, g_r2 = _split_f32(g_r1)
    g_lo = g_r2.astype(BF16)
    tri = incl.astype(BF16)
    tri_lhs = jnp.concatenate([tri, jnp.where(low_half, tri, jnp.zeros_like(tri))], axis=1)
    gh_all = _dot(tri_lhs, jnp.concatenate([g_hi, g_mid, g_lo, g_lo], axis=0))
    gh_t = jnp.transpose(jnp.concatenate([gh_all, gh_all], axis=0))
    glast_all = jnp.concatenate(
        [jnp.broadcast_to(gh_all[(qi + 1) * L - 1:(qi + 1) * L, :], (L, LANES))
         for qi in range(nseq)], axis=0)
    egh_all = jnp.exp(gh_all)
    kend_all = jnp.exp(glast_all - gh_all)

    for h in range(GDN_HEADS):
        ls = slice(h * GDN_DK, (h + 1) * GDN_DK)
        q = cq_ref[:, ls]
        k = ck_ref[:, ls]
        v = cv_ref[:, ls]
        q = q * lax.rsqrt(jnp.sum(q * q, axis=-1, keepdims=True) + EPS) * (GDN_DK ** -0.5)
        k = k * lax.rsqrt(jnp.sum(k * k, axis=-1, keepdims=True) + EPS)
        col = slice(SM_DEC + h, SM_DEC + h + 1)
        beta = beta_all[:, SM_BETA + h:SM_BETA + h + 1]
        gh = gh_all[:, col]
        egh = egh_all[:, col]
        decay = jnp.where(incl, jnp.exp(gh - gh_t[col, :]), 0.0)
        kb = k.astype(BF16)
        qkk = lax.dot_general(jnp.concatenate([q.astype(BF16), kb], axis=0),
                              jnp.concatenate([kb, kb], axis=0),
                              (((1,), (1,)), ((), ())), preferred_element_type=F32)
        att_ref[h] = (qkk[:r] * decay).astype(BF16)
        nm = jnp.where(strict, -(beta * qkk[r:] * decay), 0.0)
        n_hi, n_lo = _split_f32(nm)
        nhi_ref[h] = n_hi
        nlo_ref[h] = n_lo.astype(BF16)
        nl0_ref[h] = jnp.where(low_half, nm, n_lo).astype(BF16)
        p_ref[h] = eye + nm
        r_hi, r_lo = _split_f32(jnp.concatenate([v * beta, k * (beta * egh)], axis=1))
        rhi_ref[h] = r_hi
        rlo_ref[h] = r_lo.astype(BF16)
        qg_ref[h] = q * egh
        ke_ref[h] = k * kend_all[:, col]

    def n_lhs(h):
        n_hi = nhi_ref[h]
        return jnp.concatenate([nl0_ref[h], jnp.where(low_half, n_hi, jnp.zeros_like(n_hi))],
                               axis=1)

    p = 2
    while p < L:
        for h in range(GDN_HEADS):
            n2 = _dot(n_lhs(h), _rhs3(nhi_ref[h], nlo_ref[h]))
            n_hi, n_lo = _split_f32(n2)
            nhi_ref[h] = n_hi
            nlo_ref[h] = n_lo.astype(BF16)
            nl0_ref[h] = jnp.where(low_half, n2, n_lo).astype(BF16)
        for h in range(GDN_HEADS):
            pm = p_ref[h]
            p_ref[h] = pm + _dot(_lhs3(pm, low_half), _rhs3(nhi_ref[h], nlo_ref[h]))
        p *= 2

    for h in range(GDN_HEADS):
        sol = _dot(_lhs3(p_ref[h], low_half), _rhs3(rhi_ref[h], rlo_ref[h]))
        u_ref[h] = sol[:, :GDN_DV]
        w_ref[h] = sol[:, GDN_DV:]

    def state(qi, h):
        return s_ref[h] if carry else s0_ref[qi, h]

    for h in range(GDN_HEADS):
        for qi in range(nseq):
            rs = slice(qi * L, (qi + 1) * L)
            lhs = jnp.concatenate([w_ref[h, rs, :], qg_ref[h, rs, :]], axis=0).astype(BF16)
            res = _dot(lhs, state(qi, h).astype(BF16))
            u_ref[h, rs, :] = u_ref[h, rs, :] - res[:L]
            os_ref[h, rs, :] = res[L:]

    for h in range(GDN_HEADS):
        ls = slice(h * GDN_DV, (h + 1) * GDN_DV)
        v_new = u_ref[h]
        o = os_ref[h] + _dot(att_ref[h][:, :HALF], v_new.astype(BF16))
        inv = lax.rsqrt(jnp.mean(o * o, axis=-1, keepdims=True) + EPS)
        on = o * inv * ng_ref[...]
        o_ref[:, ls] = (on * _silu(gate_ref[:, ls])).astype(o_ref.dtype)
        for qi in range(nseq):
            rs = slice(qi * L, (qi + 1) * L)
            kv = lax.dot_general(ke_ref[h, rs, :].astype(BF16), v_new[rs].astype(BF16),
                                 (((0,), (0,)), ((), ())), preferred_element_type=F32)
            eg = egh_all[(qi + 1) * L - 1:(qi + 1) * L, SM_DEC + h:SM_DEC + h + 1]
            s_new = state(qi, h) * eg + kv
            if carry:
                s_ref[h] = s_new
            else:
                so_ref[qi, h] = s_new

    if carry:
        @pl.when(pl.program_id(1) == pl.num_programs(1) - 1)
        def _():
            so_ref[...] = s_ref[...]


def _gdn(proj, small, conv_w, alog_pad, dtb_pad, norm_g, conv0, s0, *, bsz, t):
    carry = s0 is None
    r = GDN_ROWS
    n_tok = proj.shape[0]
    tail = CONV_W - 1
    if carry:
        assert t % r == 0
        seq_len, nseq, grid = r, 1, (bsz, t // r)
        row = lambda b, n: b * (t // r) + n
    else:
        assert r % t == 0 and bsz % (r // t) == 0
        seq_len, nseq, grid = t, r // t, (bsz // (r // t), 1)
        row = lambda b, n: b
    qb = COL_QKV_B // GDN_QK
    gb = COL_GATE_B // GDN_V
    rowspec = lambda cb: pl.BlockSpec((r, GDN_QK), lambda b, n: (row(b, n), cb))
    cwspec = lambda cb: pl.BlockSpec((CONV_W, GDN_QK), lambda b, n: (0, cb))
    vec = pl.BlockSpec((1, LANES), lambda b, n: (0, 0))
    in_specs = [rowspec(qb), rowspec(qb + 1), rowspec(qb + 2), rowspec(gb),
                pl.BlockSpec((r, LANES), lambda b, n: (row(b, n), 0)),
                cwspec(0), cwspec(1), cwspec(2), vec, vec, vec]
    args = [proj, proj, proj, proj, small, conv_w, conv_w, conv_w, alog_pad, dtb_pad, norm_g]
    hshape = (GDN_HEADS, r, LANES)
    scratch = ([pltpu.VMEM((8 + r, GDN_QK), F32)] * 3 + [pltpu.VMEM((r, GDN_QK), F32)] * 3
               + [pltpu.VMEM(hshape, BF16)] * 3 + [pltpu.VMEM(hshape, F32)]
               + [pltpu.VMEM(hshape, BF16)]
               + [pltpu.VMEM((GDN_HEADS, r, 2 * LANES), BF16)] * 2
               + [pltpu.VMEM(hshape, F32)] * 5)
    if carry:
        state = pl.BlockSpec((None, GDN_HEADS, GDN_DK, GDN_DV), lambda b, n: (b, 0, 0, 0))
        scratch += [pltpu.VMEM((GDN_HEADS, GDN_DK, GDN_DV), F32)]
    else:
        state = pl.BlockSpec((nseq, GDN_HEADS, GDN_DK, GDN_DV), lambda b, n: (b, 0, 0, 0))
        c0spec = lambda cb: pl.BlockSpec((nseq, tail, GDN_QK), lambda b, n: (b, 0, cb))
        in_specs += [c0spec(0), c0spec(1), c0spec(2), state]
        args += [conv0, conv0, conv0, s0]
        scratch += [pltpu.VMEM((8 + r, GDN_QK), F32)] * 3
    return pl.pallas_call(
        functools.partial(_gdn_kernel, seq_len=seq_len, carry=carry),
        grid=grid,
        in_specs=in_specs,
        out_specs=[pl.BlockSpec((r, GDN_V), lambda b, n: (row(b, n), 0)), state],
        out_shape=[jax.ShapeDtypeStruct((n_tok, GDN_V), BF16),
                   jax.ShapeDtypeStruct((bsz, GDN_HEADS, GDN_DK, GDN_DV), F32)],
        scratch_shapes=scratch,
        compiler_params=pltpu.CompilerParams(
            dimension_semantics=("arbitrary", "arbitrary"), vmem_limit_bytes=VMEM_LIMIT),
        name="gdn",
    )(*args)


def _merge_kernel(oa_ref, ob_ref, wa_ref, wb_ref, ma_ref, mb_ref, o_ref):
    ya = jnp.dot(oa_ref[...].astype(BF16), wa_ref[...], preferred_element_type=F32)
    yb = jnp.dot(ob_ref[...].astype(BF16), wb_ref[...], preferred_element_type=F32)
    merged = _sigmoid(ma_ref[...]) * ya + _sigmoid(mb_ref[...]) * yb
    o_ref[...] = merged.astype(o_ref.dtype)


def _merge(o_a, o_b, w_br_a, w_br_b, proj):
    n_tok = o_a.shape[0]
    tm = min(1024, n_tok)
    tn = 512
    ma, mb = COL_MA // tn, COL_MB // tn
    return pl.pallas_call(
        _merge_kernel,
        grid=(n_tok // tm, D_MODEL // tn),
        in_specs=[
            pl.BlockSpec((tm, GLA_V), lambda i, j: (i, 0)),
            pl.BlockSpec((tm, GDN_V), lambda i, j: (i, 0)),
            pl.BlockSpec((GLA_V, tn), lambda i, j: (0, j)),
            pl.BlockSpec((GDN_V, tn), lambda i, j: (0, j)),
            pl.BlockSpec((tm, tn), lambda i, j: (i, ma + j)),
            pl.BlockSpec((tm, tn), lambda i, j: (i, mb + j)),
        ],
        out_specs=pl.BlockSpec((tm, tn), lambda i, j: (i, j)),
        out_shape=jax.ShapeDtypeStruct((n_tok, D_MODEL), BF16),
        compiler_params=pltpu.CompilerParams(
            dimension_semantics=("arbitrary", "arbitrary"), vmem_limit_bytes=VMEM_LIMIT),
        name="merge",
    )(o_a, o_b, w_br_a, w_br_b, proj, proj)


def _out_kernel(m_ref, w_ref, x_ref, g_ref, y_ref):
    out = x_ref[...] + jnp.dot(m_ref[...], w_ref[...], preferred_element_type=F32)
    inv = lax.rsqrt(jnp.mean(out * out, axis=-1, keepdims=True) + EPS)
    y_ref[...] = out * inv * g_ref[...]


def _out_proj(merged, w_out, x2d, final_g):
    n_tok = merged.shape[0]
    tm = min(512, n_tok)
    return pl.pallas_call(
        _out_kernel,
        grid=(n_tok // tm,),
        in_specs=[
            pl.BlockSpec((tm, D_MODEL), lambda i: (i, 0)),
            pl.BlockSpec((D_MODEL, D_MODEL), lambda i: (0, 0)),
            pl.BlockSpec((tm, D_MODEL), lambda i: (i, 0)),
            pl.BlockSpec((1, D_MODEL), lambda i: (0, 0)),
        ],
        out_specs=pl.BlockSpec((tm, D_MODEL), lambda i: (i, 0)),
        out_shape=jax.ShapeDtypeStruct((n_tok, D_MODEL), F32),
        compiler_params=pltpu.CompilerParams(
            dimension_semantics=("arbitrary",), vmem_limit_bytes=VMEM_LIMIT),
        name="out_proj",
    )(merged, w_out, x2d, final_g)


def _layer(x, s_gla, s_gdn, conv_buf, p, *, c, nc_gla, act_dtype):
    bsz, t, d = x.shape
    x2d = x.reshape(bsz * t, d)
    proj, small = _in_proj(x2d, p["ln_g"], p["w_main"], p["w_small"])
    o_a, s_gla_new = _gla(proj, small, p["w_alpha2"], p["b_alpha"], p["gla_norm_g"], s_gla,
                          bsz=bsz, t=t, c=c, nc=nc_gla, out_dtype=act_dtype)
    o_b, s_gdn_new = _gdn(proj, small, p["conv_w"], p["alog_pad"], p["dtb_pad"],
                          p["gdn_norm_g"], conv_buf, s_gdn, bsz=bsz, t=t)
    merged = _merge(o_a, o_b, p["w_br_a"], p["w_br_b"], proj)
    y = _out_proj(merged, p["w_out"], x2d, p["final_g"])
    tail = CONV_W - 1
    qkv = proj.reshape(bsz, t, N_MAIN)[:, t - tail:, COL_QKV_B:COL_QKV_B + CONV_CH]
    return y.reshape(bsz, t, d), s_gla_new, s_gdn_new, qkv


def _prep_params(ln_in_g, w_in, w_alpha2, b_alpha, conv_w, a_log, dt_bias, gla_norm_g,
                 gdn_norm_g, w_br_a, w_br_b, w_out, final_norm_g):
    sizes = (GLA_QK, GLA_QK, GLA_V, GLA_RANK, GLA_V, CONV_CH, GDN_HEADS, GDN_HEADS, GDN_V,
             D_MODEL, D_MODEL)
    offs = [0]
    for s in sizes:
        offs.append(offs[-1] + s)
    seg = lambda i: w_in[:, offs[i]:offs[i + 1]]
    w_main = jnp.concatenate([seg(0), seg(1), seg(2), seg(4), seg(5), seg(8), seg(9), seg(10)],
                             axis=1).astype(BF16)
    n_small = GLA_RANK + 2 * GDN_HEADS
    w_small = jnp.concatenate(
        [seg(3), seg(6), seg(7), jnp.zeros((D_MODEL, LANES - n_small), F32)],
        axis=1).astype(BF16)
    lane_pad = lambda v: jnp.zeros((1, LANES), F32).at[0, SM_DEC:SM_DEC + GDN_HEADS].set(v)
    return {
        "ln_g": ln_in_g.reshape(1, D_MODEL),
        "w_main": w_main,
        "w_small": w_small,
        "w_alpha2": w_alpha2,
        "b_alpha": b_alpha.reshape(1, GLA_QK),
        "conv_w": conv_w,
        "alog_pad": lane_pad(a_log),
        "dtb_pad": lane_pad(dt_bias),
        "gla_norm_g": gla_norm_g.reshape(1, GLA_DV),
        "gdn_norm_g": gdn_norm_g.reshape(1, GDN_DV),
        "w_br_a": w_br_a.astype(BF16),
        "w_br_b": w_br_b.astype(BF16),
        "w_out": w_out.astype(BF16),
        "final_g": final_norm_g.reshape(1, D_MODEL),
    }


def kernel(x_prompt, x_sample, state_gla, state_gdn, state_conv, ln_in_g, w_in, w_alpha2,
           b_alpha, conv_w, a_log, dt_bias, gla_norm_g, gdn_norm_g, w_br_a, w_br_b, w_out,
           final_norm_g):
    assert ln_in_g.shape[0] == 1, "single layer"
    p = _prep_params(ln_in_g[0], w_in[0], w_alpha2[0], b_alpha[0], conv_w[0], a_log[0],
                     dt_bias[0], gla_norm_g[0], gdn_norm_g[0], w_br_a[0], w_br_b[0], w_out[0],
                     final_norm_g)
    bp, tp, _ = x_prompt.shape
    bs, ts, _ = x_sample.shape
    z_gla = jnp.zeros((bp, GLA_HEADS, GLA_DK, GLA_DV), F32)
    cp = min(CHUNK, tp)
    yp, gla_p, gdn_p, conv_p = _layer(x_prompt, z_gla, None, None, p,
                                      c=cp, nc_gla=4, act_dtype=BF16)
    cs = min(CHUNK, ts)
    ys, gla_s, gdn_s, conv_s = _layer(x_sample, state_gla[0], state_gdn[0], state_conv[0], p,
                                      c=cs, nc_gla=1, act_dtype=F32)
    return (yp, ys, gla_p[None], gdn_p[None], conv_p[None], gla_s[None], gdn_s[None],
            conv_s[None])
```

```python
import functools

import jax
import jax.numpy as jnp
from jax import lax
from jax.experimental import pallas as pl
from jax.experimental.pallas import tpu as pltpu

F32 = jnp.float32
BF16 = jnp.bfloat16

D_MODEL = 2048
GLA_HEADS = 4
GLA_DK = 256
GLA_DV = 512
GLA_RANK = 16
GLA_TAU = 16.0
GDN_HEADS = 16
GDN_DK = 128
GDN_DV = 128
CONV_W = 4
CHUNK = 64
EPS = 1e-6

GLA_QK = GLA_HEADS * GLA_DK
GLA_V = GLA_HEADS * GLA_DV
GDN_QK = GDN_HEADS * GDN_DK
GDN_V = GDN_HEADS * GDN_DV
CONV_CH = 2 * GDN_QK + GDN_V

LANES = 128
COL_QA = 0
COL_KA = COL_QA + GLA_QK
COL_VA = COL_KA + GLA_QK
COL_GATE_A = COL_VA + GLA_V
COL_QKV_B = COL_GATE_A + GLA_V
COL_GATE_B = COL_QKV_B + CONV_CH
COL_MA = COL_GATE_B + GDN_V
COL_MB = COL_MA + D_MODEL
N_MAIN = COL_MB + D_MODEL
SM_LR = 0
SM_BETA = GLA_RANK
SM_DEC = SM_BETA + GDN_HEADS

VMEM_LIMIT = 56 * 1024 * 1024


def _bdot(a, b):
    return jnp.dot(a.astype(BF16), b.astype(BF16), preferred_element_type=F32)


def _bdot_nt(a, b):
    return lax.dot_general(a.astype(BF16), b.astype(BF16), (((1,), (1,)), ((), ())),
                           preferred_element_type=F32)


def _bdot_tn(a, b):
    return lax.dot_general(a.astype(BF16), b.astype(BF16), (((0,), (0,)), ((), ())),
                           preferred_element_type=F32)


def _hdot(a, b):
    return jnp.dot(a, b, precision=lax.Precision.HIGHEST, preferred_element_type=F32)


def _dot(a, b):
    return jnp.dot(a, b, preferred_element_type=F32)


def _sigmoid(x):
    return 1.0 / (1.0 + jnp.exp(-x))


def _silu(x):
    return x * _sigmoid(x)


def _softplus(x):
    return jnp.maximum(x, 0.0) + jnp.log(1.0 + jnp.exp(-jnp.abs(x)))


def _tri_incl(c):
    r = lax.broadcasted_iota(jnp.int32, (c, c), 0)
    col = lax.broadcasted_iota(jnp.int32, (c, c), 1)
    return r >= col, r > col


def _in_proj_kernel(x_ref, g_ref, w_ref, ws_ref, o_ref, os_ref, h_ref):
    @pl.when(pl.program_id(1) == 0)
    def _():
        x = x_ref[...]
        inv = lax.rsqrt(jnp.mean(x * x, axis=-1, keepdims=True) + EPS)
        h = (x * inv * g_ref[...]).astype(BF16)
        h_ref[...] = h
        os_ref[...] = jnp.dot(h, ws_ref[...], preferred_element_type=F32)

    o_ref[...] = jnp.dot(h_ref[...], w_ref[...], preferred_element_type=F32)


def _in_proj(x2d, ln_g, w_main, w_small):
    n_tok, d = x2d.shape
    tm = min(1024, n_tok)
    tn = 1024
    return pl.pallas_call(
        _in_proj_kernel,
        grid=(n_tok // tm, N_MAIN // tn),
        in_specs=[
            pl.BlockSpec((tm, d), lambda i, j: (i, 0)),
            pl.BlockSpec((1, d), lambda i, j: (0, 0)),
            pl.BlockSpec((d, tn), lambda i, j: (0, j)),
            pl.BlockSpec((d, LANES), lambda i, j: (0, 0)),
        ],
        out_specs=[
            pl.BlockSpec((tm, tn), lambda i, j: (i, j)),
            pl.BlockSpec((tm, LANES), lambda i, j: (i, 0)),
        ],
        out_shape=[jax.ShapeDtypeStruct((n_tok, N_MAIN), F32),
                   jax.ShapeDtypeStruct((n_tok, LANES), F32)],
        scratch_shapes=[pltpu.VMEM((tm, d), BF16)],
        compiler_params=pltpu.CompilerParams(
            dimension_semantics=("arbitrary", "arbitrary"), vmem_limit_bytes=VMEM_LIMIT),
        name="in_proj",
    )(x2d, ln_g, w_main, w_small)


GLA_PROMPT_ROWS = 128
GLA_SAMPLE_ROWS = 32


def _gla_kernel(*refs, seq_len, rows, carry):
    n_in = 8 if carry else 9
    q_ref, k_ref, v_ref, gate_ref, sm_ref, wal_ref, bal_ref, ng_ref = refs[:8]
    s0_ref = None if carry else refs[8]
    o_ref, so_ref = refs[n_in:n_in + 2]
    qd_ref, ke_ref, ebl_ref, oacc_ref = refs[n_in + 2:n_in + 6]
    s_ref = refs[n_in + 6] if carry else None
    r, L = rows, seq_len
    nsub = r // L
    g = min(r, CHUNK)

    if carry:
        @pl.when(pl.program_id(1) == 0)
        def _():
            s_ref[...] = jnp.zeros_like(s_ref)

    ri = lax.broadcasted_iota(jnp.int32, (r, r), 0)
    ci = lax.broadcasted_iota(jnp.int32, (r, r), 1)
    incl = ((ri // L) == (ci // L)) & (ri >= ci)
    gi = lax.broadcasted_iota(jnp.int32, (g, 2 * LANES), 0)
    gl = lax.broadcasted_iota(jnp.int32, (g, 2 * LANES), 1)
    gc = gl % g
    tri3 = ((gl < 3 * g) & ((gi // L) == (gc // L)) & (gi >= gc)).astype(BF16)

    def seg_cumsum(x):
        outs = []
        for gq in range(r // g):
            xg = x[gq * g:(gq + 1) * g]
            hi, r1 = _split_f32(xg)
            mid, r2 = _split_f32(r1)
            parts = [hi, mid, r2.astype(BF16)]
            if 3 * g < 2 * LANES:
                parts.append(jnp.zeros((2 * LANES - 3 * g, xg.shape[1]), BF16))
            outs.append(_dot(tri3, jnp.concatenate(parts, axis=0)))
        return outs[0] if len(outs) == 1 else jnp.concatenate(outs, axis=0)

    for h in range(GLA_HEADS):
        lk = slice(h * GLA_DK, (h + 1) * GLA_DK)
        lv = slice(h * GLA_DV, (h + 1) * GLA_DV)
        q = q_ref[:, lk] * (GLA_DK ** -0.5)
        k = k_ref[:, lk]
        lr = sm_ref[:, SM_LR:SM_LR + GLA_RANK]
        z = _bdot(lr, wal_ref[:, lk]) + bal_ref[:, lk]
        la = -_softplus(-z) / GLA_TAU
        b = seg_cumsum(la)
        lasts = [jnp.broadcast_to(b[(qi + 1) * L - 1:(qi + 1) * L, :], (L, GLA_DK))
                 for qi in range(nsub)]
        b_last = lasts[0] if nsub == 1 else jnp.concatenate(lasts, axis=0)
        q_d = q * jnp.exp(b)
        k_d = k * jnp.exp(-b)
        att = jnp.where(incl, _bdot_nt(q_d, k_d), 0.0)
        oacc_ref[h] = _bdot(att, v_ref[:, lv])
        qd_ref[h] = q_d
        ke_ref[h] = k * jnp.exp(b_last - b)
        ebl_ref[h] = jnp.exp(b_last)

    for qi in range(nsub):
        rs = slice(qi * L, (qi + 1) * L)
        for h in range(GLA_HEADS):
            lv = slice(h * GLA_DV, (h + 1) * GLA_DV)
            s = s_ref[h] if carry else s0_ref[qi, h]
            oacc_ref[h, rs, :] = oacc_ref[h, rs, :] + _bdot(qd_ref[h, rs, :], s)
            kv = _bdot_tn(ke_ref[h, rs, :], v_ref[rs, lv])
            eb = ebl_ref[h, qi * L:qi * L + 1, :]
            col = jnp.transpose(jnp.broadcast_to(eb, (LANES, GLA_DK)))
            s_new = s * jnp.concatenate([col] * (GLA_DV // LANES), axis=1) + kv
            if carry:
                s_ref[h] = s_new
            else:
                so_ref[qi, h] = s_new

    for h in range(GLA_HEADS):
        lv = slice(h * GLA_DV, (h + 1) * GLA_DV)
        o = oacc_ref[h]
        inv = lax.rsqrt(jnp.mean(o * o, axis=-1, keepdims=True) + EPS)
        on = o * inv * ng_ref[...]
        o_ref[:, lv] = (on * _silu(gate_ref[:, lv])).astype(o_ref.dtype)

    if carry:
        @pl.when(pl.program_id(1) == pl.num_programs(1) - 1)
        def _():
            so_ref[...] = s_ref[...]


def _gla(proj, small, w_alpha2, b_alpha, norm_g, s0, *, bsz, t, rows):
    carry = s0 is None
    r = rows
    n_tok = proj.shape[0]
    if carry:
        seq_len = min(CHUNK, t)
        assert t % r == 0 and r % seq_len == 0
        grid = (bsz, t // r)
        row = lambda b, n: b * (t // r) + n
    else:
        assert t <= CHUNK and r % t == 0 and bsz % (r // t) == 0
        seq_len = t
        grid = (bsz // (r // t), 1)
        row = lambda b, n: b
    nsub = r // seq_len
    qb, kb = COL_QA // GLA_QK, COL_KA // GLA_QK
    vb, gb = COL_VA // GLA_V, COL_GATE_A // GLA_V
    in_specs = [
        pl.BlockSpec((r, GLA_QK), lambda b, n: (row(b, n), qb)),
        pl.BlockSpec((r, GLA_QK), lambda b, n: (row(b, n), kb)),
        pl.BlockSpec((r, GLA_V), lambda b, n: (row(b, n), vb)),
        pl.BlockSpec((r, GLA_V), lambda b, n: (row(b, n), gb)),
        pl.BlockSpec((r, LANES), lambda b, n: (row(b, n), 0)),
        pl.BlockSpec((GLA_RANK, GLA_QK), lambda b, n: (0, 0)),
        pl.BlockSpec((1, GLA_QK), lambda b, n: (0, 0)),
        pl.BlockSpec((1, GLA_DV), lambda b, n: (0, 0)),
    ]
    args = [proj, proj, proj, proj, small, w_alpha2, b_alpha, norm_g]
    scratch = [pltpu.VMEM((GLA_HEADS, r, GLA_DK), F32)] * 3 + [
        pltpu.VMEM((GLA_HEADS, r, GLA_DV), F32)]
    if carry:
        state = pl.BlockSpec((None, GLA_HEADS, GLA_DK, GLA_DV), lambda b, n: (b, 0, 0, 0))
        scratch += [pltpu.VMEM((GLA_HEADS, GLA_DK, GLA_DV), F32)]
    else:
        state = pl.BlockSpec((nsub, GLA_HEADS, GLA_DK, GLA_DV), lambda b, n: (b, 0, 0, 0))
        in_specs += [state]
        args += [s0]
    return pl.pallas_call(
        functools.partial(_gla_kernel, seq_len=seq_len, rows=r, carry=carry),
        grid=grid,
        in_specs=in_specs,
        out_specs=[pl.BlockSpec((r, GLA_V), lambda b, n: (row(b, n), 0)), state],
        out_shape=[jax.ShapeDtypeStruct((n_tok, GLA_V), BF16),
                   jax.ShapeDtypeStruct((bsz, GLA_HEADS, GLA_DK, GLA_DV), F32)],
        scratch_shapes=scratch,
        compiler_params=pltpu.CompilerParams(
            dimension_semantics=("arbitrary", "arbitrary"), vmem_limit_bytes=VMEM_LIMIT),
        name="gla",
    )(*args)


GDN_ROWS = 64
HALF = LANES // 2


def _split_f32(x):
    hi = x.astype(BF16)
    return hi, x - hi.astype(F32)


def _lhs3(x, low_half):
    hi, lo = _split_f32(x)
    t0 = jnp.where(low_half, x, lo).astype(BF16)
    t1 = jnp.where(low_half, hi, jnp.zeros_like(hi))
    return jnp.concatenate([t0, t1], axis=1)


def _rhs3(hi, lo):
    return jnp.concatenate([hi, hi, lo, lo], axis=0)


def _gdn_kernel(*refs, seq_len, carry):
    n_in = 11 if carry else 15
    (q_ref, k_ref, v_ref, gate_ref, sm_ref, cwq_ref, cwk_ref, cwv_ref, alog_ref, dtb_ref,
     ng_ref) = refs[:11]
    cq0_ref, ck0_ref, cv0_ref, s0_ref = (None,) * 4 if carry else refs[11:15]
    o_ref, so_ref = refs[n_in:n_in + 2]
    (eq_ref, ek_ref, ev_ref, cq_ref, ck_ref, cv_ref, nhi_ref, nlo_ref, nl0_ref, p_ref, att_ref,
     rhi_ref, rlo_ref, qg_ref, ke_ref, u_ref, w_ref, os_ref) = refs[n_in + 2:n_in + 20]
    if carry:
        (s_ref,) = refs[n_in + 20:]
        hq_ref = hk_ref = hv_ref = None
    else:
        hq_ref, hk_ref, hv_ref = refs[n_in + 20:]
        s_ref = None
    r = GDN_ROWS
    L = seq_len
    nseq = r // L
    tail = CONV_W - 1
    pad = 8

    if carry:
        @pl.when(pl.program_id(1) == 0)
        def _():
            s_ref[...] = jnp.zeros_like(s_ref)
            for e_ref in (eq_ref, ek_ref, ev_ref):
                e_ref[0:pad, :] = jnp.zeros((pad, GDN_QK), F32)
    else:
        @pl.when(pl.program_id(0) == 0)
        def _():
            for e_ref in (eq_ref, ek_ref, ev_ref, hq_ref, hk_ref, hv_ref):
                e_ref[...] = jnp.zeros_like(e_ref)

    row_t = lax.broadcasted_iota(jnp.int32, (r, 1), 0) % L

    def conv(e_ref, x_ref, w_ref, out_ref, h_ref, c0_ref):
        e_ref[pad:pad + r, :] = x_ref[...]
        if not carry:
            for qi in range(nseq):
                h_ref[qi * L + pad - tail:qi * L + pad, :] = c0_ref[qi]
        acc = None
        for i in range(CONV_W):
            sh = tail - i
            xs = e_ref[pad - sh:pad - sh + r, :]
            if not carry and sh > 0:
                xs = jnp.where(row_t >= sh, xs, h_ref[pad - sh:pad - sh + r, :])
            term = xs * w_ref[i:i + 1, :]
            acc = term if acc is None else acc + term
        if carry:
            e_ref[pad - tail:pad, :] = x_ref[r - tail:r, :]
        out_ref[...] = _silu(acc)

    conv(eq_ref, q_ref, cwq_ref, cq_ref, hq_ref, cq0_ref)
    conv(ek_ref, k_ref, cwk_ref, ck_ref, hk_ref, ck0_ref)
    conv(ev_ref, v_ref, cwv_ref, cv_ref, hv_ref, cv0_ref)

    ri = lax.broadcasted_iota(jnp.int32, (r, LANES), 0)
    li = lax.broadcasted_iota(jnp.int32, (r, LANES), 1)
    cj = li % HALF
    low_half = li < HALF
    same = (ri // L) == (cj // L)
    incl = same & (ri >= cj)
    strict = same & (ri > cj)
    eye = (ri == cj).astype(F32)

    sm = sm_ref[...]
    beta_all = _sigmoid(sm)
    g_all = -jnp.exp(alog_ref[...]) * _softplus(sm + dtb_ref[...])
    g_hi, g_r1 = _split_f32(g_all)
    g_mid, g_r2 = _split_f32(g_r1)
    g_lo = g_r2.astype(BF16)
    tri = incl.astype(BF16)
    tri_lhs = jnp.concatenate([tri, jnp.where(low_half, tri, jnp.zeros_like(tri))], axis=1)
    gh_all = _dot(tri_lhs, jnp.concatenate([g_hi, g_mid, g_lo, g_lo], axis=0))
    gh_t = jnp.transpose(jnp.concatenate([gh_all, gh_all], axis=0))
    glast_all = jnp.concatenate(
        [jnp.broadcast_to(gh_all[(qi + 1) * L - 1:(qi + 1) * L, :], (L, LANES))
         for qi in range(nseq)], axis=0)
    egh_all = jnp.exp(gh_all)
    kend_all = jnp.exp(glast_all - gh_all)

    for h in range(GDN_HEADS):
        ls = slice(h * GDN_DK, (h + 1) * GDN_DK)
        q = cq_ref[:, ls]
        k = ck_ref[:, ls]
        v = cv_ref[:, ls]
        q = q * lax.rsqrt(jnp.sum(q * q, axis=-1, keepdims=True) + EPS) * (GDN_DK ** -0.5)
        k = k * lax.rsqrt(jnp.sum(k * k, axis=-1, keepdims=True) + EPS)
        col = slice(SM_DEC + h, SM_DEC + h + 1)
        beta = beta_all[:, SM_BETA + h:SM_BETA + h + 1]
        gh = gh_all[:, col]
        egh = egh_all[:, col]
        decay = jnp.where(incl, jnp.exp(gh - gh_t[col, :]), 0.0)
        kb = k.astype(BF16)
        qkk = lax.dot_general(jnp.concatenate([q.astype(BF16), kb], axis=0),
                              jnp.concatenate([kb, kb], axis=0),
                              (((1,), (1,)), ((), ())), preferred_element_type=F32)
        att_ref[h] = (qkk[:r] * decay).astype(BF16)
        nm = jnp.where(strict, -(beta * qkk[r:] * decay), 0.0)
        n_hi, n_lo = _split_f32(nm)
        nhi_ref[h] = n_hi
        nlo_ref[h] = n_lo.astype(BF16)
        nl0_ref[h] = jnp.where(low_half, nm, n_lo).astype(BF16)
        p_ref[h] = eye + nm
        r_hi, r_lo = _split_f32(jnp.concatenate([v * beta, k * (beta * egh)], axis=1))
        rhi_ref[h] = r_hi
        rlo_ref[h] = r_lo.astype(BF16)
        qg_ref[h] = q * egh
        ke_ref[h] = k * kend_all[:, col]

    def n_lhs(h):
        n_hi = nhi_ref[h]
        return jnp.concatenate([nl0_ref[h], jnp.where(low_half, n_hi, jnp.zeros_like(n_hi))],
                               axis=1)

    p = 2
    while p < L:
        for h in range(GDN_HEADS):
            n2 = _dot(n_lhs(h), _rhs3(nhi_ref[h], nlo_ref[h]))
            n_hi, n_lo = _split_f32(n2)
            nhi_ref[h] = n_hi
            nlo_ref[h] = n_lo.astype(BF16)
            nl0_ref[h] = jnp.where(low_half, n2, n_lo).astype(BF16)
        for h in range(GDN_HEADS):
            pm = p_ref[h]
            p_ref[h] = pm + _dot(_lhs3(pm, low_half), _rhs3(nhi_ref[h], nlo_ref[h]))
        p *= 2

    for h in range(GDN_HEADS):
        sol = _dot(_lhs3(p_ref[h], low_half), _rhs3(rhi_ref[h], rlo_ref[h]))
        u_ref[h] = sol[:, :GDN_DV]
        w_ref[h] = sol[:, GDN_DV:]

    def state(qi, h):
        return s_ref[h] if carry else s0_ref[qi, h]

    for h in range(GDN_HEADS):
        for qi in range(nseq):
            rs = slice(qi * L, (qi + 1) * L)
            lhs = jnp.concatenate([w_ref[h, rs, :], qg_ref[h, rs, :]], axis=0).astype(BF16)
            res = _dot(lhs, state(qi, h).astype(BF16))
            u_ref[h, rs, :] = u_ref[h, rs, :] - res[:L]
            os_ref[h, rs, :] = res[L:]

    for h in range(GDN_HEADS):
        ls = slice(h * GDN_DV, (h + 1) * GDN_DV)
        v_new = u_ref[h]
        o = os_ref[h] + _dot(att_ref[h][:, :HALF], v_new.astype(BF16))
        inv = lax.rsqrt(jnp.mean(o * o, axis=-1, keepdims=True) + EPS)
        on = o * inv * ng_ref[...]
        o_ref[:, ls] = (on * _silu(gate_ref[:, ls])).astype(o_ref.dtype)
        for qi in range(nseq):
            rs = slice(qi * L, (qi + 1) * L)
            kv = lax.dot_general(ke_ref[h, rs, :].astype(BF16), v_new[rs].astype(BF16),
                                 (((0,), (0,)), ((), ())), preferred_element_type=F32)
            eg = egh_all[(qi + 1) * L - 1:(qi + 1) * L, SM_DEC + h:SM_DEC + h + 1]
            s_new = state(qi, h) * eg + kv
            if carry:
                s_ref[h] = s_new
            else:
                so_ref[qi, h] = s_new

    if carry:
        @pl.when(pl.program_id(1) == pl.num_programs(1) - 1)
        def _():
            so_ref[...] = s_ref[...]


def _gdn(proj, small, conv_w, alog_pad, dtb_pad, norm_g, conv0, s0, *, bsz, t):
    carry = s0 is None
    r = GDN_ROWS
    n_tok = proj.shape[0]
    tail = CONV_W - 1
    if carry:
        assert t % r == 0
        seq_len, nseq, grid = r, 1, (bsz, t // r)
        row = lambda b, n: b * (t // r) + n
    else:
        assert r % t == 0 and bsz % (r // t) == 0
        seq_len, nseq, grid = t, r // t, (bsz // (r // t), 1)
        row = lambda b, n: b
    qb = COL_QKV_B // GDN_QK
    gb = COL_GATE_B // GDN_V
    rowspec = lambda cb: pl.BlockSpec((r, GDN_QK), lambda b, n: (row(b, n), cb))
    cwspec = lambda cb: pl.BlockSpec((CONV_W, GDN_QK), lambda b, n: (0, cb))
    vec = pl.BlockSpec((1, LANES), lambda b, n: (0, 0))
    in_specs = [rowspec(qb), rowspec(qb + 1), rowspec(qb + 2), rowspec(gb),
                pl.BlockSpec((r, LANES), lambda b, n: (row(b, n), 0)),
                cwspec(0), cwspec(1), cwspec(2), vec, vec, vec]
    args = [proj, proj, proj, proj, small, conv_w, conv_w, conv_w, alog_pad, dtb_pad, norm_g]
    hshape = (GDN_HEADS, r, LANES)
    scratch = ([pltpu.VMEM((8 + r, GDN_QK), F32)] * 3 + [pltpu.VMEM((r, GDN_QK), F32)] * 3
               + [pltpu.VMEM(hshape, BF16)] * 3 + [pltpu.VMEM(hshape, F32)]
               + [pltpu.VMEM(hshape, BF16)]
               + [pltpu.VMEM((GDN_HEADS, r, 2 * LANES), BF16)] * 2
               + [pltpu.VMEM(hshape, F32)] * 5)
    if carry:
        state = pl.BlockSpec((None, GDN_HEADS, GDN_DK, GDN_DV), lambda b, n: (b, 0, 0, 0))
        scratch += [pltpu.VMEM((GDN_HEADS, GDN_DK, GDN_DV), F32)]
    else:
        state = pl.BlockSpec((nseq, GDN_HEADS, GDN_DK, GDN_DV), lambda b, n: (b, 0, 0, 0))
        c0spec = lambda cb: pl.BlockSpec((nseq, tail, GDN_QK), lambda b, n: (b, 0, cb))
        in_specs += [c0spec(0), c0spec(1), c0spec(2), state]
        args += [conv0, conv0, conv0, s0]
        scratch += [pltpu.VMEM((8 + r, GDN_QK), F32)] * 3
    return pl.pallas_call(
        functools.partial(_gdn_kernel, seq_len=seq_len, carry=carry),
        grid=grid,
        in_specs=in_specs,
        out_specs=[pl.BlockSpec((r, GDN_V), lambda b, n: (row(b, n), 0)), state],
        out_shape=[jax.ShapeDtypeStruct((n_tok, GDN_V), BF16),
                   jax.ShapeDtypeStruct((bsz, GDN_HEADS, GDN_DK, GDN_DV), F32)],
        scratch_shapes=scratch,
        compiler_params=pltpu.CompilerParams(
            dimension_semantics=("arbitrary", "arbitrary"), vmem_limit_bytes=VMEM_LIMIT),
        name="gdn",
    )(*args)


def _merge_kernel(oa_ref, ob_ref, wa_ref, wb_ref, ma_ref, mb_ref, o_ref):
    ya = jnp.dot(oa_ref[...].astype(BF16), wa_ref[...], preferred_element_type=F32)
    yb = jnp.dot(ob_ref[...].astype(BF16), wb_ref[...], preferred_element_type=F32)
    merged = _sigmoid(ma_ref[...]) * ya + _sigmoid(mb_ref[...]) * yb
    o_ref[...] = merged.astype(o_ref.dtype)


def _merge(o_a, o_b, w_br_a, w_br_b, proj):
    n_tok = o_a.shape[0]
    tm = min(1024, n_tok)
    tn = 512
    ma, mb = COL_MA // tn, COL_MB // tn
    return pl.pallas_call(
        _merge_kernel,
        grid=(n_tok // tm, D_MODEL // tn),
        in_specs=[
            pl.BlockSpec((tm, GLA_V), lambda i, j: (i, 0)),
            pl.BlockSpec((tm, GDN_V), lambda i, j: (i, 0)),
            pl.BlockSpec((GLA_V, tn), lambda i, j: (0, j)),
            pl.BlockSpec((GDN_V, tn), lambda i, j: (0, j)),
            pl.BlockSpec((tm, tn), lambda i, j: (i, ma + j)),
            pl.BlockSpec((tm, tn), lambda i, j: (i, mb + j)),
        ],
        out_specs=pl.BlockSpec((tm, tn), lambda i, j: (i, j)),
        out_shape=jax.ShapeDtypeStruct((n_tok, D_MODEL), BF16),
        compiler_params=pltpu.CompilerParams(
            dimension_semantics=("arbitrary", "arbitrary"), vmem_limit_bytes=VMEM_LIMIT),
        name="merge",
    )(o_a, o_b, w_br_a, w_br_b, proj, proj)


def _out_kernel(m_ref, w_ref, x_ref, g_ref, y_ref):
    out = x_ref[...] + jnp.dot(m_ref[...], w_ref[...], preferred_element_type=F32)
    inv = lax.rsqrt(jnp.mean(out * out, axis=-1, keepdims=True) + EPS)
    y_ref[...] = out * inv * g_ref[...]


def _out_proj(merged, w_out, x2d, final_g):
    n_tok = merged.shape[0]
    tm = min(512, n_tok)
    return pl.pallas_call(
        _out_kernel,
        grid=(n_tok // tm,),
        in_specs=[
            pl.BlockSpec((tm, D_MODEL), lambda i: (i, 0)),
            pl.BlockSpec((D_MODEL, D_MODEL), lambda i: (0, 0)),
            pl.BlockSpec((tm, D_MODEL), lambda i: (i, 0)),
            pl.BlockSpec((1, D_MODEL), lambda i: (0, 0)),
        ],
        out_specs=pl.BlockSpec((tm, D_MODEL), lambda i: (i, 0)),
        out_shape=jax.ShapeDtypeStruct((n_tok, D_MODEL), F32),
        compiler_params=pltpu.CompilerParams(
            dimension_semantics=("arbitrary",), vmem_limit_bytes=VMEM_LIMIT),
        name="out_proj",
    )(merged, w_out, x2d, final_g)


def _layer(x, s_gla, s_gdn, conv_buf, p, *, gla_rows):
    bsz, t, d = x.shape
    x2d = x.reshape(bsz * t, d)
    proj, small = _in_proj(x2d, p["ln_g"], p["w_main"], p["w_small"])
    o_a, s_gla_new = _gla(proj, small, p["w_alpha2"], p["b_alpha"], p["gla_norm_g"], s_gla,
                          bsz=bsz, t=t, rows=gla_rows)
    o_b, s_gdn_new = _gdn(proj, small, p["conv_w"], p["alog_pad"], p["dtb_pad"],
                          p["gdn_norm_g"], conv_buf, s_gdn, bsz=bsz, t=t)
    merged = _merge(o_a, o_b, p["w_br_a"], p["w_br_b"], proj)
    y = _out_proj(merged, p["w_out"], x2d, p["final_g"])
    tail = CONV_W - 1
    qkv = proj.reshape(bsz, t, N_MAIN)[:, t - tail:, COL_QKV_B:COL_QKV_B + CONV_CH]
    return y.reshape(bsz, t, d), s_gla_new, s_gdn_new, qkv


def _prep_params(ln_in_g, w_in, w_alpha2, b_alpha, conv_w, a_log, dt_bias, gla_norm_g,
                 gdn_norm_g, w_br_a, w_br_b, w_out, final_norm_g):
    sizes = (GLA_QK, GLA_QK, GLA_V, GLA_RANK, GLA_V, CONV_CH, GDN_HEADS, GDN_HEADS, GDN_V,
             D_MODEL, D_MODEL)
    offs = [0]
    for s in sizes:
        offs.append(offs[-1] + s)
    seg = lambda i: w_in[:, offs[i]:offs[i + 1]].astype(BF16)
    w_main = jnp.concatenate([seg(0), seg(1), seg(2), seg(4), seg(5), seg(8), seg(9), seg(10)],
                             axis=1)
    lr0, bd0 = offs[3] - SM_LR, offs[6] - SM_BETA
    assert lr0 % LANES == 0 and bd0 % LANES == 0 and offs[7] - bd0 == SM_DEC
    lane = jnp.arange(LANES)
    w_small = jnp.where(lane < SM_BETA, w_in[:, lr0:lr0 + LANES],
                        jnp.where(lane < SM_DEC + GDN_HEADS, w_in[:, bd0:bd0 + LANES], 0.0)
                        ).astype(BF16)
    lane_pad = lambda v: jnp.zeros((1, LANES), F32).at[0, SM_DEC:SM_DEC + GDN_HEADS].set(v)
    return {
        "ln_g": ln_in_g.reshape(1, D_MODEL),
        "w_main": w_main,
        "w_small": w_small,
        "w_alpha2": w_alpha2,
        "b_alpha": b_alpha.reshape(1, GLA_QK),
        "conv_w": conv_w,
        "alog_pad": lane_pad(a_log),
        "dtb_pad": lane_pad(dt_bias),
        "gla_norm_g": gla_norm_g.reshape(1, GLA_DV),
        "gdn_norm_g": gdn_norm_g.reshape(1, GDN_DV),
        "w_br_a": w_br_a.astype(BF16),
        "w_br_b": w_br_b.astype(BF16),
        "w_out": w_out.astype(BF16),
        "final_g": final_norm_g.reshape(1, D_MODEL),
    }


def kernel(x_prompt, x_sample, state_gla, state_gdn, state_conv, ln_in_g, w_in, w_alpha2,
           b_alpha, conv_w, a_log, dt_bias, gla_norm_g, gdn_norm_g, w_br_a, w_br_b, w_out,
           final_norm_g):
    assert ln_in_g.shape[0] == 1, "single layer"
    p = _prep_params(ln_in_g[0], w_in[0], w_alpha2[0], b_alpha[0], conv_w[0], a_log[0],
                     dt_bias[0], gla_norm_g[0], gdn_norm_g[0], w_br_a[0], w_br_b[0], w_out[0],
                     final_norm_g)
    yp, gla_p, gdn_p, conv_p = _layer(x_prompt, None, None, None, p, gla_rows=GLA_PROMPT_ROWS)
    ys, gla_s, gdn_s, conv_s = _layer(x_sample, state_gla[0], state_gdn[0], state_conv[0], p,
                                      gla_rows=GLA_SAMPLE_ROWS)
    return (yp, ys, gla_p[None], gdn_p[None], conv_p[None], gla_s[None], gdn_s[None],
            conv_s[None])
```

```python
import functools

import jax
import jax.numpy as jnp
from jax import lax
from jax.experimental import pallas as pl
from jax.experimental.pallas import tpu as pltpu

F32 = jnp.float32
BF16 = jnp.bfloat16

D_MODEL = 2048
GLA_HEADS = 4
GLA_DK = 256
GLA_DV = 512
GLA_RANK = 16
GLA_TAU = 16.0
GDN_HEADS = 16
GDN_DK = 128
GDN_DV = 128
CONV_W = 4
CHUNK = 64
EPS = 1e-6

GLA_QK = GLA_HEADS * GLA_DK
GLA_V = GLA_HEADS * GLA_DV
GDN_QK = GDN_HEADS * GDN_DK
GDN_V = GDN_HEADS * GDN_DV
CONV_CH = 2 * GDN_QK + GDN_V

LANES = 128
COL_QA = 0
COL_KA = COL_QA + GLA_QK
COL_VA = COL_KA + GLA_QK
COL_GATE_A = COL_VA + GLA_V
COL_QKV_B = COL_GATE_A + GLA_V
COL_GATE_B = COL_QKV_B + CONV_CH
COL_MA = COL_GATE_B + GDN_V
COL_MB = COL_MA + D_MODEL
N_MAIN = COL_MB + D_MODEL
SM_LR = 0
SM_BETA = GLA_RANK
SM_DEC = SM_BETA + GDN_HEADS

VMEM_LIMIT = 56 * 1024 * 1024


def _bdot(a, b):
    return jnp.dot(a.astype(BF16), b.astype(BF16), preferred_element_type=F32)


def _bdot_nt(a, b):
    return lax.dot_general(a.astype(BF16), b.astype(BF16), (((1,), (1,)), ((), ())),
                           preferred_element_type=F32)


def _bdot_tn(a, b):
    return lax.dot_general(a.astype(BF16), b.astype(BF16), (((0,), (0,)), ((), ())),
                           preferred_element_type=F32)


def _hdot(a, b):
    return jnp.dot(a, b, precision=lax.Precision.HIGHEST, preferred_element_type=F32)


def _dot(a, b):
    return jnp.dot(a, b, preferred_element_type=F32)


def _sigmoid(x):
    return 1.0 / (1.0 + jnp.exp(-x))


def _silu(x):
    return x * _sigmoid(x)


def _softplus(x):
    return jnp.maximum(x, 0.0) + jnp.log(1.0 + jnp.exp(-jnp.abs(x)))


def _tri_incl(c):
    r = lax.broadcasted_iota(jnp.int32, (c, c), 0)
    col = lax.broadcasted_iota(jnp.int32, (c, c), 1)
    return r >= col, r > col


W_IN_SIZES = (GLA_QK, GLA_QK, GLA_V, GLA_RANK, GLA_V, CONV_CH, GDN_HEADS, GDN_HEADS, GDN_V,
              D_MODEL, D_MODEL)
W_IN_OFFS = tuple(sum(W_IN_SIZES[:i]) for i in range(len(W_IN_SIZES) + 1))
PREP_STARTS = (0, COL_GATE_A, COL_GATE_B)
PREP_SHIFTS = (0, W_IN_OFFS[4] - COL_GATE_A, W_IN_OFFS[8] - COL_GATE_B)
PREP_TN = 1024


def _wprep_kernel(a_ref, nxt_ref, o_ref):
    j = pl.program_id(1)
    bounds = tuple(s // PREP_TN for s in PREP_STARTS) + (N_MAIN // PREP_TN,)
    for run, shift in enumerate(PREP_SHIFTS):
        @pl.when((j >= bounds[run]) & (j < bounds[run + 1]))
        def _(shift=shift):
            if shift == 0:
                o_ref[...] = a_ref[...].astype(BF16)
            else:
                x = jnp.concatenate([a_ref[...], nxt_ref[...]], axis=1)
                o_ref[...] = x[:, shift:shift + PREP_TN].astype(BF16)


def _prep_w_main(w_in):
    d = w_in.shape[0]
    tr = 512
    assert all(s % PREP_TN == 0 for s in PREP_STARTS) and max(PREP_SHIFTS) < LANES
    return pl.pallas_call(
        _wprep_kernel,
        grid=(d // tr, N_MAIN // PREP_TN),
        in_specs=[
            pl.BlockSpec((tr, PREP_TN), lambda i, j: (i, j)),
            pl.BlockSpec((tr, LANES), lambda i, j: (i, (j + 1) * (PREP_TN // LANES))),
        ],
        out_specs=pl.BlockSpec((tr, PREP_TN), lambda i, j: (i, j)),
        out_shape=jax.ShapeDtypeStruct((d, N_MAIN), BF16),
        compiler_params=pltpu.CompilerParams(
            dimension_semantics=("arbitrary", "arbitrary"), vmem_limit_bytes=VMEM_LIMIT),
        name="w_prep",
    )(w_in, w_in)


def _in_proj_kernel(x_ref, g_ref, w_ref, ws_ref, o_ref, os_ref, h_ref):
    @pl.when(pl.program_id(1) == 0)
    def _():
        x = x_ref[...]
        inv = lax.rsqrt(jnp.mean(x * x, axis=-1, keepdims=True) + EPS)
        h = (x * inv * g_ref[...]).astype(BF16)
        h_ref[...] = h
        os_ref[...] = jnp.dot(h, ws_ref[...], preferred_element_type=F32)

    o_ref[...] = jnp.dot(h_ref[...], w_ref[...], preferred_element_type=F32)


def _in_proj(x2d, ln_g, w_main, w_small):
    n_tok, d = x2d.shape
    tm = min(1024, n_tok)
    tn = 1024
    return pl.pallas_call(
        _in_proj_kernel,
        grid=(n_tok // tm, N_MAIN // tn),
        in_specs=[
            pl.BlockSpec((tm, d), lambda i, j: (i, 0)),
            pl.BlockSpec((1, d), lambda i, j: (0, 0)),
            pl.BlockSpec((d, tn), lambda i, j: (0, j)),
            pl.BlockSpec((d, LANES), lambda i, j: (0, 0)),
        ],
        out_specs=[
            pl.BlockSpec((tm, tn), lambda i, j: (i, j)),
            pl.BlockSpec((tm, LANES), lambda i, j: (i, 0)),
        ],
        out_shape=[jax.ShapeDtypeStruct((n_tok, N_MAIN), F32),
                   jax.ShapeDtypeStruct((n_tok, LANES), F32)],
        scratch_shapes=[pltpu.VMEM((tm, d), BF16)],
        compiler_params=pltpu.CompilerParams(
            dimension_semantics=("arbitrary", "arbitrary"), vmem_limit_bytes=VMEM_LIMIT),
        name="in_proj",
    )(x2d, ln_g, w_main, w_small)


GLA_PROMPT_ROWS = 512
GLA_SAMPLE_ROWS = 32


def _gla_kernel(*refs, seq_len, rows, carry):
    n_in = 8 if carry else 9
    q_ref, k_ref, v_ref, gate_ref, sm_ref, wal_ref, bal_ref, ng_ref = refs[:8]
    s0_ref = None if carry else refs[8]
    o_ref, so_ref = refs[n_in:n_in + 2]
    qd_ref, ke_ref, ebl_ref, oacc_ref = refs[n_in + 2:n_in + 6]
    s_ref = refs[n_in + 6] if carry else None
    r, L = rows, seq_len
    nsub = r // L
    g = min(r, CHUNK)

    if carry:
        @pl.when(pl.program_id(1) == 0)
        def _():
            s_ref[...] = jnp.zeros_like(s_ref)

    ri = lax.broadcasted_iota(jnp.int32, (r, r), 0)
    ci = lax.broadcasted_iota(jnp.int32, (r, r), 1)
    incl = ((ri // L) == (ci // L)) & (ri >= ci)
    gi = lax.broadcasted_iota(jnp.int32, (g, 2 * LANES), 0)
    gl = lax.broadcasted_iota(jnp.int32, (g, 2 * LANES), 1)
    gc = gl % g
    tri3 = ((gl < 3 * g) & ((gi // L) == (gc // L)) & (gi >= gc)).astype(BF16)

    def seg_cumsum(x):
        outs = []
        for gq in range(r // g):
            xg = x[gq * g:(gq + 1) * g]
            hi, r1 = _split_f32(xg)
            mid, r2 = _split_f32(r1)
            parts = [hi, mid, r2.astype(BF16)]
            if 3 * g < 2 * LANES:
                parts.append(jnp.zeros((2 * LANES - 3 * g, xg.shape[1]), BF16))
            outs.append(_dot(tri3, jnp.concatenate(parts, axis=0)))
        return outs[0] if len(outs) == 1 else jnp.concatenate(outs, axis=0)

    for h in range(GLA_HEADS):
        lk = slice(h * GLA_DK, (h + 1) * GLA_DK)
        lv = slice(h * GLA_DV, (h + 1) * GLA_DV)
        q = q_ref[:, lk] * (GLA_DK ** -0.5)
        k = k_ref[:, lk]
        lr = sm_ref[:, SM_LR:SM_LR + GLA_RANK]
        z = _bdot(lr, wal_ref[:, lk]) + bal_ref[:, lk]
        la = -_softplus(-z) / GLA_TAU
        b = seg_cumsum(la)
        lasts = [jnp.broadcast_to(b[(qi + 1) * L - 1:(qi + 1) * L, :], (L, GLA_DK))
                 for qi in range(nsub)]
        b_last = lasts[0] if nsub == 1 else jnp.concatenate(lasts, axis=0)
        q_d = q * jnp.exp(b)
        k_d = k * jnp.exp(-b)
        att = jnp.where(incl, _bdot_nt(q_d, k_d), 0.0)
        oacc_ref[h] = _bdot(att, v_ref[:, lv])
        qd_ref[h] = q_d
        ke_ref[h] = k * jnp.exp(b_last - b)
        ebl_ref[h] = jnp.exp(b_last)

    for qi in range(nsub):
        rs = slice(qi * L, (qi + 1) * L)
        for h in range(GLA_HEADS):
            lv = slice(h * GLA_DV, (h + 1) * GLA_DV)
            s = s_ref[h] if carry else s0_ref[qi, h]
            oacc_ref[h, rs, :] = oacc_ref[h, rs, :] + _bdot(qd_ref[h, rs, :], s)
            kv = _bdot_tn(ke_ref[h, rs, :], v_ref[rs, lv])
            eb = ebl_ref[h, qi * L:qi * L + 1, :]
            col = jnp.transpose(jnp.broadcast_to(eb, (LANES, GLA_DK)))
            s_new = s * jnp.concatenate([col] * (GLA_DV // LANES), axis=1) + kv
            if carry:
                s_ref[h] = s_new
            else:
                so_ref[qi, h] = s_new

    for h in range(GLA_HEADS):
        lv = slice(h * GLA_DV, (h + 1) * GLA_DV)
        o = oacc_ref[h]
        inv = lax.rsqrt(jnp.mean(o * o, axis=-1, keepdims=True) + EPS)
        on = o * inv * ng_ref[...]
        o_ref[:, lv] = (on * _silu(gate_ref[:, lv])).astype(o_ref.dtype)

    if carry:
        @pl.when(pl.program_id(1) == pl.num_programs(1) - 1)
        def _():
            so_ref[...] = s_ref[...]


def _gla(proj, small, w_alpha2, b_alpha, norm_g, s0, *, bsz, t, rows):
    carry = s0 is None
    r = rows
    n_tok = proj.shape[0]
    if carry:
        seq_len = min(CHUNK, t)
        assert t % r == 0 and r % seq_len == 0
        grid = (bsz, t // r)
        row = lambda b, n: b * (t // r) + n
    else:
        assert t <= CHUNK and r % t == 0 and bsz % (r // t) == 0
        seq_len = t
        grid = (bsz // (r // t), 1)
        row = lambda b, n: b
    nsub = r // seq_len
    qb, kb = COL_QA // GLA_QK, COL_KA // GLA_QK
    vb, gb = COL_VA // GLA_V, COL_GATE_A // GLA_V
    in_specs = [
        pl.BlockSpec((r, GLA_QK), lambda b, n: (row(b, n), qb)),
        pl.BlockSpec((r, GLA_QK), lambda b, n: (row(b, n), kb)),
        pl.BlockSpec((r, GLA_V), lambda b, n: (row(b, n), vb)),
        pl.BlockSpec((r, GLA_V), lambda b, n: (row(b, n), gb)),
        pl.BlockSpec((r, LANES), lambda b, n: (row(b, n), 0)),
        pl.BlockSpec((GLA_RANK, GLA_QK), lambda b, n: (0, 0)),
        pl.BlockSpec((1, GLA_QK), lambda b, n: (0, 0)),
        pl.BlockSpec((1, GLA_DV), lambda b, n: (0, 0)),
    ]
    args = [proj, proj, proj, proj, small, w_alpha2, b_alpha, norm_g]
    scratch = [pltpu.VMEM((GLA_HEADS, r, GLA_DK), F32)] * 3 + [
        pltpu.VMEM((GLA_HEADS, r, GLA_DV), F32)]
    if carry:
        state = pl.BlockSpec((None, GLA_HEADS, GLA_DK, GLA_DV), lambda b, n: (b, 0, 0, 0))
        scratch += [pltpu.VMEM((GLA_HEADS, GLA_DK, GLA_DV), F32)]
    else:
        state = pl.BlockSpec((nsub, GLA_HEADS, GLA_DK, GLA_DV), lambda b, n: (b, 0, 0, 0))
        in_specs += [state]
        args += [s0]
    return pl.pallas_call(
        functools.partial(_gla_kernel, seq_len=seq_len, rows=r, carry=carry),
        grid=grid,
        in_specs=in_specs,
        out_specs=[pl.BlockSpec((r, GLA_V), lambda b, n: (row(b, n), 0)), state],
        out_shape=[jax.ShapeDtypeStruct((n_tok, GLA_V), BF16),
                   jax.ShapeDtypeStruct((bsz, GLA_HEADS, GLA_DK, GLA_DV), F32)],
        scratch_shapes=scratch,
        compiler_params=pltpu.CompilerParams(
            dimension_semantics=("arbitrary", "arbitrary"), vmem_limit_bytes=VMEM_LIMIT),
        name="gla",
    )(*args)


GDN_ROWS = 64
HALF = LANES // 2


def _split_f32(x):
    hi = x.astype(BF16)
    return hi, x - hi.astype(F32)


def _lhs3(x, low_half):
    hi, lo = _split_f32(x)
    t0 = jnp.where(low_half, x, lo).astype(BF16)
    t1 = jnp.where(low_half, hi, jnp.zeros_like(hi))
    return jnp.concatenate([t0, t1], axis=1)


def _rhs3(hi, lo):
    return jnp.concatenate([hi, hi, lo, lo], axis=0)


def _gdn_kernel(*refs, seq_len, carry):
    n_in = 11 if carry else 15
    (q_ref, k_ref, v_ref, gate_ref, sm_ref, cwq_ref, cwk_ref, cwv_ref, alog_ref, dtb_ref,
     ng_ref) = refs[:11]
    cq0_ref, ck0_ref, cv0_ref, s0_ref = (None,) * 4 if carry else refs[11:15]
    o_ref, so_ref = refs[n_in:n_in + 2]
    (eq_ref, ek_ref, ev_ref, cq_ref, ck_ref, cv_ref, nhi_ref, nlo_ref, nl0_ref, p_ref, att_ref,
     rhi_ref, rlo_ref, qg_ref, ke_ref, u_ref, w_ref, os_ref) = refs[n_in + 2:n_in + 20]
    if carry:
        (s_ref,) = refs[n_in + 20:]
        hq_ref = hk_ref = hv_ref = None
    else:
        hq_ref, hk_ref, hv_ref = refs[n_in + 20:]
        s_ref = None
    r = GDN_ROWS
    L = seq_len
    nseq = r // L
    tail = CONV_W - 1
    pad = 8

    if carry:
        @pl.when(pl.program_id(1) == 0)
        def _():
            s_ref[...] = jnp.zeros_like(s_ref)
            for e_ref in (eq_ref, ek_ref, ev_ref):
                e_ref[0:pad, :] = jnp.zeros((pad, GDN_QK), F32)
    else:
        @pl.when(pl.program_id(0) == 0)
        def _():
            for e_ref in (eq_ref, ek_ref, ev_ref, hq_ref, hk_ref, hv_ref):
                e_ref[...] = jnp.zeros_like(e_ref)

    row_t = lax.broadcasted_iota(jnp.int32, (r, 1), 0) % L

    def conv(e_ref, x_ref, w_ref, out_ref, h_ref, c0_ref):
        e_ref[pad:pad + r, :] = x_ref[...]
        if not carry:
            for qi in range(nseq):
                h_ref[qi * L + pad - tail:qi * L + pad, :] = c0_ref[qi]
        acc = None
        for i in range(CONV_W):
            sh = tail - i
            xs = e_ref[pad - sh:pad - sh + r, :]
            if not carry and sh > 0:
                xs = jnp.where(row_t >= sh, xs, h_ref[pad - sh:pad - sh + r, :])
            term = xs * w_ref[i:i + 1, :]
            acc = term if acc is None else acc + term
        if carry:
            e_ref[pad - tail:pad, :] = x_ref[r - tail:r, :]
        out_ref[...] = _silu(acc)

    conv(eq_ref, q_ref, cwq_ref, cq_ref, hq_ref, cq0_ref)
    conv(ek_ref, k_ref, cwk_ref, ck_ref, hk_ref, ck0_ref)
    conv(ev_ref, v_ref, cwv_ref, cv_ref, hv_ref, cv0_ref)

    ri = lax.broadcasted_iota(jnp.int32, (r, LANES), 0)
    li = lax.broadcasted_iota(jnp.int32, (r, LANES), 1)
    cj = li % HALF
    low_half = li < HALF
    same = (ri // L) == (cj // L)
    incl = same & (ri >= cj)
    strict = same & (ri > cj)
    eye = (ri == cj).astype(F32)

    sm = sm_ref[...]
    beta_all = _sigmoid(sm)
    g_all = -jnp.exp(alog_ref[...]) * _softplus(sm + dtb_ref[...])
    g_hi, g_r1 = _split_f32(g_all)
    g_mid, g_r2 = _split_f32(g_r1)
    g_lo = g_r2.astype(BF16)
    tri = incl.astype(BF16)
    tri_lhs = jnp.concatenate([tri, jnp.where(low_half, tri, jnp.zeros_like(tri))], axis=1)
    gh_all = _dot(tri_lhs, jnp.concatenate([g_hi, g_mid, g_lo, g_lo], axis=0))
    gh_t = jnp.transpose(jnp.concatenate([gh_all, gh_all], axis=0))
    glast_all = jnp.concatenate(
        [jnp.broadcast_to(gh_all[(qi + 1) * L - 1:(qi + 1) * L, :], (L, LANES))
         for qi in range(nseq)], axis=0)
    egh_all = jnp.exp(gh_all)
    kend_all = jnp.exp(glast_all - gh_all)

    for h in range(GDN_HEADS):
        ls = slice(h * GDN_DK, (h + 1) * GDN_DK)
        q = cq_ref[:, ls]
        k = ck_ref[:, ls]
        v = cv_ref[:, ls]
        q = q * lax.rsqrt(jnp.sum(q * q, axis=-1, keepdims=True) + EPS) * (GDN_DK ** -0.5)
        k = k * lax.rsqrt(jnp.sum(k * k, axis=-1, keepdims=True) + EPS)
        col = slice(SM_DEC + h, SM_DEC + h + 1)
        beta = beta_all[:, SM_BETA + h:SM_BETA + h + 1]
        gh = gh_all[:, col]
        egh = egh_all[:, col]
        decay = jnp.where(incl, jnp.exp(gh - gh_t[col, :]), 0.0)
        kb = k.astype(BF16)
        qkk = lax.dot_general(jnp.concatenate([q.astype(BF16), kb], axis=0),
                              jnp.concatenate([kb, kb], axis=0),
                              (((1,), (1,)), ((), ())), preferred_element_type=F32)
        att_ref[h] = (qkk[:r] * decay).astype(BF16)
        nm = jnp.where(strict, -(beta * qkk[r:] * decay), 0.0)
        n_hi, n_lo = _split_f32(nm)
        nhi_ref[h] = n_hi
        nlo_ref[h] = n_lo.astype(BF16)
        nl0_ref[h] = jnp.where(low_half, nm, n_lo).astype(BF16)
        p_ref[h] = eye + nm
        r_hi, r_lo = _split_f32(jnp.concatenate([v * beta, k * (beta * egh)], axis=1))
        rhi_ref[h] = r_hi
        rlo_ref[h] = r_lo.astype(BF16)
        qg_ref[h] = q * egh
        ke_ref[h] = k * kend_all[:, col]

    def n_lhs(h):
        n_hi = nhi_ref[h]
        return jnp.concatenate([nl0_ref[h], jnp.where(low_half, n_hi, jnp.zeros_like(n_hi))],
                               axis=1)

    p = 2
    while p < L:
        for h in range(GDN_HEADS):
            n2 = _dot(n_lhs(h), _rhs3(nhi_ref[h], nlo_ref[h]))
            n_hi, n_lo = _split_f32(n2)
            nhi_ref[h] = n_hi
            nlo_ref[h] = n_lo.astype(BF16)
            nl0_ref[h] = jnp.where(low_half, n2, n_lo).astype(BF16)
        for h in range(GDN_HEADS):
            pm = p_ref[h]
            p_ref[h] = pm + _dot(_lhs3(pm, low_half), _rhs3(nhi_ref[h], nlo_ref[h]))
        p *= 2

    for h in range(GDN_HEADS):
        sol = _dot(_lhs3(p_ref[h], low_half), _rhs3(rhi_ref[h], rlo_ref[h]))
        u_ref[h] = sol[:, :GDN_DV]
        w_ref[h] = sol[:, GDN_DV:]

    def state(qi, h):
        return s_ref[h] if carry else s0_ref[qi, h]

    for h in range(GDN_HEADS):
        for qi in range(nseq):
            rs = slice(qi * L, (qi + 1) * L)
            lhs = jnp.concatenate([w_ref[h, rs, :], qg_ref[h, rs, :]], axis=0).astype(BF16)
            res = _dot(lhs, state(qi, h).astype(BF16))
            u_ref[h, rs, :] = u_ref[h, rs, :] - res[:L]
            os_ref[h, rs, :] = res[L:]

    for h in range(GDN_HEADS):
        ls = slice(h * GDN_DV, (h + 1) * GDN_DV)
        v_new = u_ref[h]
        o = os_ref[h] + _dot(att_ref[h][:, :HALF], v_new.astype(BF16))
        inv = lax.rsqrt(jnp.mean(o * o, axis=-1, keepdims=True) + EPS)
        on = o * inv * ng_ref[...]
        o_ref[:, ls] = (on * _silu(gate_ref[:, ls])).astype(o_ref.dtype)
        for qi in range(nseq):
            rs = slice(qi * L, (qi + 1) * L)
            kv = lax.dot_general(ke_ref[h, rs, :].astype(BF16), v_new[rs].astype(BF16),
                                 (((0,), (0,)), ((), ())), preferred_element_type=F32)
            eg = egh_all[(qi + 1) * L - 1:(qi + 1) * L, SM_DEC + h:SM_DEC + h + 1]
            s_new = state(qi, h) * eg + kv
            if carry:
                s_ref[h] = s_new
            else:
                so_ref[qi, h] = s_new

    if carry:
        @pl.when(pl.program_id(1) == pl.num_programs(1) - 1)
        def _():
            so_ref[...] = s_ref[...]


def _gdn(proj, small, conv_w, alog_pad, dtb_pad, norm_g, conv0, s0, *, bsz, t):
    carry = s0 is None
    r = GDN_ROWS
    n_tok = proj.shape[0]
    tail = CONV_W - 1
    if carry:
        assert t % r == 0
        seq_len, nseq, grid = r, 1, (bsz, t // r)
        row = lambda b, n: b * (t // r) + n
    else:
        assert r % t == 0 and bsz % (r // t) == 0
        seq_len, nseq, grid = t, r // t, (bsz // (r // t), 1)
        row = lambda b, n: b
    qb = COL_QKV_B // GDN_QK
    gb = COL_GATE_B // GDN_V
    rowspec = lambda cb: pl.BlockSpec((r, GDN_QK), lambda b, n: (row(b, n), cb))
    cwspec = lambda cb: pl.BlockSpec((CONV_W, GDN_QK), lambda b, n: (0, cb))
    vec = pl.BlockSpec((1, LANES), lambda b, n: (0, 0))
    in_specs = [rowspec(qb), rowspec(qb + 1), rowspec(qb + 2), rowspec(gb),
                pl.BlockSpec((r, LANES), lambda b, n: (row(b, n), 0)),
                cwspec(0), cwspec(1), cwspec(2), vec, vec, vec]
    args = [proj, proj, proj, proj, small, conv_w, conv_w, conv_w, alog_pad, dtb_pad, norm_g]
    hshape = (GDN_HEADS, r, LANES)
    scratch = ([pltpu.VMEM((8 + r, GDN_QK), F32)] * 3 + [pltpu.VMEM((r, GDN_QK), F32)] * 3
               + [pltpu.VMEM(hshape, BF16)] * 3 + [pltpu.VMEM(hshape, F32)]
               + [pltpu.VMEM(hshape, BF16)]
               + [pltpu.VMEM((GDN_HEADS, r, 2 * LANES), BF16)] * 2
               + [pltpu.VMEM(hshape, F32)] * 5)
    if carry:
        state = pl.BlockSpec((None, GDN_HEADS, GDN_DK, GDN_DV), lambda b, n: (b, 0, 0, 0))
        scratch += [pltpu.VMEM((GDN_HEADS, GDN_DK, GDN_DV), F32)]
    else:
        state = pl.BlockSpec((nseq, GDN_HEADS, GDN_DK, GDN_DV), lambda b, n: (b, 0, 0, 0))
        c0spec = lambda cb: pl.BlockSpec((nseq, tail, GDN_QK), lambda b, n: (b, 0, cb))
        in_specs += [c0spec(0), c0spec(1), c0spec(2), state]
        args += [conv0, conv0, conv0, s0]
        scratch += [pltpu.VMEM((8 + r, GDN_QK), F32)] * 3
    return pl.pallas_call(
        functools.partial(_gdn_kernel, seq_len=seq_len, carry=carry),
        grid=grid,
        in_specs=in_specs,
        out_specs=[pl.BlockSpec((r, GDN_V), lambda b, n: (row(b, n), 0)), state],
        out_shape=[jax.ShapeDtypeStruct((n_tok, GDN_V), BF16),
                   jax.ShapeDtypeStruct((bsz, GDN_HEADS, GDN_DK, GDN_DV), F32)],
        scratch_shapes=scratch,
        compiler_params=pltpu.CompilerParams(
            dimension_semantics=("arbitrary", "arbitrary"), vmem_limit_bytes=VMEM_LIMIT),
        name="gdn",
    )(*args)


def _merge_kernel(oa_ref, ob_ref, wa_ref, wb_ref, ma_ref, mb_ref, o_ref):
    ya = jnp.dot(oa_ref[...].astype(BF16), wa_ref[...], preferred_element_type=F32)
    yb = jnp.dot(ob_ref[...].astype(BF16), wb_ref[...], preferred_element_type=F32)
    merged = _sigmoid(ma_ref[...]) * ya + _sigmoid(mb_ref[...]) * yb
    o_ref[...] = merged.astype(o_ref.dtype)


def _merge(o_a, o_b, w_br_a, w_br_b, proj):
    n_tok = o_a.shape[0]
    tm = min(1024, n_tok)
    tn = 512
    ma, mb = COL_MA // tn, COL_MB // tn
    return pl.pallas_call(
        _merge_kernel,
        grid=(n_tok // tm, D_MODEL // tn),
        in_specs=[
            pl.BlockSpec((tm, GLA_V), lambda i, j: (i, 0)),
            pl.BlockSpec((tm, GDN_V), lambda i, j: (i, 0)),
            pl.BlockSpec((GLA_V, tn), lambda i, j: (0, j)),
            pl.BlockSpec((GDN_V, tn), lambda i, j: (0, j)),
            pl.BlockSpec((tm, tn), lambda i, j: (i, ma + j)),
            pl.BlockSpec((tm, tn), lambda i, j: (i, mb + j)),
        ],
        out_specs=pl.BlockSpec((tm, tn), lambda i, j: (i, j)),
        out_shape=jax.ShapeDtypeStruct((n_tok, D_MODEL), BF16),
        compiler_params=pltpu.CompilerParams(
            dimension_semantics=("arbitrary", "arbitrary"), vmem_limit_bytes=VMEM_LIMIT),
        name="merge",
    )(o_a, o_b, w_br_a, w_br_b, proj, proj)


def _out_kernel(m_ref, w_ref, x_ref, g_ref, y_ref):
    out = x_ref[...] + jnp.dot(m_ref[...], w_ref[...], preferred_element_type=F32)
    inv = lax.rsqrt(jnp.mean(out * out, axis=-1, keepdims=True) + EPS)
    y_ref[...] = out * inv * g_ref[...]


def _out_proj(merged, w_out, x2d, final_g):
    n_tok = merged.shape[0]
    tm = min(512, n_tok)
    return pl.pallas_call(
        _out_kernel,
        grid=(n_tok // tm,),
        in_specs=[
            pl.BlockSpec((tm, D_MODEL), lambda i: (i, 0)),
            pl.BlockSpec((D_MODEL, D_MODEL), lambda i: (0, 0)),
            pl.BlockSpec((tm, D_MODEL), lambda i: (i, 0)),
            pl.BlockSpec((1, D_MODEL), lambda i: (0, 0)),
        ],
        out_specs=pl.BlockSpec((tm, D_MODEL), lambda i: (i, 0)),
        out_shape=jax.ShapeDtypeStruct((n_tok, D_MODEL), F32),
        compiler_params=pltpu.CompilerParams(
            dimension_semantics=("arbitrary",), vmem_limit_bytes=VMEM_LIMIT),
        name="out_proj",
    )(merged, w_out, x2d, final_g)


def _layer(x, s_gla, s_gdn, conv_buf, p, *, gla_rows):
    bsz, t, d = x.shape
    x2d = x.reshape(bsz * t, d)
    proj, small = _in_proj(x2d, p["ln_g"], p["w_main"], p["w_small"])
    o_a, s_gla_new = _gla(proj, small, p["w_alpha2"], p["b_alpha"], p["gla_norm_g"], s_gla,
                          bsz=bsz, t=t, rows=gla_rows)
    o_b, s_gdn_new = _gdn(proj, small, p["conv_w"], p["alog_pad"], p["dtb_pad"],
                          p["gdn_norm_g"], conv_buf, s_gdn, bsz=bsz, t=t)
    merged = _merge(o_a, o_b, p["w_br_a"], p["w_br_b"], proj)
    y = _out_proj(merged, p["w_out"], x2d, p["final_g"])
    tail = CONV_W - 1
    qkv = proj.reshape(bsz, t, N_MAIN)[:, t - tail:, COL_QKV_B:COL_QKV_B + CONV_CH]
    return y.reshape(bsz, t, d), s_gla_new, s_gdn_new, qkv


def _prep_params(ln_in_g, w_in, w_alpha2, b_alpha, conv_w, a_log, dt_bias, gla_norm_g,
                 gdn_norm_g, w_br_a, w_br_b, w_out, final_norm_g):
    w_main = _prep_w_main(w_in)
    lr0, bd0 = W_IN_OFFS[3] - SM_LR, W_IN_OFFS[6] - SM_BETA
    assert lr0 % LANES == 0 and bd0 % LANES == 0 and W_IN_OFFS[7] - bd0 == SM_DEC
    lane = jnp.arange(LANES)
    w_small = jnp.where(lane < SM_BETA, w_in[:, lr0:lr0 + LANES],
                        jnp.where(lane < SM_DEC + GDN_HEADS, w_in[:, bd0:bd0 + LANES], 0.0)
                        ).astype(BF16)
    lane_pad = lambda v: jnp.zeros((1, LANES), F32).at[0, SM_DEC:SM_DEC + GDN_HEADS].set(v)
    return {
        "ln_g": ln_in_g.reshape(1, D_MODEL),
        "w_main": w_main,
        "w_small": w_small,
        "w_alpha2": w_alpha2,
        "b_alpha": b_alpha.reshape(1, GLA_QK),
        "conv_w": conv_w,
        "alog_pad": lane_pad(a_log),
        "dtb_pad": lane_pad(dt_bias),
        "gla_norm_g": gla_norm_g.reshape(1, GLA_DV),
        "gdn_norm_g": gdn_norm_g.reshape(1, GDN_DV),
        "w_br_a": w_br_a.astype(BF16),
        "w_br_b": w_br_b.astype(BF16),
        "w_out": w_out.astype(BF16),
        "final_g": final_norm_g.reshape(1, D_MODEL),
    }


def kernel(x_prompt, x_sample, state_gla, state_gdn, state_conv, ln_in_g, w_in, w_alpha2,
           b_alpha, conv_w, a_log, dt_bias, gla_norm_g, gdn_norm_g, w_br_a, w_br_b, w_out,
           final_norm_g):
    assert ln_in_g.shape[0] == 1, "single layer"
    p = _prep_params(ln_in_g[0], w_in[0], w_alpha2[0], b_alpha[0], conv_w[0], a_log[0],
                     dt_bias[0], gla_norm_g[0], gdn_norm_g[0], w_br_a[0], w_br_b[0], w_out[0],
                     final_norm_g)
    yp, gla_p, gdn_p, conv_p = _layer(x_prompt, None, None, None, p, gla_rows=GLA_PROMPT_ROWS)
    ys, gla_s, gdn_s, conv_s = _layer(x_sample, state_gla[0], state_gdn[0], state_conv[0], p,
                                      gla_rows=GLA_SAMPLE_ROWS)
    return (yp, ys, gla_p[None], gdn_p[None], conv_p[None], gla_s[None], gdn_s[None],
            conv_s[None])
```

```python
import functools

import jax
import jax.numpy as jnp
from jax import lax
from jax.experimental import pallas as pl
from jax.experimental.pallas import tpu as pltpu

F32 = jnp.float32
BF16 = jnp.bfloat16

D_MODEL = 2048
GLA_HEADS = 4
GLA_DK = 256
GLA_DV = 512
GLA_RANK = 16
GLA_TAU = 16.0
GDN_HEADS = 16
GDN_DK = 128
GDN_DV = 128
CONV_W = 4
CHUNK = 64
EPS = 1e-6

GLA_QK = GLA_HEADS * GLA_DK
GLA_V = GLA_HEADS * GLA_DV
GDN_QK = GDN_HEADS * GDN_DK
GDN_V = GDN_HEADS * GDN_DV
CONV_CH = 2 * GDN_QK + GDN_V

LANES = 128
COL_QA = 0
COL_KA = COL_QA + GLA_QK
COL_VA = COL_KA + GLA_QK
COL_GATE_A = COL_VA + GLA_V
COL_QKV_B = COL_GATE_A + GLA_V
COL_GATE_B = COL_QKV_B + CONV_CH
COL_MA = COL_GATE_B + GDN_V
COL_MB = COL_MA + D_MODEL
N_MAIN = COL_MB + D_MODEL
SM_LR = 0
SM_BETA = GLA_RANK
SM_DEC = SM_BETA + GDN_HEADS

VMEM_LIMIT = 56 * 1024 * 1024


def _bdot(a, b):
    return jnp.dot(a.astype(BF16), b.astype(BF16), preferred_element_type=F32)


def _bdot_nt(a, b):
    return lax.dot_general(a.astype(BF16), b.astype(BF16), (((1,), (1,)), ((), ())),
                           preferred_element_type=F32)


def _bdot_tn(a, b):
    return lax.dot_general(a.astype(BF16), b.astype(BF16), (((0,), (0,)), ((), ())),
                           preferred_element_type=F32)


def _hdot(a, b):
    return jnp.dot(a, b, precision=lax.Precision.HIGHEST, preferred_element_type=F32)


def _dot(a, b):
    return jnp.dot(a, b, preferred_element_type=F32)


def _sigmoid(x):
    return 1.0 / (1.0 + jnp.exp(-x))


def _silu(x):
    return x * _sigmoid(x)


def _softplus(x):
    return jnp.maximum(x, 0.0) + jnp.log(1.0 + jnp.exp(-jnp.abs(x)))


def _tri_incl(c):
    r = lax.broadcasted_iota(jnp.int32, (c, c), 0)
    col = lax.broadcasted_iota(jnp.int32, (c, c), 1)
    return r >= col, r > col


W_IN_SIZES = (GLA_QK, GLA_QK, GLA_V, GLA_RANK, GLA_V, CONV_CH, GDN_HEADS, GDN_HEADS, GDN_V,
              D_MODEL, D_MODEL)
W_IN_OFFS = tuple(sum(W_IN_SIZES[:i]) for i in range(len(W_IN_SIZES) + 1))
PREP_STARTS = (0, COL_GATE_A, COL_GATE_B)
PREP_SHIFTS = (0, W_IN_OFFS[4] - COL_GATE_A, W_IN_OFFS[8] - COL_GATE_B)
PREP_ROWS = 512
PREP_HALO = 64


def _wprep_kernel(a_ref, nxt_ref, o_ref):
    j = pl.program_id(0)
    bounds = tuple(s // PREP_ROWS for s in PREP_STARTS) + (N_MAIN // PREP_ROWS,)
    for run, shift in enumerate(PREP_SHIFTS):
        @pl.when((j >= bounds[run]) & (j < bounds[run + 1]))
        def _(shift=shift):
            if shift == 0:
                o_ref[...] = a_ref[...].astype(BF16)
            else:
                x = jnp.concatenate([a_ref[...], nxt_ref[...]], axis=0)
                o_ref[...] = x[shift:shift + PREP_ROWS, :].astype(BF16)


def _prep_w_main(wt):
    d = wt.shape[1]
    assert all(s % PREP_ROWS == 0 for s in PREP_STARTS)
    assert max(PREP_SHIFTS) <= PREP_HALO and all(s % 16 == 0 for s in PREP_SHIFTS)
    return pl.pallas_call(
        _wprep_kernel,
        grid=(N_MAIN // PREP_ROWS,),
        in_specs=[
            pl.BlockSpec((PREP_ROWS, d), lambda j: (j, 0)),
            pl.BlockSpec((PREP_HALO, d), lambda j: ((j + 1) * (PREP_ROWS // PREP_HALO), 0)),
        ],
        out_specs=pl.BlockSpec((PREP_ROWS, d), lambda j: (j, 0)),
        out_shape=jax.ShapeDtypeStruct((N_MAIN, d), BF16),
        compiler_params=pltpu.CompilerParams(
            dimension_semantics=("arbitrary",), vmem_limit_bytes=VMEM_LIMIT),
        name="w_prep",
    )(wt, wt)


def _dot_nt(a, b):
    return lax.dot_general(a, b, (((1,), (1,)), ((), ())), preferred_element_type=F32)


def _in_proj_kernel(x_ref, g_ref, w_ref, ws_ref, o_ref, os_ref, h_ref):
    @pl.when(pl.program_id(1) == 0)
    def _():
        x = x_ref[...]
        inv = lax.rsqrt(jnp.mean(x * x, axis=-1, keepdims=True) + EPS)
        h = (x * inv * g_ref[...]).astype(BF16)
        h_ref[...] = h
        os_ref[...] = _dot_nt(h, ws_ref[...])

    o_ref[...] = _dot_nt(h_ref[...], w_ref[...])


def _in_proj(x2d, ln_g, w_main, w_small):
    n_tok, d = x2d.shape
    tm = min(1024, n_tok)
    tn = 1024
    return pl.pallas_call(
        _in_proj_kernel,
        grid=(n_tok // tm, N_MAIN // tn),
        in_specs=[
            pl.BlockSpec((tm, d), lambda i, j: (i, 0)),
            pl.BlockSpec((1, d), lambda i, j: (0, 0)),
            pl.BlockSpec((tn, d), lambda i, j: (j, 0)),
            pl.BlockSpec((LANES, d), lambda i, j: (0, 0)),
        ],
        out_specs=[
            pl.BlockSpec((tm, tn), lambda i, j: (i, j)),
            pl.BlockSpec((tm, LANES), lambda i, j: (i, 0)),
        ],
        out_shape=[jax.ShapeDtypeStruct((n_tok, N_MAIN), F32),
                   jax.ShapeDtypeStruct((n_tok, LANES), F32)],
        scratch_shapes=[pltpu.VMEM((tm, d), BF16)],
        compiler_params=pltpu.CompilerParams(
            dimension_semantics=("arbitrary", "arbitrary"), vmem_limit_bytes=VMEM_LIMIT),
        name="in_proj",
    )(x2d, ln_g, w_main, w_small)


GLA_PROMPT_ROWS = 512
GLA_SAMPLE_ROWS = 32


def _gla_kernel(*refs, seq_len, rows, carry):
    n_in = 8 if carry else 9
    q_ref, k_ref, v_ref, gate_ref, sm_ref, wal_ref, bal_ref, ng_ref = refs[:8]
    s0_ref = None if carry else refs[8]
    o_ref, so_ref = refs[n_in:n_in + 2]
    qd_ref, ke_ref, ebl_ref, oacc_ref = refs[n_in + 2:n_in + 6]
    s_ref = refs[n_in + 6] if carry else None
    r, L = rows, seq_len
    nsub = r // L
    g = min(r, CHUNK)

    if carry:
        @pl.when(pl.program_id(1) == 0)
        def _():
            s_ref[...] = jnp.zeros_like(s_ref)

    ri = lax.broadcasted_iota(jnp.int32, (r, r), 0)
    ci = lax.broadcasted_iota(jnp.int32, (r, r), 1)
    incl = ((ri // L) == (ci // L)) & (ri >= ci)
    gi = lax.broadcasted_iota(jnp.int32, (g, 2 * LANES), 0)
    gl = lax.broadcasted_iota(jnp.int32, (g, 2 * LANES), 1)
    gc = gl % g
    tri3 = ((gl < 3 * g) & ((gi // L) == (gc // L)) & (gi >= gc)).astype(BF16)

    def seg_cumsum(x):
        outs = []
        for gq in range(r // g):
            xg = x[gq * g:(gq + 1) * g]
            hi, r1 = _split_f32(xg)
            mid, r2 = _split_f32(r1)
            parts = [hi, mid, r2.astype(BF16)]
            if 3 * g < 2 * LANES:
                parts.append(jnp.zeros((2 * LANES - 3 * g, xg.shape[1]), BF16))
            outs.append(_dot(tri3, jnp.concatenate(parts, axis=0)))
        return outs[0] if len(outs) == 1 else jnp.concatenate(outs, axis=0)

    for h in range(GLA_HEADS):
        lk = slice(h * GLA_DK, (h + 1) * GLA_DK)
        lv = slice(h * GLA_DV, (h + 1) * GLA_DV)
        q = q_ref[:, lk] * (GLA_DK ** -0.5)
        k = k_ref[:, lk]
        lr = sm_ref[:, SM_LR:SM_LR + GLA_RANK]
        z = _bdot(lr, wal_ref[:, lk]) + bal_ref[:, lk]
        la = -_softplus(-z) / GLA_TAU
        b = seg_cumsum(la)
        lasts = [jnp.broadcast_to(b[(qi + 1) * L - 1:(qi + 1) * L, :], (L, GLA_DK))
                 for qi in range(nsub)]
        b_last = lasts[0] if nsub == 1 else jnp.concatenate(lasts, axis=0)
        q_d = q * jnp.exp(b)
        k_d = k * jnp.exp(-b)
        att = jnp.where(incl, _bdot_nt(q_d, k_d), 0.0)
        oacc_ref[h] = _bdot(att, v_ref[:, lv])
        qd_ref[h] = q_d
        ke_ref[h] = k * jnp.exp(b_last - b)
        ebl_ref[h] = jnp.exp(b_last)

    for qi in range(nsub):
        rs = slice(qi * L, (qi + 1) * L)
        for h in range(GLA_HEADS):
            lv = slice(h * GLA_DV, (h + 1) * GLA_DV)
            s = s_ref[h] if carry else s0_ref[qi, h]
            oacc_ref[h, rs, :] = oacc_ref[h, rs, :] + _bdot(qd_ref[h, rs, :], s)
            kv = _bdot_tn(ke_ref[h, rs, :], v_ref[rs, lv])
            eb = ebl_ref[h, qi * L:qi * L + 1, :]
            col = jnp.transpose(jnp.broadcast_to(eb, (LANES, GLA_DK)))
            s_new = s * jnp.concatenate([col] * (GLA_DV // LANES), axis=1) + kv
            if carry:
                s_ref[h] = s_new
            else:
                so_ref[qi, h] = s_new

    for h in range(GLA_HEADS):
        lv = slice(h * GLA_DV, (h + 1) * GLA_DV)
        o = oacc_ref[h]
        inv = lax.rsqrt(jnp.mean(o * o, axis=-1, keepdims=True) + EPS)
        on = o * inv * ng_ref[...]
        o_ref[:, lv] = (on * _silu(gate_ref[:, lv])).astype(o_ref.dtype)

    if carry:
        @pl.when(pl.program_id(1) == pl.num_programs(1) - 1)
        def _():
            so_ref[...] = s_ref[...]


def _gla(proj, small, w_alpha2, b_alpha, norm_g, s0, *, bsz, t, rows):
    carry = s0 is None
    r = rows
    n_tok = proj.shape[0]
    if carry:
        seq_len = min(CHUNK, t)
        assert t % r == 0 and r % seq_len == 0
        grid = (bsz, t // r)
        row = lambda b, n: b * (t // r) + n
    else:
        assert t <= CHUNK and r % t == 0 and bsz % (r // t) == 0
        seq_len = t
        grid = (bsz // (r // t), 1)
        row = lambda b, n: b
    nsub = r // seq_len
    qb, kb = COL_QA // GLA_QK, COL_KA // GLA_QK
    vb, gb = COL_VA // GLA_V, COL_GATE_A // GLA_V
    in_specs = [
        pl.BlockSpec((r, GLA_QK), lambda b, n: (row(b, n), qb)),
        pl.BlockSpec((r, GLA_QK), lambda b, n: (row(b, n), kb)),
        pl.BlockSpec((r, GLA_V), lambda b, n: (row(b, n), vb)),
        pl.BlockSpec((r, GLA_V), lambda b, n: (row(b, n), gb)),
        pl.BlockSpec((r, LANES), lambda b, n: (row(b, n), 0)),
        pl.BlockSpec((GLA_RANK, GLA_QK), lambda b, n: (0, 0)),
        pl.BlockSpec((1, GLA_QK), lambda b, n: (0, 0)),
        pl.BlockSpec((1, GLA_DV), lambda b, n: (0, 0)),
    ]
    args = [proj, proj, proj, proj, small, w_alpha2, b_alpha, norm_g]
    scratch = [pltpu.VMEM((GLA_HEADS, r, GLA_DK), F32)] * 3 + [
        pltpu.VMEM((GLA_HEADS, r, GLA_DV), F32)]
    if carry:
        state = pl.BlockSpec((None, GLA_HEADS, GLA_DK, GLA_DV), lambda b, n: (b, 0, 0, 0))
        scratch += [pltpu.VMEM((GLA_HEADS, GLA_DK, GLA_DV), F32)]
    else:
        state = pl.BlockSpec((nsub, GLA_HEADS, GLA_DK, GLA_DV), lambda b, n: (b, 0, 0, 0))
        in_specs += [state]
        args += [s0]
    return pl.pallas_call(
        functools.partial(_gla_kernel, seq_len=seq_len, rows=r, carry=carry),
        grid=grid,
        in_specs=in_specs,
        out_specs=[pl.BlockSpec((r, GLA_V), lambda b, n: (row(b, n), 0)), state],
        out_shape=[jax.ShapeDtypeStruct((n_tok, GLA_V), BF16),
                   jax.ShapeDtypeStruct((bsz, GLA_HEADS, GLA_DK, GLA_DV), F32)],
        scratch_shapes=scratch,
        compiler_params=pltpu.CompilerParams(
            dimension_semantics=("arbitrary", "arbitrary"), vmem_limit_bytes=VMEM_LIMIT),
        name="gla",
    )(*args)


GDN_ROWS = 64
HALF = LANES // 2


def _split_f32(x):
    hi = x.astype(BF16)
    return hi, x - hi.astype(F32)


def _lhs3(x, low_half):
    hi, lo = _split_f32(x)
    t0 = jnp.where(low_half, x, lo).astype(BF16)
    t1 = jnp.where(low_half, hi, jnp.zeros_like(hi))
    return jnp.concatenate([t0, t1], axis=1)


def _rhs3(hi, lo):
    return jnp.concatenate([hi, hi, lo, lo], axis=0)


def _gdn_kernel(*refs, seq_len, carry):
    n_in = 11 if carry else 15
    (q_ref, k_ref, v_ref, gate_ref, sm_ref, cwq_ref, cwk_ref, cwv_ref, alog_ref, dtb_ref,
     ng_ref) = refs[:11]
    cq0_ref, ck0_ref, cv0_ref, s0_ref = (None,) * 4 if carry else refs[11:15]
    o_ref, so_ref = refs[n_in:n_in + 2]
    (eq_ref, ek_ref, ev_ref, cq_ref, ck_ref, cv_ref, nhi_ref, nlo_ref, nl0_ref, p_ref, att_ref,
     rhi_ref, rlo_ref, qg_ref, ke_ref, u_ref, w_ref, os_ref) = refs[n_in + 2:n_in + 20]
    if carry:
        (s_ref,) = refs[n_in + 20:]
        hq_ref = hk_ref = hv_ref = None
    else:
        hq_ref, hk_ref, hv_ref = refs[n_in + 20:]
        s_ref = None
    r = GDN_ROWS
    L = seq_len
    nseq = r // L
    tail = CONV_W - 1
    pad = 8

    if carry:
        @pl.when(pl.program_id(1) == 0)
        def _():
            s_ref[...] = jnp.zeros_like(s_ref)
            for e_ref in (eq_ref, ek_ref, ev_ref):
                e_ref[0:pad, :] = jnp.zeros((pad, GDN_QK), F32)
    else:
        @pl.when(pl.program_id(0) == 0)
        def _():
            for e_ref in (eq_ref, ek_ref, ev_ref, hq_ref, hk_ref, hv_ref):
                e_ref[...] = jnp.zeros_like(e_ref)

    row_t = lax.broadcasted_iota(jnp.int32, (r, 1), 0) % L

    def conv(e_ref, x_ref, w_ref, out_ref, h_ref, c0_ref):
        e_ref[pad:pad + r, :] = x_ref[...]
        if not carry:
            for qi in range(nseq):
                h_ref[qi * L + pad - tail:qi * L + pad, :] = c0_ref[qi]
        acc = None
        for i in range(CONV_W):
            sh = tail - i
            xs = e_ref[pad - sh:pad - sh + r, :]
            if not carry and sh > 0:
                xs = jnp.where(row_t >= sh, xs, h_ref[pad - sh:pad - sh + r, :])
            term = xs * w_ref[i:i + 1, :]
            acc = term if acc is None else acc + term
        if carry:
            e_ref[pad - tail:pad, :] = x_ref[r - tail:r, :]
        out_ref[...] = _silu(acc)

    conv(eq_ref, q_ref, cwq_ref, cq_ref, hq_ref, cq0_ref)
    conv(ek_ref, k_ref, cwk_ref, ck_ref, hk_ref, ck0_ref)
    conv(ev_ref, v_ref, cwv_ref, cv_ref, hv_ref, cv0_ref)

    ri = lax.broadcasted_iota(jnp.int32, (r, LANES), 0)
    li = lax.broadcasted_iota(jnp.int32, (r, LANES), 1)
    cj = li % HALF
    low_half = li < HALF
    same = (ri // L) == (cj // L)
    incl = same & (ri >= cj)
    strict = same & (ri > cj)
    eye = (ri == cj).astype(F32)

    sm = sm_ref[...]
    beta_all = _sigmoid(sm)
    g_all = -jnp.exp(alog_ref[...]) * _softplus(sm + dtb_ref[...])
    g_hi, g_r1 = _split_f32(g_all)
    g_mid, g_r2 = _split_f32(g_r1)
    g_lo = g_r2.astype(BF16)
    tri = incl.astype(BF16)
    tri_lhs = jnp.concatenate([tri, jnp.where(low_half, tri, jnp.zeros_like(tri))], axis=1)
    gh_all = _dot(tri_lhs, jnp.concatenate([g_hi, g_mid, g_lo, g_lo], axis=0))
    gh_t = jnp.transpose(jnp.concatenate([gh_all, gh_all], axis=0))
    glast_all = jnp.concatenate(
        [jnp.broadcast_to(gh_all[(qi + 1) * L - 1:(qi + 1) * L, :], (L, LANES))
         for qi in range(nseq)], axis=0)
    egh_all = jnp.exp(gh_all)
    kend_all = jnp.exp(glast_all - gh_all)

    for h in range(GDN_HEADS):
        ls = slice(h * GDN_DK, (h + 1) * GDN_DK)
        q = cq_ref[:, ls]
        k = ck_ref[:, ls]
        v = cv_ref[:, ls]
        q = q * lax.rsqrt(jnp.sum(q * q, axis=-1, keepdims=True) + EPS) * (GDN_DK ** -0.5)
        k = k * lax.rsqrt(jnp.sum(k * k, axis=-1, keepdims=True) + EPS)
        col = slice(SM_DEC + h, SM_DEC + h + 1)
        beta = beta_all[:, SM_BETA + h:SM_BETA + h + 1]
        gh = gh_all[:, col]
        egh = egh_all[:, col]
        decay = jnp.where(incl, jnp.exp(gh - gh_t[col, :]), 0.0)
        kb = k.astype(BF16)
        qkk = lax.dot_general(jnp.concatenate([q.astype(BF16), kb], axis=0),
                              jnp.concatenate([kb, kb], axis=0),
                              (((1,), (1,)), ((), ())), preferred_element_type=F32)
        att_ref[h] = (qkk[:r] * decay).astype(BF16)
        nm = jnp.where(strict, -(beta * qkk[r:] * decay), 0.0)
        n_hi, n_lo = _split_f32(nm)
        nhi_ref[h] = n_hi
        nlo_ref[h] = n_lo.astype(BF16)
        nl0_ref[h] = jnp.where(low_half, nm, n_lo).astype(BF16)
        p_ref[h] = eye + nm
        r_hi, r_lo = _split_f32(jnp.concatenate([v * beta, k * (beta * egh)], axis=1))
        rhi_ref[h] = r_hi
        rlo_ref[h] = r_lo.astype(BF16)
        qg_ref[h] = q * egh
        ke_ref[h] = k * kend_all[:, col]

    def n_lhs(h):
        n_hi = nhi_ref[h]
        return jnp.concatenate([nl0_ref[h], jnp.where(low_half, n_hi, jnp.zeros_like(n_hi))],
                               axis=1)

    p = 2
    while p < L:
        for h in range(GDN_HEADS):
            n2 = _dot(n_lhs(h), _rhs3(nhi_ref[h], nlo_ref[h]))
            n_hi, n_lo = _split_f32(n2)
            nhi_ref[h] = n_hi
            nlo_ref[h] = n_lo.astype(BF16)
            nl0_ref[h] = jnp.where(low_half, n2, n_lo).astype(BF16)
        for h in range(GDN_HEADS):
            pm = p_ref[h]
            p_ref[h] = pm + _dot(_lhs3(pm, low_half), _rhs3(nhi_ref[h], nlo_ref[h]))
        p *= 2

    for h in range(GDN_HEADS):
        sol = _dot(_lhs3(p_ref[h], low_half), _rhs3(rhi_ref[h], rlo_ref[h]))
        u_ref[h] = sol[:, :GDN_DV]
        w_ref[h] = sol[:, GDN_DV:]

    def state(qi, h):
        return s_ref[h] if carry else s0_ref[qi, h]

    for h in range(GDN_HEADS):
        for qi in range(nseq):
            rs = slice(qi * L, (qi + 1) * L)
            lhs = jnp.concatenate([w_ref[h, rs, :], qg_ref[h, rs, :]], axis=0).astype(BF16)
            res = _dot(lhs, state(qi, h).astype(BF16))
            u_ref[h, rs, :] = u_ref[h, rs, :] - res[:L]
            os_ref[h, rs, :] = res[L:]

    for h in range(GDN_HEADS):
        ls = slice(h * GDN_DV, (h + 1) * GDN_DV)
        v_new = u_ref[h]
        o = os_ref[h] + _dot(att_ref[h][:, :HALF], v_new.astype(BF16))
        inv = lax.rsqrt(jnp.mean(o * o, axis=-1, keepdims=True) + EPS)
        on = o * inv * ng_ref[...]
        o_ref[:, ls] = (on * _silu(gate_ref[:, ls])).astype(o_ref.dtype)
        for qi in range(nseq):
            rs = slice(qi * L, (qi + 1) * L)
            kv = lax.dot_general(ke_ref[h, rs, :].astype(BF16), v_new[rs].astype(BF16),
                                 (((0,), (0,)), ((), ())), preferred_element_type=F32)
            eg = egh_all[(qi + 1) * L - 1:(qi + 1) * L, SM_DEC + h:SM_DEC + h + 1]
            s_new = state(qi, h) * eg + kv
            if carry:
                s_ref[h] = s_new
            else:
                so_ref[qi, h] = s_new

    if carry:
        @pl.when(pl.program_id(1) == pl.num_programs(1) - 1)
        def _():
            so_ref[...] = s_ref[...]


def _gdn(proj, small, conv_w, alog_pad, dtb_pad, norm_g, conv0, s0, *, bsz, t):
    carry = s0 is None
    r = GDN_ROWS
    n_tok = proj.shape[0]
    tail = CONV_W - 1
    if carry:
        assert t % r == 0
        seq_len, nseq, grid = r, 1, (bsz, t // r)
        row = lambda b, n: b * (t // r) + n
    else:
        assert r % t == 0 and bsz % (r // t) == 0
        seq_len, nseq, grid = t, r // t, (bsz // (r // t), 1)
        row = lambda b, n: b
    qb = COL_QKV_B // GDN_QK
    gb = COL_GATE_B // GDN_V
    rowspec = lambda cb: pl.BlockSpec((r, GDN_QK), lambda b, n: (row(b, n), cb))
    cwspec = lambda cb: pl.BlockSpec((CONV_W, GDN_QK), lambda b, n: (0, cb))
    vec = pl.BlockSpec((1, LANES), lambda b, n: (0, 0))
    in_specs = [rowspec(qb), rowspec(qb + 1), rowspec(qb + 2), rowspec(gb),
                pl.BlockSpec((r, LANES), lambda b, n: (row(b, n), 0)),
                cwspec(0), cwspec(1), cwspec(2), vec, vec, vec]
    args = [proj, proj, proj, proj, small, conv_w, conv_w, conv_w, alog_pad, dtb_pad, norm_g]
    hshape = (GDN_HEADS, r, LANES)
    scratch = ([pltpu.VMEM((8 + r, GDN_QK), F32)] * 3 + [pltpu.VMEM((r, GDN_QK), F32)] * 3
               + [pltpu.VMEM(hshape, BF16)] * 3 + [pltpu.VMEM(hshape, F32)]
               + [pltpu.VMEM(hshape, BF16)]
               + [pltpu.VMEM((GDN_HEADS, r, 2 * LANES), BF16)] * 2
               + [pltpu.VMEM(hshape, F32)] * 5)
    if carry:
        state = pl.BlockSpec((None, GDN_HEADS, GDN_DK, GDN_DV), lambda b, n: (b, 0, 0, 0))
        scratch += [pltpu.VMEM((GDN_HEADS, GDN_DK, GDN_DV), F32)]
    else:
        state = pl.BlockSpec((nseq, GDN_HEADS, GDN_DK, GDN_DV), lambda b, n: (b, 0, 0, 0))
        c0spec = lambda cb: pl.BlockSpec((nseq, tail, GDN_QK), lambda b, n: (b, 0, cb))
        in_specs += [c0spec(0), c0spec(1), c0spec(2), state]
        args += [conv0, conv0, conv0, s0]
        scratch += [pltpu.VMEM((8 + r, GDN_QK), F32)] * 3
    return pl.pallas_call(
        functools.partial(_gdn_kernel, seq_len=seq_len, carry=carry),
        grid=grid,
        in_specs=in_specs,
        out_specs=[pl.BlockSpec((r, GDN_V), lambda b, n: (row(b, n), 0)), state],
        out_shape=[jax.ShapeDtypeStruct((n_tok, GDN_V), BF16),
                   jax.ShapeDtypeStruct((bsz, GDN_HEADS, GDN_DK, GDN_DV), F32)],
        scratch_shapes=scratch,
        compiler_params=pltpu.CompilerParams(
            dimension_semantics=("arbitrary", "arbitrary"), vmem_limit_bytes=VMEM_LIMIT),
        name="gdn",
    )(*args)


def _merge_kernel(oa_ref, ob_ref, wa_ref, wb_ref, ma_ref, mb_ref, o_ref):
    ya = jnp.dot(oa_ref[...].astype(BF16), wa_ref[...], preferred_element_type=F32)
    yb = jnp.dot(ob_ref[...].astype(BF16), wb_ref[...], preferred_element_type=F32)
    merged = _sigmoid(ma_ref[...]) * ya + _sigmoid(mb_ref[...]) * yb
    o_ref[...] = merged.astype(o_ref.dtype)


def _merge(o_a, o_b, w_br_a, w_br_b, proj):
    n_tok = o_a.shape[0]
    tm = min(1024, n_tok)
    tn = 512
    ma, mb = COL_MA // tn, COL_MB // tn
    return pl.pallas_call(
        _merge_kernel,
        grid=(n_tok // tm, D_MODEL // tn),
        in_specs=[
            pl.BlockSpec((tm, GLA_V), lambda i, j: (i, 0)),
            pl.BlockSpec((tm, GDN_V), lambda i, j: (i, 0)),
            pl.BlockSpec((GLA_V, tn), lambda i, j: (0, j)),
            pl.BlockSpec((GDN_V, tn), lambda i, j: (0, j)),
            pl.BlockSpec((tm, tn), lambda i, j: (i, ma + j)),
            pl.BlockSpec((tm, tn), lambda i, j: (i, mb + j)),
        ],
        out_specs=pl.BlockSpec((tm, tn), lambda i, j: (i, j)),
        out_shape=jax.ShapeDtypeStruct((n_tok, D_MODEL), BF16),
        compiler_params=pltpu.CompilerParams(
            dimension_semantics=("arbitrary", "arbitrary"), vmem_limit_bytes=VMEM_LIMIT),
        name="merge",
    )(o_a, o_b, w_br_a, w_br_b, proj, proj)


def _out_kernel(m_ref, w_ref, x_ref, g_ref, y_ref):
    out = x_ref[...] + jnp.dot(m_ref[...], w_ref[...], preferred_element_type=F32)
    inv = lax.rsqrt(jnp.mean(out * out, axis=-1, keepdims=True) + EPS)
    y_ref[...] = out * inv * g_ref[...]


def _out_proj(merged, w_out, x2d, final_g):
    n_tok = merged.shape[0]
    tm = min(512, n_tok)
    return pl.pallas_call(
        _out_kernel,
        grid=(n_tok // tm,),
        in_specs=[
            pl.BlockSpec((tm, D_MODEL), lambda i: (i, 0)),
            pl.BlockSpec((D_MODEL, D_MODEL), lambda i: (0, 0)),
            pl.BlockSpec((tm, D_MODEL), lambda i: (i, 0)),
            pl.BlockSpec((1, D_MODEL), lambda i: (0, 0)),
        ],
        out_specs=pl.BlockSpec((tm, D_MODEL), lambda i: (i, 0)),
        out_shape=jax.ShapeDtypeStruct((n_tok, D_MODEL), F32),
        compiler_params=pltpu.CompilerParams(
            dimension_semantics=("arbitrary",), vmem_limit_bytes=VMEM_LIMIT),
        name="out_proj",
    )(merged, w_out, x2d, final_g)


def _layer(x, s_gla, s_gdn, conv_buf, p, *, gla_rows):
    bsz, t, d = x.shape
    x2d = x.reshape(bsz * t, d)
    proj, small = _in_proj(x2d, p["ln_g"], p["w_main"], p["w_small"])
    o_a, s_gla_new = _gla(proj, small, p["w_alpha2"], p["b_alpha"], p["gla_norm_g"], s_gla,
                          bsz=bsz, t=t, rows=gla_rows)
    o_b, s_gdn_new = _gdn(proj, small, p["conv_w"], p["alog_pad"], p["dtb_pad"],
                          p["gdn_norm_g"], conv_buf, s_gdn, bsz=bsz, t=t)
    merged = _merge(o_a, o_b, p["w_br_a"], p["w_br_b"], proj)
    y = _out_proj(merged, p["w_out"], x2d, p["final_g"])
    tail = CONV_W - 1
    qkv = proj.reshape(bsz, t, N_MAIN)[:, t - tail:, COL_QKV_B:COL_QKV_B + CONV_CH]
    return y.reshape(bsz, t, d), s_gla_new, s_gdn_new, qkv


def _prep_params(ln_in_g, w_in, w_alpha2, b_alpha, conv_w, a_log, dt_bias, gla_norm_g,
                 gdn_norm_g, w_br_a, w_br_b, w_out, final_norm_g):
    wt = jnp.transpose(w_in)
    w_main = _prep_w_main(wt)
    n_small = GLA_RANK + 2 * GDN_HEADS
    w_small = jnp.concatenate(
        [wt[W_IN_OFFS[3]:W_IN_OFFS[4]], wt[W_IN_OFFS[6]:W_IN_OFFS[8]],
         jnp.zeros((LANES - n_small, D_MODEL), F32)], axis=0).astype(BF16)
    lane_pad = lambda v: jnp.zeros((1, LANES), F32).at[0, SM_DEC:SM_DEC + GDN_HEADS].set(v)
    return {
        "ln_g": ln_in_g.reshape(1, D_MODEL),
        "w_main": w_main,
        "w_small": w_small,
        "w_alpha2": w_alpha2,
        "b_alpha": b_alpha.reshape(1, GLA_QK),
        "conv_w": conv_w,
        "alog_pad": lane_pad(a_log),
        "dtb_pad": lane_pad(dt_bias),
        "gla_norm_g": gla_norm_g.reshape(1, GLA_DV),
        "gdn_norm_g": gdn_norm_g.reshape(1, GDN_DV),
        "w_br_a": w_br_a.astype(BF16),
        "w_br_b": w_br_b.astype(BF16),
        "w_out": w_out.astype(BF16),
        "final_g": final_norm_g.reshape(1, D_MODEL),
    }


def kernel(x_prompt, x_sample, state_gla, state_gdn, state_conv, ln_in_g, w_in, w_alpha2,
           b_alpha, conv_w, a_log, dt_bias, gla_norm_g, gdn_norm_g, w_br_a, w_br_b, w_out,
           final_norm_g):
    assert ln_in_g.shape[0] == 1, "single layer"
    p = _prep_params(ln_in_g[0], w_in[0], w_alpha2[0], b_alpha[0], conv_w[0], a_log[0],
                     dt_bias[0], gla_norm_g[0], gdn_norm_g[0], w_br_a[0], w_br_b[0], w_out[0],
                     final_norm_g)
    yp, gla_p, gdn_p, conv_p = _layer(x_prompt, None, None, None, p, gla_rows=GLA_PROMPT_ROWS)
    ys, gla_s, gdn_s, conv_s = _layer(x_sample, state_gla[0], state_gdn[0], state_conv[0], p,
                                      gla_rows=GLA_SAMPLE_ROWS)
    return (yp, ys, gla_p[None], gdn_p[None], conv_p[None], gla_s[None], gdn_s[None],
            conv_s[None])
```

```python
import functools

import jax
import jax.numpy as jnp
from jax import lax
from jax.experimental import pallas as pl
from jax.experimental.pallas import tpu as pltpu

F32 = jnp.float32
BF16 = jnp.bfloat16

D_MODEL = 2048
GLA_HEADS = 4
GLA_DK = 256
GLA_DV = 512
GLA_RANK = 16
GLA_TAU = 16.0
GDN_HEADS = 16
GDN_DK = 128
GDN_DV = 128
CONV_W = 4
CHUNK = 64
EPS = 1e-6

GLA_QK = GLA_HEADS * GLA_DK
GLA_V = GLA_HEADS * GLA_DV
GDN_QK = GDN_HEADS * GDN_DK
GDN_V = GDN_HEADS * GDN_DV
CONV_CH = 2 * GDN_QK + GDN_V

LANES = 128
COL_QA = 0
COL_KA = COL_QA + GLA_QK
COL_VA = COL_KA + GLA_QK
COL_GATE_A = COL_VA + GLA_V
COL_QKV_B = COL_GATE_A + GLA_V
COL_GATE_B = COL_QKV_B + CONV_CH
COL_MA = COL_GATE_B + GDN_V
COL_MB = COL_MA + D_MODEL
N_MAIN = COL_MB + D_MODEL
SM_LR = 0
SM_BETA = GLA_RANK
SM_DEC = SM_BETA + GDN_HEADS

VMEM_LIMIT = 56 * 1024 * 1024


def _bdot(a, b):
    return jnp.dot(a.astype(BF16), b.astype(BF16), preferred_element_type=F32)


def _bdot_nt(a, b):
    return lax.dot_general(a.astype(BF16), b.astype(BF16), (((1,), (1,)), ((), ())),
                           preferred_element_type=F32)


def _bdot_tn(a, b):
    return lax.dot_general(a.astype(BF16), b.astype(BF16), (((0,), (0,)), ((), ())),
                           preferred_element_type=F32)


def _hdot(a, b):
    return jnp.dot(a, b, precision=lax.Precision.HIGHEST, preferred_element_type=F32)


def _dot(a, b):
    return jnp.dot(a, b, preferred_element_type=F32)


def _sigmoid(x):
    return 1.0 / (1.0 + jnp.exp(-x))


def _silu(x):
    return x * _sigmoid(x)


def _softplus(x):
    return jnp.maximum(x, 0.0) + jnp.log(1.0 + jnp.exp(-jnp.abs(x)))


def _tri_incl(c):
    r = lax.broadcasted_iota(jnp.int32, (c, c), 0)
    col = lax.broadcasted_iota(jnp.int32, (c, c), 1)
    return r >= col, r > col


W_IN_SIZES = (GLA_QK, GLA_QK, GLA_V, GLA_RANK, GLA_V, CONV_CH, GDN_HEADS, GDN_HEADS, GDN_V,
              D_MODEL, D_MODEL)
W_IN_OFFS = tuple(sum(W_IN_SIZES[:i]) for i in range(len(W_IN_SIZES) + 1))
PREP_STARTS = (0, COL_GATE_A, COL_GATE_B)
PREP_SHIFTS = (0, W_IN_OFFS[4] - COL_GATE_A, W_IN_OFFS[8] - COL_GATE_B)
PREP_ROWS = 512
PREP_HALO = 64


def _wprep_kernel(a_ref, nxt_ref, o_ref):
    j = pl.program_id(0)
    bounds = tuple(s // PREP_ROWS for s in PREP_STARTS) + (N_MAIN // PREP_ROWS,)
    for run, shift in enumerate(PREP_SHIFTS):
        @pl.when((j >= bounds[run]) & (j < bounds[run + 1]))
        def _(shift=shift):
            if shift == 0:
                o_ref[...] = a_ref[...].astype(BF16)
            else:
                x = jnp.concatenate([a_ref[...], nxt_ref[...]], axis=0)
                o_ref[...] = x[shift:shift + PREP_ROWS, :].astype(BF16)


def _prep_w_main(wt):
    d = wt.shape[1]
    assert all(s % PREP_ROWS == 0 for s in PREP_STARTS)
    assert max(PREP_SHIFTS) <= PREP_HALO and all(s % 16 == 0 for s in PREP_SHIFTS)
    return pl.pallas_call(
        _wprep_kernel,
        grid=(N_MAIN // PREP_ROWS,),
        in_specs=[
            pl.BlockSpec((PREP_ROWS, d), lambda j: (j, 0)),
            pl.BlockSpec((PREP_HALO, d), lambda j: ((j + 1) * (PREP_ROWS // PREP_HALO), 0)),
        ],
        out_specs=pl.BlockSpec((PREP_ROWS, d), lambda j: (j, 0)),
        out_shape=jax.ShapeDtypeStruct((N_MAIN, d), BF16),
        compiler_params=pltpu.CompilerParams(
            dimension_semantics=("arbitrary",), vmem_limit_bytes=VMEM_LIMIT),
        name="w_prep",
    )(wt, wt)


def _dot_nt(a, b):
    return lax.dot_general(a, b, (((1,), (1,)), ((), ())), preferred_element_type=F32)


def _in_proj_kernel(x_ref, g_ref, w_ref, ws_ref, o_ref, os_ref, h_ref):
    @pl.when(pl.program_id(1) == 0)
    def _():
        x = x_ref[...]
        inv = lax.rsqrt(jnp.mean(x * x, axis=-1, keepdims=True) + EPS)
        h = (x * inv * g_ref[...]).astype(BF16)
        h_ref[...] = h
        os_ref[...] = _dot_nt(h, ws_ref[...].astype(BF16))

    o_ref[...] = _dot_nt(h_ref[...], w_ref[...])


def _in_proj(x2d, ln_g, w_main, w_small):
    n_tok, d = x2d.shape
    tm = min(1024, n_tok)
    tn = 1024
    return pl.pallas_call(
        _in_proj_kernel,
        grid=(n_tok // tm, N_MAIN // tn),
        in_specs=[
            pl.BlockSpec((tm, d), lambda i, j: (i, 0)),
            pl.BlockSpec((1, d), lambda i, j: (0, 0)),
            pl.BlockSpec((tn, d), lambda i, j: (j, 0)),
            pl.BlockSpec((LANES, d), lambda i, j: (0, 0)),
        ],
        out_specs=[
            pl.BlockSpec((tm, tn), lambda i, j: (i, j)),
            pl.BlockSpec((tm, LANES), lambda i, j: (i, 0)),
        ],
        out_shape=[jax.ShapeDtypeStruct((n_tok, N_MAIN), F32),
                   jax.ShapeDtypeStruct((n_tok, LANES), F32)],
        scratch_shapes=[pltpu.VMEM((tm, d), BF16)],
        compiler_params=pltpu.CompilerParams(
            dimension_semantics=("arbitrary", "arbitrary"), vmem_limit_bytes=VMEM_LIMIT),
        name="in_proj",
    )(x2d, ln_g, w_main, w_small)


GLA_PROMPT_ROWS = 512
GLA_SAMPLE_ROWS = 32


def _gla_kernel(*refs, seq_len, rows, carry):
    n_in = 8 if carry else 9
    q_ref, k_ref, v_ref, gate_ref, sm_ref, wal_ref, bal_ref, ng_ref = refs[:8]
    s0_ref = None if carry else refs[8]
    o_ref, so_ref = refs[n_in:n_in + 2]
    qd_ref, ke_ref, ebl_ref, oacc_ref = refs[n_in + 2:n_in + 6]
    s_ref = refs[n_in + 6] if carry else None
    r, L = rows, seq_len
    nsub = r // L
    g = min(r, CHUNK)

    if carry:
        @pl.when(pl.program_id(1) == 0)
        def _():
            s_ref[...] = jnp.zeros_like(s_ref)

    ri = lax.broadcasted_iota(jnp.int32, (r, r), 0)
    ci = lax.broadcasted_iota(jnp.int32, (r, r), 1)
    incl = ((ri // L) == (ci // L)) & (ri >= ci)
    gi = lax.broadcasted_iota(jnp.int32, (g, 2 * LANES), 0)
    gl = lax.broadcasted_iota(jnp.int32, (g, 2 * LANES), 1)
    gc = gl % g
    tri3 = ((gl < 3 * g) & ((gi // L) == (gc // L)) & (gi >= gc)).astype(BF16)

    def seg_cumsum(x):
        outs = []
        for gq in range(r // g):
            xg = x[gq * g:(gq + 1) * g]
            hi, r1 = _split_f32(xg)
            mid, r2 = _split_f32(r1)
            parts = [hi, mid, r2.astype(BF16)]
            if 3 * g < 2 * LANES:
                parts.append(jnp.zeros((2 * LANES - 3 * g, xg.shape[1]), BF16))
            outs.append(_dot(tri3, jnp.concatenate(parts, axis=0)))
        return outs[0] if len(outs) == 1 else jnp.concatenate(outs, axis=0)

    for h in range(GLA_HEADS):
        lk = slice(h * GLA_DK, (h + 1) * GLA_DK)
        lv = slice(h * GLA_DV, (h + 1) * GLA_DV)
        q = q_ref[:, lk] * (GLA_DK ** -0.5)
        k = k_ref[:, lk]
        lr = sm_ref[:, SM_LR:SM_LR + GLA_RANK]
        z = _bdot(lr, wal_ref[:, lk]) + bal_ref[:, lk]
        la = -_softplus(-z) / GLA_TAU
        b = seg_cumsum(la)
        lasts = [jnp.broadcast_to(b[(qi + 1) * L - 1:(qi + 1) * L, :], (L, GLA_DK))
                 for qi in range(nsub)]
        b_last = lasts[0] if nsub == 1 else jnp.concatenate(lasts, axis=0)
        q_d = q * jnp.exp(b)
        k_d = k * jnp.exp(-b)
        att = jnp.where(incl, _bdot_nt(q_d, k_d), 0.0)
        oacc_ref[h] = _bdot(att, v_ref[:, lv])
        qd_ref[h] = q_d
        ke_ref[h] = k * jnp.exp(b_last - b)
        ebl_ref[h] = jnp.exp(b_last)

    for qi in range(nsub):
        rs = slice(qi * L, (qi + 1) * L)
        for h in range(GLA_HEADS):
            lv = slice(h * GLA_DV, (h + 1) * GLA_DV)
            s = s_ref[h] if carry else s0_ref[qi, h]
            oacc_ref[h, rs, :] = oacc_ref[h, rs, :] + _bdot(qd_ref[h, rs, :], s)
            kv = _bdot_tn(ke_ref[h, rs, :], v_ref[rs, lv])
            eb = ebl_ref[h, qi * L:qi * L + 1, :]
            col = jnp.transpose(jnp.broadcast_to(eb, (LANES, GLA_DK)))
            s_new = s * jnp.concatenate([col] * (GLA_DV // LANES), axis=1) + kv
            if carry:
                s_ref[h] = s_new
            else:
                so_ref[qi, h] = s_new

    for h in range(GLA_HEADS):
        lv = slice(h * GLA_DV, (h + 1) * GLA_DV)
        o = oacc_ref[h]
        inv = lax.rsqrt(jnp.mean(o * o, axis=-1, keepdims=True) + EPS)
        on = o * inv * ng_ref[...]
        o_ref[:, lv] = (on * _silu(gate_ref[:, lv])).astype(o_ref.dtype)

    if carry:
        @pl.when(pl.program_id(1) == pl.num_programs(1) - 1)
        def _():
            so_ref[...] = s_ref[...]


def _gla(proj, small, w_alpha2, b_alpha, norm_g, s0, *, bsz, t, rows):
    carry = s0 is None
    r = rows
    n_tok = proj.shape[0]
    if carry:
        seq_len = min(CHUNK, t)
        assert t % r == 0 and r % seq_len == 0
        grid = (bsz, t // r)
        row = lambda b, n: b * (t // r) + n
    else:
        assert t <= CHUNK and r % t == 0 and bsz % (r // t) == 0
        seq_len = t
        grid = (bsz // (r // t), 1)
        row = lambda b, n: b
    nsub = r // seq_len
    qb, kb = COL_QA // GLA_QK, COL_KA // GLA_QK
    vb, gb = COL_VA // GLA_V, COL_GATE_A // GLA_V
    in_specs = [
        pl.BlockSpec((r, GLA_QK), lambda b, n: (row(b, n), qb)),
        pl.BlockSpec((r, GLA_QK), lambda b, n: (row(b, n), kb)),
        pl.BlockSpec((r, GLA_V), lambda b, n: (row(b, n), vb)),
        pl.BlockSpec((r, GLA_V), lambda b, n: (row(b, n), gb)),
        pl.BlockSpec((r, LANES), lambda b, n: (row(b, n), 0)),
        pl.BlockSpec((GLA_RANK, GLA_QK), lambda b, n: (0, 0)),
        pl.BlockSpec((1, GLA_QK), lambda b, n: (0, 0)),
        pl.BlockSpec((1, GLA_DV), lambda b, n: (0, 0)),
    ]
    args = [proj, proj, proj, proj, small, w_alpha2, b_alpha, norm_g]
    scratch = [pltpu.VMEM((GLA_HEADS, r, GLA_DK), F32)] * 3 + [
        pltpu.VMEM((GLA_HEADS, r, GLA_DV), F32)]
    if carry:
        state = pl.BlockSpec((None, GLA_HEADS, GLA_DK, GLA_DV), lambda b, n: (b, 0, 0, 0))
        scratch += [pltpu.VMEM((GLA_HEADS, GLA_DK, GLA_DV), F32)]
    else:
        state = pl.BlockSpec((nsub, GLA_HEADS, GLA_DK, GLA_DV), lambda b, n: (b, 0, 0, 0))
        in_specs += [state]
        args += [s0]
    return pl.pallas_call(
        functools.partial(_gla_kernel, seq_len=seq_len, rows=r, carry=carry),
        grid=grid,
        in_specs=in_specs,
        out_specs=[pl.BlockSpec((r, GLA_V), lambda b, n: (row(b, n), 0)), state],
        out_shape=[jax.ShapeDtypeStruct((n_tok, GLA_V), BF16),
                   jax.ShapeDtypeStruct((bsz, GLA_HEADS, GLA_DK, GLA_DV), F32)],
        scratch_shapes=scratch,
        compiler_params=pltpu.CompilerParams(
            dimension_semantics=("arbitrary", "arbitrary"), vmem_limit_bytes=VMEM_LIMIT),
        name="gla",
    )(*args)


GDN_ROWS = 64
HALF = LANES // 2


def _split_f32(x):
    hi = x.astype(BF16)
    return hi, x - hi.astype(F32)


def _lhs3(x, low_half):
    hi, lo = _split_f32(x)
    t0 = jnp.where(low_half, x, lo).astype(BF16)
    t1 = jnp.where(low_half, hi, jnp.zeros_like(hi))
    return jnp.concatenate([t0, t1], axis=1)


def _rhs3(hi, lo):
    return jnp.concatenate([hi, hi, lo, lo], axis=0)


def _gdn_kernel(*refs, seq_len, carry):
    n_in = 11 if carry else 15
    (q_ref, k_ref, v_ref, gate_ref, sm_ref, cwq_ref, cwk_ref, cwv_ref, alog_ref, dtb_ref,
     ng_ref) = refs[:11]
    cq0_ref, ck0_ref, cv0_ref, s0_ref = (None,) * 4 if carry else refs[11:15]
    o_ref, so_ref = refs[n_in:n_in + 2]
    (eq_ref, ek_ref, ev_ref, cq_ref, ck_ref, cv_ref, nb_ref, tb_ref, nlhs_ref, p_ref, att_ref,
     rhs_ref, x0_ref, qg_ref, ke_ref, u_ref, w_ref, os_ref) = refs[n_in + 2:n_in + 20]
    if carry:
        (s_ref,) = refs[n_in + 20:]
        hq_ref = hk_ref = hv_ref = None
    else:
        hq_ref, hk_ref, hv_ref = refs[n_in + 20:]
        s_ref = None
    r = GDN_ROWS
    L = seq_len
    nseq = r // L
    tail = CONV_W - 1
    pad = 8

    if carry:
        @pl.when(pl.program_id(1) == 0)
        def _():
            s_ref[...] = jnp.zeros_like(s_ref)
            for e_ref in (eq_ref, ek_ref, ev_ref):
                e_ref[0:pad, :] = jnp.zeros((pad, GDN_QK), F32)
    else:
        @pl.when(pl.program_id(0) == 0)
        def _():
            for e_ref in (eq_ref, ek_ref, ev_ref, hq_ref, hk_ref, hv_ref):
                e_ref[...] = jnp.zeros_like(e_ref)

    row_t = lax.broadcasted_iota(jnp.int32, (r, 1), 0) % L

    def conv(e_ref, x_ref, w_ref, out_ref, h_ref, c0_ref):
        e_ref[pad:pad + r, :] = x_ref[...]
        if not carry:
            for qi in range(nseq):
                h_ref[qi * L + pad - tail:qi * L + pad, :] = c0_ref[qi]
        acc = None
        for i in range(CONV_W):
            sh = tail - i
            xs = e_ref[pad - sh:pad - sh + r, :]
            if not carry and sh > 0:
                xs = jnp.where(row_t >= sh, xs, h_ref[pad - sh:pad - sh + r, :])
            term = xs * w_ref[i:i + 1, :]
            acc = term if acc is None else acc + term
        if carry:
            e_ref[pad - tail:pad, :] = x_ref[r - tail:r, :]
        out_ref[...] = _silu(acc)

    conv(eq_ref, q_ref, cwq_ref, cq_ref, hq_ref, cq0_ref)
    conv(ek_ref, k_ref, cwk_ref, ck_ref, hk_ref, ck0_ref)
    conv(ev_ref, v_ref, cwv_ref, cv_ref, hv_ref, cv0_ref)

    ri = lax.broadcasted_iota(jnp.int32, (r, LANES), 0)
    li = lax.broadcasted_iota(jnp.int32, (r, LANES), 1)
    cj = li % HALF
    low_half = li < HALF
    same = (ri // L) == (cj // L)
    incl = same & (ri >= cj)
    strict = same & (ri > cj)
    eye = (ri == cj).astype(F32)

    sm = sm_ref[...]
    beta_all = _sigmoid(sm)
    g_all = -jnp.exp(alog_ref[...]) * _softplus(sm + dtb_ref[...])
    g_hi, g_r1 = _split_f32(g_all)
    g_mid, g_r2 = _split_f32(g_r1)
    g_lo = g_r2.astype(BF16)
    tri = incl.astype(BF16)
    tri_lhs = jnp.concatenate([tri, jnp.where(low_half, tri, jnp.zeros_like(tri))], axis=1)
    gh_all = _dot(tri_lhs, jnp.concatenate([g_hi, g_mid, g_lo, g_lo], axis=0))
    gh_t = jnp.transpose(jnp.concatenate([gh_all, gh_all], axis=0))
    glast_all = jnp.concatenate(
        [jnp.broadcast_to(gh_all[(qi + 1) * L - 1:(qi + 1) * L, :], (L, LANES))
         for qi in range(nseq)], axis=0)
    egh_all = jnp.exp(gh_all)
    kend_all = jnp.exp(glast_all - gh_all)

    for h in range(GDN_HEADS):
        ls = slice(h * GDN_DK, (h + 1) * GDN_DK)
        q = cq_ref[:, ls]
        k = ck_ref[:, ls]
        v = cv_ref[:, ls]
        q = q * lax.rsqrt(jnp.sum(q * q, axis=-1, keepdims=True) + EPS) * (GDN_DK ** -0.5)
        k = k * lax.rsqrt(jnp.sum(k * k, axis=-1, keepdims=True) + EPS)
        col = slice(SM_DEC + h, SM_DEC + h + 1)
        beta = beta_all[:, SM_BETA + h:SM_BETA + h + 1]
        gh = gh_all[:, col]
        egh = egh_all[:, col]
        decay = jnp.where(incl, jnp.exp(gh - gh_t[col, :]), 0.0)
        kb = k.astype(BF16)
        qkk = lax.dot_general(jnp.concatenate([q.astype(BF16), kb], axis=0),
                              jnp.concatenate([kb, kb], axis=0),
                              (((1,), (1,)), ((), ())), preferred_element_type=F32)
        att_ref[h] = (qkk[:r] * decay).astype(BF16)
        nm = jnp.where(strict, -(beta * qkk[r:] * decay), 0.0)
        nlhs_ref[h] = _lhs3(nm, low_half)
        nb_ref[h] = nm[:, :HALF].astype(BF16)
        p_ref[h] = (eye + nm)[:, :HALF]
        rhs_ref[h] = jnp.concatenate([v * beta, k * (beta * egh)], axis=1)
        qg_ref[h] = q * egh
        ke_ref[h] = k * kend_all[:, col]

    p = 2
    while p < L:
        for h in range(GDN_HEADS):
            nb = nb_ref[h]
            nb_ref[h] = _dot(nb, nb).astype(BF16)
        for h in range(GDN_HEADS):
            pm = p_ref[h]
            p_ref[h] = pm + _dot(pm.astype(BF16), nb_ref[h])
        p *= 2

    for h in range(GDN_HEADS):
        tb = p_ref[h].astype(BF16)
        tb_ref[h] = tb
        x0_ref[h] = _dot(tb, rhs_ref[h].astype(BF16))
    for h in range(GDN_HEADS):
        x0 = x0_ref[h]
        x_hi, x_lo = _split_f32(x0)
        rhs_ref[h] = rhs_ref[h] - x0 + _dot(nlhs_ref[h], _rhs3(x_hi, x_lo.astype(BF16)))
    for h in range(GDN_HEADS):
        sol = x0_ref[h] + _dot(tb_ref[h], rhs_ref[h].astype(BF16))
        u_ref[h] = sol[:, :GDN_DV]
        w_ref[h] = sol[:, GDN_DV:]

    def state(qi, h):
        return s_ref[h] if carry else s0_ref[qi, h]

    for h in range(GDN_HEADS):
        for qi in range(nseq):
            rs = slice(qi * L, (qi + 1) * L)
            lhs = jnp.concatenate([w_ref[h, rs, :], qg_ref[h, rs, :]], axis=0).astype(BF16)
            res = _dot(lhs, state(qi, h).astype(BF16))
            u_ref[h, rs, :] = u_ref[h, rs, :] - res[:L]
            os_ref[h, rs, :] = res[L:]

    for h in range(GDN_HEADS):
        ls = slice(h * GDN_DV, (h + 1) * GDN_DV)
        v_new = u_ref[h]
        o = os_ref[h] + _dot(att_ref[h][:, :HALF], v_new.astype(BF16))
        inv = lax.rsqrt(jnp.mean(o * o, axis=-1, keepdims=True) + EPS)
        on = o * inv * ng_ref[...]
        o_ref[:, ls] = (on * _silu(gate_ref[:, ls])).astype(o_ref.dtype)
        for qi in range(nseq):
            rs = slice(qi * L, (qi + 1) * L)
            kv = lax.dot_general(ke_ref[h, rs, :].astype(BF16), v_new[rs].astype(BF16),
                                 (((0,), (0,)), ((), ())), preferred_element_type=F32)
            eg = egh_all[(qi + 1) * L - 1:(qi + 1) * L, SM_DEC + h:SM_DEC + h + 1]
            s_new = state(qi, h) * eg + kv
            if carry:
                s_ref[h] = s_new
            else:
                so_ref[qi, h] = s_new

    if carry:
        @pl.when(pl.program_id(1) == pl.num_programs(1) - 1)
        def _():
            so_ref[...] = s_ref[...]


def _gdn(proj, small, conv_w, alog_pad, dtb_pad, norm_g, conv0, s0, *, bsz, t):
    carry = s0 is None
    r = GDN_ROWS
    n_tok = proj.shape[0]
    tail = CONV_W - 1
    if carry:
        assert t % r == 0
        seq_len, nseq, grid = r, 1, (bsz, t // r)
        row = lambda b, n: b * (t // r) + n
    else:
        assert r % t == 0 and bsz % (r // t) == 0
        seq_len, nseq, grid = t, r // t, (bsz // (r // t), 1)
        row = lambda b, n: b
    qb = COL_QKV_B // GDN_QK
    gb = COL_GATE_B // GDN_V
    rowspec = lambda cb: pl.BlockSpec((r, GDN_QK), lambda b, n: (row(b, n), cb))
    cwspec = lambda cb: pl.BlockSpec((CONV_W, GDN_QK), lambda b, n: (0, cb))
    vec = pl.BlockSpec((1, LANES), lambda b, n: (0, 0))
    in_specs = [rowspec(qb), rowspec(qb + 1), rowspec(qb + 2), rowspec(gb),
                pl.BlockSpec((r, LANES), lambda b, n: (row(b, n), 0)),
                cwspec(0), cwspec(1), cwspec(2), vec, vec, vec]
    args = [proj, proj, proj, proj, small, conv_w, conv_w, conv_w, alog_pad, dtb_pad, norm_g]
    hshape = (GDN_HEADS, r, LANES)
    sq = (GDN_HEADS, r, r)
    wide = (GDN_HEADS, r, 2 * LANES)
    scratch = ([pltpu.VMEM((8 + r, GDN_QK), F32)] * 3 + [pltpu.VMEM((r, GDN_QK), F32)] * 3
               + [pltpu.VMEM(sq, BF16)] * 2 + [pltpu.VMEM(wide, BF16)] + [pltpu.VMEM(sq, F32)]
               + [pltpu.VMEM(hshape, BF16)]
               + [pltpu.VMEM(wide, F32)] * 2
               + [pltpu.VMEM(hshape, F32)] * 5)
    if carry:
        state = pl.BlockSpec((None, GDN_HEADS, GDN_DK, GDN_DV), lambda b, n: (b, 0, 0, 0))
        scratch += [pltpu.VMEM((GDN_HEADS, GDN_DK, GDN_DV), F32)]
    else:
        state = pl.BlockSpec((nseq, GDN_HEADS, GDN_DK, GDN_DV), lambda b, n: (b, 0, 0, 0))
        c0spec = lambda cb: pl.BlockSpec((nseq, tail, GDN_QK), lambda b, n: (b, 0, cb))
        in_specs += [c0spec(0), c0spec(1), c0spec(2), state]
        args += [conv0, conv0, conv0, s0]
        scratch += [pltpu.VMEM((8 + r, GDN_QK), F32)] * 3
    return pl.pallas_call(
        functools.partial(_gdn_kernel, seq_len=seq_len, carry=carry),
        grid=grid,
        in_specs=in_specs,
        out_specs=[pl.BlockSpec((r, GDN_V), lambda b, n: (row(b, n), 0)), state],
        out_shape=[jax.ShapeDtypeStruct((n_tok, GDN_V), BF16),
                   jax.ShapeDtypeStruct((bsz, GDN_HEADS, GDN_DK, GDN_DV), F32)],
        scratch_shapes=scratch,
        compiler_params=pltpu.CompilerParams(
            dimension_semantics=("arbitrary", "arbitrary"), vmem_limit_bytes=VMEM_LIMIT),
        name="gdn",
    )(*args)


def _merge_kernel(oa_ref, ob_ref, wa_ref, wb_ref, ma_ref, mb_ref, o_ref):
    ya = jnp.dot(oa_ref[...].astype(BF16), wa_ref[...], preferred_element_type=F32)
    yb = jnp.dot(ob_ref[...].astype(BF16), wb_ref[...], preferred_element_type=F32)
    merged = _sigmoid(ma_ref[...]) * ya + _sigmoid(mb_ref[...]) * yb
    o_ref[...] = merged.astype(o_ref.dtype)


def _merge(o_a, o_b, w_br_a, w_br_b, proj):
    n_tok = o_a.shape[0]
    tm = min(1024, n_tok)
    tn = 512
    ma, mb = COL_MA // tn, COL_MB // tn
    return pl.pallas_call(
        _merge_kernel,
        grid=(n_tok // tm, D_MODEL // tn),
        in_specs=[
            pl.BlockSpec((tm, GLA_V), lambda i, j: (i, 0)),
            pl.BlockSpec((tm, GDN_V), lambda i, j: (i, 0)),
            pl.BlockSpec((GLA_V, tn), lambda i, j: (0, j)),
            pl.BlockSpec((GDN_V, tn), lambda i, j: (0, j)),
            pl.BlockSpec((tm, tn), lambda i, j: (i, ma + j)),
            pl.BlockSpec((tm, tn), lambda i, j: (i, mb + j)),
        ],
        out_specs=pl.BlockSpec((tm, tn), lambda i, j: (i, j)),
        out_shape=jax.ShapeDtypeStruct((n_tok, D_MODEL), BF16),
        compiler_params=pltpu.CompilerParams(
            dimension_semantics=("arbitrary", "arbitrary"), vmem_limit_bytes=VMEM_LIMIT),
        name="merge",
    )(o_a, o_b, w_br_a, w_br_b, proj, proj)


def _out_kernel(m_ref, w_ref, x_ref, g_ref, y_ref):
    out = x_ref[...] + jnp.dot(m_ref[...], w_ref[...], preferred_element_type=F32)
    inv = lax.rsqrt(jnp.mean(out * out, axis=-1, keepdims=True) + EPS)
    y_ref[...] = out * inv * g_ref[...]


def _out_proj(merged, w_out, x2d, final_g):
    n_tok = merged.shape[0]
    tm = min(512, n_tok)
    return pl.pallas_call(
        _out_kernel,
        grid=(n_tok // tm,),
        in_specs=[
            pl.BlockSpec((tm, D_MODEL), lambda i: (i, 0)),
            pl.BlockSpec((D_MODEL, D_MODEL), lambda i: (0, 0)),
            pl.BlockSpec((tm, D_MODEL), lambda i: (i, 0)),
            pl.BlockSpec((1, D_MODEL), lambda i: (0, 0)),
        ],
        out_specs=pl.BlockSpec((tm, D_MODEL), lambda i: (i, 0)),
        out_shape=jax.ShapeDtypeStruct((n_tok, D_MODEL), F32),
        compiler_params=pltpu.CompilerParams(
            dimension_semantics=("arbitrary",), vmem_limit_bytes=VMEM_LIMIT),
        name="out_proj",
    )(merged, w_out, x2d, final_g)


def _layer(x, s_gla, s_gdn, conv_buf, p, *, gla_rows):
    bsz, t, d = x.shape
    x2d = x.reshape(bsz * t, d)
    proj, small = _in_proj(x2d, p["ln_g"], p["w_main"], p["w_small"])
    o_a, s_gla_new = _gla(proj, small, p["w_alpha2"], p["b_alpha"], p["gla_norm_g"], s_gla,
                          bsz=bsz, t=t, rows=gla_rows)
    o_b, s_gdn_new = _gdn(proj, small, p["conv_w"], p["alog_pad"], p["dtb_pad"],
                          p["gdn_norm_g"], conv_buf, s_gdn, bsz=bsz, t=t)
    merged = _merge(o_a, o_b, p["w_br_a"], p["w_br_b"], proj)
    y = _out_proj(merged, p["w_out"], x2d, p["final_g"])
    tail = CONV_W - 1
    qkv = proj.reshape(bsz, t, N_MAIN)[:, t - tail:, COL_QKV_B:COL_QKV_B + CONV_CH]
    return y.reshape(bsz, t, d), s_gla_new, s_gdn_new, qkv


def _prep_params(ln_in_g, w_in, w_alpha2, b_alpha, conv_w, a_log, dt_bias, gla_norm_g,
                 gdn_norm_g, w_br_a, w_br_b, w_out, final_norm_g):
    wt = jnp.transpose(w_in)
    w_main = _prep_w_main(wt)
    n_small = GLA_RANK + 2 * GDN_HEADS
    w_small = jnp.concatenate(
        [wt[W_IN_OFFS[3]:W_IN_OFFS[4]], wt[W_IN_OFFS[6]:W_IN_OFFS[8]],
         jnp.zeros((LANES - n_small, D_MODEL), F32)], axis=0)
    lane_pad = lambda v: jnp.zeros((1, LANES), F32).at[0, SM_DEC:SM_DEC + GDN_HEADS].set(v)
    return {
        "ln_g": ln_in_g.reshape(1, D_MODEL),
        "w_main": w_main,
        "w_small": w_small,
        "w_alpha2": w_alpha2,
        "b_alpha": b_alpha.reshape(1, GLA_QK),
        "conv_w": conv_w,
        "alog_pad": lane_pad(a_log),
        "dtb_pad": lane_pad(dt_bias),
        "gla_norm_g": gla_norm_g.reshape(1, GLA_DV),
        "gdn_norm_g": gdn_norm_g.reshape(1, GDN_DV),
        "w_br_a": w_br_a.astype(BF16),
        "w_br_b": w_br_b.astype(BF16),
        "w_out": w_out.astype(BF16),
        "final_g": final_norm_g.reshape(1, D_MODEL),
    }


def kernel(x_prompt, x_sample, state_gla, state_gdn, state_conv, ln_in_g, w_in, w_alpha2,
           b_alpha, conv_w, a_log, dt_bias, gla_norm_g, gdn_norm_g, w_br_a, w_br_b, w_out,
           final_norm_g):
    assert ln_in_g.shape[0] == 1, "single layer"
    p = _prep_params(ln_in_g[0], w_in[0], w_alpha2[0], b_alpha[0], conv_w[0], a_log[0],
                     dt_bias[0], gla_norm_g[0], gdn_norm_g[0], w_br_a[0], w_br_b[0], w_out[0],
                     final_norm_g)
    yp, gla_p, gdn_p, conv_p = _layer(x_prompt, None, None, None, p, gla_rows=GLA_PROMPT_ROWS)
    ys, gla_s, gdn_s, conv_s = _layer(x_sample, state_gla[0], state_gdn[0], state_conv[0], p,
                                      gla_rows=GLA_SAMPLE_ROWS)
    return (yp, ys, gla_p[None], gdn_p[None], conv_p[None], gla_s[None], gdn_s[None],
            conv_s[None])
```

```python
import functools

import jax
import jax.numpy as jnp
from jax import lax
from jax.experimental import pallas as pl
from jax.experimental.pallas import tpu as pltpu

F32 = jnp.float32
BF16 = jnp.bfloat16

D_MODEL = 2048
GLA_HEADS = 4
GLA_DK = 256
GLA_DV = 512
GLA_RANK = 16
GLA_TAU = 16.0
GDN_HEADS = 16
GDN_DK = 128
GDN_DV = 128
CONV_W = 4
CHUNK = 64
EPS = 1e-6

GLA_QK = GLA_HEADS * GLA_DK
GLA_V = GLA_HEADS * GLA_DV
GDN_QK = GDN_HEADS * GDN_DK
GDN_V = GDN_HEADS * GDN_DV
CONV_CH = 2 * GDN_QK + GDN_V

LANES = 128
COL_QA = 0
COL_KA = COL_QA + GLA_QK
COL_VA = COL_KA + GLA_QK
COL_GATE_A = COL_VA + GLA_V
COL_QKV_B = COL_GATE_A + GLA_V
COL_GATE_B = COL_QKV_B + CONV_CH
COL_MA = COL_GATE_B + GDN_V
COL_MB = COL_MA + D_MODEL
N_MAIN = COL_MB + D_MODEL
SM_LR = 0
SM_BETA = GLA_RANK
SM_DEC = SM_BETA + GDN_HEADS

VMEM_LIMIT = 56 * 1024 * 1024


def _bdot(a, b):
    return jnp.dot(a.astype(BF16), b.astype(BF16), preferred_element_type=F32)


def _bdot_nt(a, b):
    return lax.dot_general(a.astype(BF16), b.astype(BF16), (((1,), (1,)), ((), ())),
                           preferred_element_type=F32)


def _bdot_tn(a, b):
    return lax.dot_general(a.astype(BF16), b.astype(BF16), (((0,), (0,)), ((), ())),
                           preferred_element_type=F32)


def _dot(a, b):
    return jnp.dot(a, b, preferred_element_type=F32)


def _sigmoid(x):
    return 1.0 / (1.0 + jnp.exp(-x))


def _silu(x):
    return x * _sigmoid(x)


def _softplus(x):
    return jnp.maximum(x, 0.0) + jnp.log(1.0 + jnp.exp(-jnp.abs(x)))


W_IN_SIZES = (GLA_QK, GLA_QK, GLA_V, GLA_RANK, GLA_V, CONV_CH, GDN_HEADS, GDN_HEADS, GDN_V,
              D_MODEL, D_MODEL)
W_IN_OFFS = tuple(sum(W_IN_SIZES[:i]) for i in range(len(W_IN_SIZES) + 1))
PREP_STARTS = (0, COL_GATE_A, COL_GATE_B)
PREP_SHIFTS = (0, W_IN_OFFS[4] - COL_GATE_A, W_IN_OFFS[8] - COL_GATE_B)
PREP_ROWS = 512
PREP_HALO = 64


def _wprep_kernel(a_ref, nxt_ref, o_ref):
    j = pl.program_id(0)
    bounds = tuple(s // PREP_ROWS for s in PREP_STARTS) + (N_MAIN // PREP_ROWS,)
    for run, shift in enumerate(PREP_SHIFTS):
        @pl.when((j >= bounds[run]) & (j < bounds[run + 1]))
        def _(shift=shift):
            if shift == 0:
                o_ref[...] = a_ref[...].astype(BF16)
            else:
                x = jnp.concatenate([a_ref[...], nxt_ref[...]], axis=0)
                o_ref[...] = x[shift:shift + PREP_ROWS, :].astype(BF16)


def _prep_w_main(wt):
    d = wt.shape[1]
    assert all(s % PREP_ROWS == 0 for s in PREP_STARTS)
    assert max(PREP_SHIFTS) <= PREP_HALO and all(s % 16 == 0 for s in PREP_SHIFTS)
    return pl.pallas_call(
        _wprep_kernel,
        grid=(N_MAIN // PREP_ROWS,),
        in_specs=[
            pl.BlockSpec((PREP_ROWS, d), lambda j: (j, 0)),
            pl.BlockSpec((PREP_HALO, d), lambda j: ((j + 1) * (PREP_ROWS // PREP_HALO), 0)),
        ],
        out_specs=pl.BlockSpec((PREP_ROWS, d), lambda j: (j, 0)),
        out_shape=jax.ShapeDtypeStruct((N_MAIN, d), BF16),
        compiler_params=pltpu.CompilerParams(
            dimension_semantics=("arbitrary",), vmem_limit_bytes=VMEM_LIMIT),
        name="w_prep",
    )(wt, wt)


def _dot_nt(a, b):
    return lax.dot_general(a, b, (((1,), (1,)), ((), ())), preferred_element_type=F32)


IN_PROJ_TM = 1024


def _in_proj_kernel(x_ref, g_ref, w_ref, ws_ref, o_ref, os_ref, h_ref):
    @pl.when(pl.program_id(1) == 0)
    def _():
        x = x_ref[...]
        inv = lax.rsqrt(jnp.mean(x * x, axis=-1, keepdims=True) + EPS)
        h = (x * inv * g_ref[...]).astype(BF16)
        h_ref[...] = h
        os_ref[...] = _dot_nt(h, ws_ref[...].astype(BF16))

    o_ref[...] = _dot_nt(h_ref[...], w_ref[...])


def _in_proj(x2d, ln_g, w_main, w_small):
    n_tok, d = x2d.shape
    tm = min(IN_PROJ_TM, n_tok)
    tn = 1024
    return pl.pallas_call(
        _in_proj_kernel,
        grid=(n_tok // tm, N_MAIN // tn),
        in_specs=[
            pl.BlockSpec((tm, d), lambda i, j: (i, 0)),
            pl.BlockSpec((1, d), lambda i, j: (0, 0)),
            pl.BlockSpec((tn, d), lambda i, j: (j, 0)),
            pl.BlockSpec((LANES, d), lambda i, j: (0, 0)),
        ],
        out_specs=[
            pl.BlockSpec((tm, tn), lambda i, j: (i, j)),
            pl.BlockSpec((tm, LANES), lambda i, j: (i, 0)),
        ],
        out_shape=[jax.ShapeDtypeStruct((n_tok, N_MAIN), F32),
                   jax.ShapeDtypeStruct((n_tok, LANES), F32)],
        scratch_shapes=[pltpu.VMEM((tm, d), BF16)],
        compiler_params=pltpu.CompilerParams(
            dimension_semantics=("arbitrary", "arbitrary"), vmem_limit_bytes=VMEM_LIMIT),
        name="in_proj",
    )(x2d, ln_g, w_main, w_small)


GLA_PROMPT_ROWS = 512
GLA_SAMPLE_ROWS = 32


def _gla_kernel(*refs, seq_len, rows, carry):
    n_in = 8 if carry else 9
    q_ref, k_ref, v_ref, gate_ref, sm_ref, wal_ref, bal_ref, ng_ref = refs[:8]
    s0_ref = None if carry else refs[8]
    o_ref, so_ref = refs[n_in:n_in + 2]
    qd_ref, ke_ref, ebl_ref, oacc_ref = refs[n_in + 2:n_in + 6]
    s_ref = refs[n_in + 6] if carry else None
    r, L = rows, seq_len
    nsub = r // L
    g = min(r, CHUNK)

    if carry:
        @pl.when(pl.program_id(1) == 0)
        def _():
            s_ref[...] = jnp.zeros_like(s_ref)

    ri = lax.broadcasted_iota(jnp.int32, (r, r), 0)
    ci = lax.broadcasted_iota(jnp.int32, (r, r), 1)
    incl = ((ri // L) == (ci // L)) & (ri >= ci)
    gi = lax.broadcasted_iota(jnp.int32, (g, 2 * LANES), 0)
    gl = lax.broadcasted_iota(jnp.int32, (g, 2 * LANES), 1)
    gc = gl % g
    tri3 = ((gl < 3 * g) & ((gi // L) == (gc // L)) & (gi >= gc)).astype(BF16)

    def seg_cumsum(x):
        outs = []
        for gq in range(r // g):
            xg = x[gq * g:(gq + 1) * g]
            hi, r1 = _split_f32(xg)
            mid, r2 = _split_f32(r1)
            parts = [hi, mid, r2.astype(BF16)]
            if 3 * g < 2 * LANES:
                parts.append(jnp.zeros((2 * LANES - 3 * g, xg.shape[1]), BF16))
            outs.append(_dot(tri3, jnp.concatenate(parts, axis=0)))
        return outs[0] if len(outs) == 1 else jnp.concatenate(outs, axis=0)

    for h in range(GLA_HEADS):
        lk = slice(h * GLA_DK, (h + 1) * GLA_DK)
        lv = slice(h * GLA_DV, (h + 1) * GLA_DV)
        q = q_ref[:, lk] * (GLA_DK ** -0.5)
        k = k_ref[:, lk]
        lr = sm_ref[:, SM_LR:SM_LR + GLA_RANK]
        z = _bdot(lr, wal_ref[:, lk]) + bal_ref[:, lk]
        la = -_softplus(-z) / GLA_TAU
        b = seg_cumsum(la)
        lasts = [jnp.broadcast_to(b[(qi + 1) * L - 1:(qi + 1) * L, :], (L, GLA_DK))
                 for qi in range(nsub)]
        b_last = lasts[0] if nsub == 1 else jnp.concatenate(lasts, axis=0)
        q_d = q * jnp.exp(b)
        k_d = k * jnp.exp(-b)
        att = jnp.where(incl, _bdot_nt(q_d, k_d), 0.0)
        oacc_ref[h] = _bdot(att, v_ref[:, lv])
        qd_ref[h] = q_d
        ke_ref[h] = k * jnp.exp(b_last - b)
        ebl_ref[h] = jnp.exp(b_last)

    for qi in range(nsub):
        rs = slice(qi * L, (qi + 1) * L)
        for h in range(GLA_HEADS):
            lv = slice(h * GLA_DV, (h + 1) * GLA_DV)
            s = s_ref[h] if carry else s0_ref[qi, h]
            oacc_ref[h, rs, :] = oacc_ref[h, rs, :] + _bdot(qd_ref[h, rs, :], s)
            kv = _bdot_tn(ke_ref[h, rs, :], v_ref[rs, lv])
            eb = ebl_ref[h, qi * L:qi * L + 1, :]
            col = jnp.transpose(jnp.broadcast_to(eb, (LANES, GLA_DK)))
            s_new = s * jnp.concatenate([col] * (GLA_DV // LANES), axis=1) + kv
            if carry:
                s_ref[h] = s_new
            else:
                so_ref[qi, h] = s_new

    for h in range(GLA_HEADS):
        lv = slice(h * GLA_DV, (h + 1) * GLA_DV)
        o = oacc_ref[h]
        inv = lax.rsqrt(jnp.mean(o * o, axis=-1, keepdims=True) + EPS)
        on = o * inv * ng_ref[...]
        o_ref[:, lv] = (on * _silu(gate_ref[:, lv])).astype(o_ref.dtype)

    if carry:
        @pl.when(pl.program_id(1) == pl.num_programs(1) - 1)
        def _():
            so_ref[...] = s_ref[...]


def _gla(proj, small, w_alpha2, b_alpha, norm_g, s0, *, bsz, t, rows):
    carry = s0 is None
    r = rows
    n_tok = proj.shape[0]
    if carry:
        seq_len = min(CHUNK, t)
        assert t % r == 0 and r % seq_len == 0
        grid = (bsz, t // r)
        row = lambda b, n: b * (t // r) + n
    else:
        assert t <= CHUNK and r % t == 0 and bsz % (r // t) == 0
        seq_len = t
        grid = (bsz // (r // t), 1)
        row = lambda b, n: b
    nsub = r // seq_len
    qb, kb = COL_QA // GLA_QK, COL_KA // GLA_QK
    vb, gb = COL_VA // GLA_V, COL_GATE_A // GLA_V
    in_specs = [
        pl.BlockSpec((r, GLA_QK), lambda b, n: (row(b, n), qb)),
        pl.BlockSpec((r, GLA_QK), lambda b, n: (row(b, n), kb)),
        pl.BlockSpec((r, GLA_V), lambda b, n: (row(b, n), vb)),
        pl.BlockSpec((r, GLA_V), lambda b, n: (row(b, n), gb)),
        pl.BlockSpec((r, LANES), lambda b, n: (row(b, n), 0)),
        pl.BlockSpec((GLA_RANK, GLA_QK), lambda b, n: (0, 0)),
        pl.BlockSpec((1, GLA_QK), lambda b, n: (0, 0)),
        pl.BlockSpec((1, GLA_DV), lambda b, n: (0, 0)),
    ]
    args = [proj, proj, proj, proj, small, w_alpha2, b_alpha, norm_g]
    scratch = [pltpu.VMEM((GLA_HEADS, r, GLA_DK), F32)] * 3 + [
        pltpu.VMEM((GLA_HEADS, r, GLA_DV), F32)]
    if carry:
        state = pl.BlockSpec((None, GLA_HEADS, GLA_DK, GLA_DV), lambda b, n: (b, 0, 0, 0))
        scratch += [pltpu.VMEM((GLA_HEADS, GLA_DK, GLA_DV), F32)]
    else:
        state = pl.BlockSpec((nsub, GLA_HEADS, GLA_DK, GLA_DV), lambda b, n: (b, 0, 0, 0))
        in_specs += [state]
        args += [s0]
    return pl.pallas_call(
        functools.partial(_gla_kernel, seq_len=seq_len, rows=r, carry=carry),
        grid=grid,
        in_specs=in_specs,
        out_specs=[pl.BlockSpec((r, GLA_V), lambda b, n: (row(b, n), 0)), state],
        out_shape=[jax.ShapeDtypeStruct((n_tok, GLA_V), BF16),
                   jax.ShapeDtypeStruct((bsz, GLA_HEADS, GLA_DK, GLA_DV), F32)],
        scratch_shapes=scratch,
        compiler_params=pltpu.CompilerParams(
            dimension_semantics=("arbitrary", "arbitrary"), vmem_limit_bytes=VMEM_LIMIT),
        name="gla",
    )(*args)


GDN_ROWS = 64
HALF = LANES // 2
GDN_GROUP_CARRY = GDN_HEADS
GDN_GROUP_SEQS = 1


def _split_f32(x):
    hi = x.astype(BF16)
    return hi, x - hi.astype(F32)


def _lhs3(x, low_half):
    hi, lo = _split_f32(x)
    t0 = jnp.where(low_half, x, lo).astype(BF16)
    t1 = jnp.where(low_half, hi, jnp.zeros_like(hi))
    return jnp.concatenate([t0, t1], axis=1)


def _rhs3(hi, lo):
    return jnp.concatenate([hi, hi, lo, lo], axis=0)


def _gdn_kernel(*refs, seq_len, carry, group):
    r = GDN_ROWS
    L = seq_len
    nseq = r // L
    tail = CONV_W - 1
    pad = 8
    n_in = 11 if carry else 15
    (q_ref, k_ref, v_ref, gate_ref, sm_ref, alog_ref, dtb_ref, ng_ref, cwq_ref, cwk_ref,
     cwv_ref) = refs[:11]
    cq0_ref, ck0_ref, cv0_ref, s0_ref = (None,) * 4 if carry else refs[11:15]
    o_ref, so_ref = refs[n_in:n_in + 2]
    (nb_ref, tb_ref, nlhs_ref, p_ref, att_ref, rhs_ref, x0_ref, qg_ref, ke_ref, u_ref, w_ref,
     os_ref, eq_ref, ek_ref, ev_ref, cq_ref, ck_ref, cv_ref) = refs[n_in + 2:n_in + 20]
    ri = lax.broadcasted_iota(jnp.int32, (r, LANES), 0)
    li = lax.broadcasted_iota(jnp.int32, (r, LANES), 1)
    cj = li % HALF
    low_half = li < HALF
    if carry:
        (s_ref,) = refs[n_in + 20:]
        hq_ref = hk_ref = hv_ref = None

        @pl.when(pl.program_id(1) == 0)
        def _():
            s_ref[...] = jnp.zeros_like(s_ref)
            for e_ref in (eq_ref, ek_ref, ev_ref):
                e_ref[0:pad, :] = jnp.zeros((pad, GDN_QK), F32)
    else:
        s_ref = None
        hq_ref, hk_ref, hv_ref = refs[n_in + 20:]

        @pl.when(pl.program_id(0) == 0)
        def _():
            for e_ref in (eq_ref, ek_ref, ev_ref, hq_ref, hk_ref, hv_ref):
                e_ref[...] = jnp.zeros_like(e_ref)

    row_t = lax.broadcasted_iota(jnp.int32, (r, 1), 0) % L

    def conv(e_ref, x_ref, w_ref, out_ref, h_ref, c0_ref):
        e_ref[pad:pad + r, :] = x_ref[...]
        if not carry:
            for qi in range(nseq):
                h_ref[qi * L + pad - tail:qi * L + pad, :] = c0_ref[qi]
        acc = None
        for i in range(CONV_W):
            sh = tail - i
            xs = e_ref[pad - sh:pad - sh + r, :]
            if not carry and sh > 0:
                xs = jnp.where(row_t >= sh, xs, h_ref[pad - sh:pad - sh + r, :])
            term = xs * w_ref[i:i + 1, :]
            acc = term if acc is None else acc + term
        if carry:
            e_ref[pad - tail:pad, :] = x_ref[r - tail:r, :]
        out_ref[...] = _silu(acc)

    conv(eq_ref, q_ref, cwq_ref, cq_ref, hq_ref, cq0_ref)
    conv(ek_ref, k_ref, cwk_ref, ck_ref, hk_ref, ck0_ref)
    conv(ev_ref, v_ref, cwv_ref, cv_ref, hv_ref, cv0_ref)
    sm = sm_ref[...]
    same = (ri // L) == (cj // L)
    incl = same & (ri >= cj)
    strict = same & (ri > cj)
    eye = (ri == cj).astype(F32)

    beta_all = _sigmoid(sm)
    g_all = -jnp.exp(alog_ref[...]) * _softplus(sm + dtb_ref[...])
    g_hi, g_r1 = _split_f32(g_all)
    g_mid, g_r2 = _split_f32(g_r1)
    g_lo = g_r2.astype(BF16)
    tri = incl.astype(BF16)
    tri_lhs = jnp.concatenate([tri, jnp.where(low_half, tri, jnp.zeros_like(tri))], axis=1)
    gh_all = _dot(tri_lhs, jnp.concatenate([g_hi, g_mid, g_lo, g_lo], axis=0))
    gh_t = jnp.transpose(jnp.concatenate([gh_all, gh_all], axis=0))
    glast_all = jnp.concatenate(
        [jnp.broadcast_to(gh_all[(qi + 1) * L - 1:(qi + 1) * L, :], (L, LANES))
         for qi in range(nseq)], axis=0)
    egh_all = jnp.exp(gh_all)
    kend_all = jnp.exp(glast_all - gh_all)

    def chunk_matrices(h):
        ls = slice(h * GDN_DK, (h + 1) * GDN_DK)
        q = cq_ref[:, ls]
        k = ck_ref[:, ls]
        v = cv_ref[:, ls]
        q = q * lax.rsqrt(jnp.sum(q * q, axis=-1, keepdims=True) + EPS) * (GDN_DK ** -0.5)
        k = k * lax.rsqrt(jnp.sum(k * k, axis=-1, keepdims=True) + EPS)
        col = slice(SM_DEC + h, SM_DEC + h + 1)
        beta = beta_all[:, SM_BETA + h:SM_BETA + h + 1]
        gh = gh_all[:, col]
        egh = egh_all[:, col]
        decay = jnp.where(incl, jnp.exp(gh - gh_t[col, :]), 0.0)
        kb = k.astype(BF16)
        qkk = lax.dot_general(jnp.concatenate([q.astype(BF16), kb], axis=0),
                              jnp.concatenate([kb, kb], axis=0),
                              (((1,), (1,)), ((), ())), preferred_element_type=F32)
        att_ref[h] = (qkk[:r] * decay).astype(BF16)
        nm = jnp.where(strict, -(beta * qkk[r:] * decay), 0.0)
        nlhs_ref[h] = _lhs3(nm, low_half)
        nb_ref[h] = nm[:, :HALF].astype(BF16)
        p_ref[h] = (eye + nm)[:, :HALF]
        rhs_ref[h] = jnp.concatenate([v * beta, k * (beta * egh)], axis=1)
        qg_ref[h] = q * egh
        ke_ref[h] = k * kend_all[:, col]

    def square_power(h):
        n_pow = nb_ref[h]
        nb_ref[h] = _dot(n_pow, n_pow).astype(BF16)

    def extend_inverse(h):
        pm = p_ref[h]
        p_ref[h] = pm + _dot(pm.astype(BF16), nb_ref[h])

    def first_solve(h):
        tb = p_ref[h].astype(BF16)
        tb_ref[h] = tb
        x0_ref[h] = _dot(tb, rhs_ref[h].astype(BF16))

    def residual(h):
        x0 = x0_ref[h]
        x_hi, x_lo = _split_f32(x0)
        rhs_ref[h] = rhs_ref[h] - x0 + _dot(nlhs_ref[h], _rhs3(x_hi, x_lo.astype(BF16)))

    def refine(h):
        sol = x0_ref[h] + _dot(tb_ref[h], rhs_ref[h].astype(BF16))
        u_ref[h] = sol[:, :GDN_DV]
        w_ref[h] = sol[:, GDN_DV:]

    def state(qi, h):
        return s_ref[h] if carry else s0_ref[qi, h]

    def state_products(h):
        for qi in range(nseq):
            rs = slice(qi * L, (qi + 1) * L)
            lhs = jnp.concatenate([w_ref[h, rs, :], qg_ref[h, rs, :]], axis=0).astype(BF16)
            res = _dot(lhs, state(qi, h).astype(BF16))
            u_ref[h, rs, :] = u_ref[h, rs, :] - res[:L]
            os_ref[h, rs, :] = res[L:]

    def outputs(h):
        ls = slice(h * GDN_DV, (h + 1) * GDN_DV)
        v_new = u_ref[h]
        o = os_ref[h] + _dot(att_ref[h][:, :HALF], v_new.astype(BF16))
        inv = lax.rsqrt(jnp.mean(o * o, axis=-1, keepdims=True) + EPS)
        on = o * inv * ng_ref[...]
        o_ref[:, ls] = (on * _silu(gate_ref[:, ls])).astype(o_ref.dtype)
        for qi in range(nseq):
            rs = slice(qi * L, (qi + 1) * L)
            kv = lax.dot_general(ke_ref[h, rs, :].astype(BF16), v_new[rs].astype(BF16),
                                 (((0,), (0,)), ((), ())), preferred_element_type=F32)
            eg = egh_all[(qi + 1) * L - 1:(qi + 1) * L, SM_DEC + h:SM_DEC + h + 1]
            s_new = state(qi, h) * eg + kv
            if carry:
                s_ref[h] = s_new
            else:
                so_ref[qi, h] = s_new

    stages = [chunk_matrices]
    p = 2
    while p < L:
        stages += [square_power, extend_inverse]
        p *= 2
    stages += [first_solve, residual, refine, state_products, outputs]
    ngroups = GDN_HEADS // group
    for t in range(ngroups + len(stages) - 1):
        for k, stage in enumerate(stages):
            if 0 <= t - k < ngroups:
                for h in range((t - k) * group, (t - k + 1) * group):
                    stage(h)

    if carry:
        @pl.when(pl.program_id(1) == pl.num_programs(1) - 1)
        def _():
            so_ref[...] = s_ref[...]


def _gdn(proj, small, conv_w, alog_pad, dtb_pad, norm_g, conv0, s0, *, bsz, t):
    carry = s0 is None
    r = GDN_ROWS
    n_tok = proj.shape[0]
    tail = CONV_W - 1
    if carry:
        assert t % r == 0
        seq_len, nseq, grid = r, 1, (bsz, t // r)
        row = lambda b, n: b * (t // r) + n
    else:
        assert r % t == 0 and bsz % (r // t) == 0
        seq_len, nseq, grid = t, r // t, (bsz // (r // t), 1)
        row = lambda b, n: b
    qb = COL_QKV_B // GDN_QK
    gb = COL_GATE_B // GDN_V
    rowspec = lambda cb: pl.BlockSpec((r, GDN_QK), lambda b, n: (row(b, n), cb))
    cwspec = lambda cb: pl.BlockSpec((CONV_W, GDN_QK), lambda b, n: (0, cb))
    vec = pl.BlockSpec((1, LANES), lambda b, n: (0, 0))
    in_specs = [rowspec(qb), rowspec(qb + 1), rowspec(qb + 2), rowspec(gb),
                pl.BlockSpec((r, LANES), lambda b, n: (row(b, n), 0)), vec, vec, vec,
                cwspec(0), cwspec(1), cwspec(2)]
    args = [proj, proj, proj, proj, small, alog_pad, dtb_pad, norm_g, conv_w, conv_w, conv_w]
    hshape = (GDN_HEADS, r, LANES)
    sq = (GDN_HEADS, r, r)
    wide = (GDN_HEADS, r, 2 * LANES)
    scratch = ([pltpu.VMEM(sq, BF16)] * 2 + [pltpu.VMEM(wide, BF16)] + [pltpu.VMEM(sq, F32)]
               + [pltpu.VMEM(hshape, BF16)]
               + [pltpu.VMEM(wide, F32)] * 2
               + [pltpu.VMEM(hshape, F32)] * 5
               + [pltpu.VMEM((8 + r, GDN_QK), F32)] * 3 + [pltpu.VMEM((r, GDN_QK), F32)] * 3)
    if carry:
        state = pl.BlockSpec((None, GDN_HEADS, GDN_DK, GDN_DV), lambda b, n: (b, 0, 0, 0))
        scratch += [pltpu.VMEM((GDN_HEADS, GDN_DK, GDN_DV), F32)]
    else:
        state = pl.BlockSpec((nseq, GDN_HEADS, GDN_DK, GDN_DV), lambda b, n: (b, 0, 0, 0))
        c0spec = lambda cb: pl.BlockSpec((nseq, tail, GDN_QK), lambda b, n: (b, 0, cb))
        in_specs += [c0spec(0), c0spec(1), c0spec(2), state]
        args += [conv0, conv0, conv0, s0]
        scratch += [pltpu.VMEM((8 + r, GDN_QK), F32)] * 3
    return pl.pallas_call(
        functools.partial(_gdn_kernel, seq_len=seq_len, carry=carry,
                          group=GDN_GROUP_CARRY if carry else GDN_GROUP_SEQS),
        grid=grid,
        in_specs=in_specs,
        out_specs=[pl.BlockSpec((r, GDN_V), lambda b, n: (row(b, n), 0)), state],
        out_shape=[jax.ShapeDtypeStruct((n_tok, GDN_V), BF16),
                   jax.ShapeDtypeStruct((bsz, GDN_HEADS, GDN_DK, GDN_DV), F32)],
        scratch_shapes=scratch,
        compiler_params=pltpu.CompilerParams(
            dimension_semantics=("arbitrary", "arbitrary"), vmem_limit_bytes=VMEM_LIMIT),
        name="gdn",
    )(*args)


def _merge_kernel(oa_ref, ob_ref, wa_ref, wb_ref, ma_ref, mb_ref, o_ref):
    ya = jnp.dot(oa_ref[...].astype(BF16), wa_ref[...], preferred_element_type=F32)
    yb = jnp.dot(ob_ref[...].astype(BF16), wb_ref[...], preferred_element_type=F32)
    merged = _sigmoid(ma_ref[...]) * ya + _sigmoid(mb_ref[...]) * yb
    o_ref[...] = merged.astype(o_ref.dtype)


def _merge(o_a, o_b, w_br_a, w_br_b, proj):
    n_tok = o_a.shape[0]
    tm = min(1024, n_tok)
    tn = 512
    ma, mb = COL_MA // tn, COL_MB // tn
    return pl.pallas_call(
        _merge_kernel,
        grid=(n_tok // tm, D_MODEL // tn),
        in_specs=[
            pl.BlockSpec((tm, GLA_V), lambda i, j: (i, 0)),
            pl.BlockSpec((tm, GDN_V), lambda i, j: (i, 0)),
            pl.BlockSpec((GLA_V, tn), lambda i, j: (0, j)),
            pl.BlockSpec((GDN_V, tn), lambda i, j: (0, j)),
            pl.BlockSpec((tm, tn), lambda i, j: (i, ma + j)),
            pl.BlockSpec((tm, tn), lambda i, j: (i, mb + j)),
        ],
        out_specs=pl.BlockSpec((tm, tn), lambda i, j: (i, j)),
        out_shape=jax.ShapeDtypeStruct((n_tok, D_MODEL), BF16),
        compiler_params=pltpu.CompilerParams(
            dimension_semantics=("arbitrary", "arbitrary"), vmem_limit_bytes=VMEM_LIMIT),
        name="merge",
    )(o_a, o_b, w_br_a, w_br_b, proj, proj)


def _out_kernel(m_ref, w_ref, x_ref, g_ref, y_ref):
    out = x_ref[...] + jnp.dot(m_ref[...], w_ref[...], preferred_element_type=F32)
    inv = lax.rsqrt(jnp.mean(out * out, axis=-1, keepdims=True) + EPS)
    y_ref[...] = out * inv * g_ref[...]


def _out_proj(merged, w_out, x2d, final_g):
    n_tok = merged.shape[0]
    tm = min(512, n_tok)
    return pl.pallas_call(
        _out_kernel,
        grid=(n_tok // tm,),
        in_specs=[
            pl.BlockSpec((tm, D_MODEL), lambda i: (i, 0)),
            pl.BlockSpec((D_MODEL, D_MODEL), lambda i: (0, 0)),
            pl.BlockSpec((tm, D_MODEL), lambda i: (i, 0)),
            pl.BlockSpec((1, D_MODEL), lambda i: (0, 0)),
        ],
        out_specs=pl.BlockSpec((tm, D_MODEL), lambda i: (i, 0)),
        out_shape=jax.ShapeDtypeStruct((n_tok, D_MODEL), F32),
        compiler_params=pltpu.CompilerParams(
            dimension_semantics=("arbitrary",), vmem_limit_bytes=VMEM_LIMIT),
        name="out_proj",
    )(merged, w_out, x2d, final_g)


def _layer(x, s_gla, s_gdn, conv_buf, p, *, gla_rows):
    bsz, t, d = x.shape
    x2d = x.reshape(bsz * t, d)
    tail = CONV_W - 1
    proj, small = _in_proj(x2d, p["ln_g"], p["w_main"], p["w_small"])
    conv_new = proj.reshape(bsz, t, N_MAIN)[:, t - tail:, COL_QKV_B:COL_QKV_B + CONV_CH]
    o_a, s_gla_new = _gla(proj, small, p["w_alpha2"], p["b_alpha"], p["gla_norm_g"], s_gla,
                          bsz=bsz, t=t, rows=gla_rows)
    o_b, s_gdn_new = _gdn(proj, small, p["conv_w"], p["alog_pad"], p["dtb_pad"],
                          p["gdn_norm_g"], conv_buf, s_gdn, bsz=bsz, t=t)
    merged = _merge(o_a, o_b, p["w_br_a"], p["w_br_b"], proj)
    y = _out_proj(merged, p["w_out"], x2d, p["final_g"])
    return y.reshape(bsz, t, d), s_gla_new, s_gdn_new, conv_new


def _prep_params(ln_in_g, w_in, w_alpha2, b_alpha, conv_w, a_log, dt_bias, gla_norm_g,
                 gdn_norm_g, w_br_a, w_br_b, w_out, final_norm_g):
    wt = jnp.transpose(w_in)
    w_main = _prep_w_main(wt)
    n_small = GLA_RANK + 2 * GDN_HEADS
    w_small = jnp.concatenate(
        [wt[W_IN_OFFS[3]:W_IN_OFFS[4]], wt[W_IN_OFFS[6]:W_IN_OFFS[8]],
         jnp.zeros((LANES - n_small, D_MODEL), F32)], axis=0)
    lane_pad = lambda v: jnp.zeros((1, LANES), F32).at[0, SM_DEC:SM_DEC + GDN_HEADS].set(v)
    return {
        "ln_g": ln_in_g.reshape(1, D_MODEL),
        "w_main": w_main,
        "w_small": w_small,
        "w_alpha2": w_alpha2,
        "b_alpha": b_alpha.reshape(1, GLA_QK),
        "conv_w": conv_w,
        "alog_pad": lane_pad(a_log),
        "dtb_pad": lane_pad(dt_bias),
        "gla_norm_g": gla_norm_g.reshape(1, GLA_DV),
        "gdn_norm_g": gdn_norm_g.reshape(1, GDN_DV),
        "w_br_a": w_br_a.astype(BF16),
        "w_br_b": w_br_b.astype(BF16),
        "w_out": w_out.astype(BF16),
        "final_g": final_norm_g.reshape(1, D_MODEL),
    }


def kernel(x_prompt, x_sample, state_gla, state_gdn, state_conv, ln_in_g, w_in, w_alpha2,
           b_alpha, conv_w, a_log, dt_bias, gla_norm_g, gdn_norm_g, w_br_a, w_br_b, w_out,
           final_norm_g):
    assert ln_in_g.shape[0] == 1, "single layer"
    p = _prep_params(ln_in_g[0], w_in[0], w_alpha2[0], b_alpha[0], conv_w[0], a_log[0],
                     dt_bias[0], gla_norm_g[0], gdn_norm_g[0], w_br_a[0], w_br_b[0], w_out[0],
                     final_norm_g)
    yp, gla_p, gdn_p, conv_p = _layer(x_prompt, None, None, None, p, gla_rows=GLA_PROMPT_ROWS)
    ys, gla_s, gdn_s, conv_s = _layer(x_sample, state_gla[0], state_gdn[0], state_conv[0], p,
                                      gla_rows=GLA_SAMPLE_ROWS)
    return (yp, ys, gla_p[None], gdn_p[None], conv_p[None], gla_s[None], gdn_s[None],
            conv_s[None])
```

```python
import functools

import jax
import jax.numpy as jnp
from jax import lax
from jax.experimental import pallas as pl
from jax.experimental.pallas import tpu as pltpu

F32 = jnp.float32
BF16 = jnp.bfloat16

D_MODEL = 2048
GLA_HEADS = 4
GLA_DK = 256
GLA_DV = 512
GLA_RANK = 16
GLA_TAU = 16.0
GDN_HEADS = 16
GDN_DK = 128
GDN_DV = 128
CONV_W = 4
CHUNK = 64
EPS = 1e-6

GLA_QK = GLA_HEADS * GLA_DK
GLA_V = GLA_HEADS * GLA_DV
GDN_QK = GDN_HEADS * GDN_DK
GDN_V = GDN_HEADS * GDN_DV
CONV_CH = 2 * GDN_QK + GDN_V

LANES = 128
COL_QA = 0
COL_KA = COL_QA + GLA_QK
COL_VA = COL_KA + GLA_QK
COL_GATE_A = COL_VA + GLA_V
COL_QKV_B = COL_GATE_A + GLA_V
COL_GATE_B = COL_QKV_B + CONV_CH
COL_MA = COL_GATE_B + GDN_V
COL_MB = COL_MA + D_MODEL
N_MAIN = COL_MB + D_MODEL
SM_LR = 0
SM_BETA = GLA_RANK
SM_DEC = SM_BETA + GDN_HEADS

VMEM_LIMIT = 56 * 1024 * 1024


def _bdot(a, b):
    return jnp.dot(a.astype(BF16), b.astype(BF16), preferred_element_type=F32)


def _bdot_nt(a, b):
    return lax.dot_general(a.astype(BF16), b.astype(BF16), (((1,), (1,)), ((), ())),
                           preferred_element_type=F32)


def _bdot_tn(a, b):
    return lax.dot_general(a.astype(BF16), b.astype(BF16), (((0,), (0,)), ((), ())),
                           preferred_element_type=F32)


def _dot(a, b):
    return jnp.dot(a, b, preferred_element_type=F32)


def _sigmoid(x):
    return 1.0 / (1.0 + jnp.exp(-x))


def _silu(x):
    return x * _sigmoid(x)


def _softplus(x):
    return jnp.maximum(x, 0.0) + jnp.log(1.0 + jnp.exp(-jnp.abs(x)))


W_IN_SIZES = (GLA_QK, GLA_QK, GLA_V, GLA_RANK, GLA_V, CONV_CH, GDN_HEADS, GDN_HEADS, GDN_V,
              D_MODEL, D_MODEL)
W_IN_OFFS = tuple(sum(W_IN_SIZES[:i]) for i in range(len(W_IN_SIZES) + 1))
PREP_STARTS = (0, COL_GATE_A, COL_GATE_B)
PREP_SHIFTS = (0, W_IN_OFFS[4] - COL_GATE_A, W_IN_OFFS[8] - COL_GATE_B)
PREP_ROWS = 512
PREP_HALO = 64


def _wprep_kernel(a_ref, nxt_ref, o_ref):
    j = pl.program_id(0)
    bounds = tuple(s // PREP_ROWS for s in PREP_STARTS) + (N_MAIN // PREP_ROWS,)
    for run, shift in enumerate(PREP_SHIFTS):
        @pl.when((j >= bounds[run]) & (j < bounds[run + 1]))
        def _(shift=shift):
            if shift == 0:
                o_ref[...] = a_ref[...].astype(BF16)
            else:
                x = jnp.concatenate([a_ref[...], nxt_ref[...]], axis=0)
                o_ref[...] = x[shift:shift + PREP_ROWS, :].astype(BF16)


def _prep_w_main(wt):
    d = wt.shape[1]
    assert all(s % PREP_ROWS == 0 for s in PREP_STARTS)
    assert max(PREP_SHIFTS) <= PREP_HALO and all(s % 16 == 0 for s in PREP_SHIFTS)
    return pl.pallas_call(
        _wprep_kernel,
        grid=(N_MAIN // PREP_ROWS,),
        in_specs=[
            pl.BlockSpec((PREP_ROWS, d), lambda j: (j, 0)),
            pl.BlockSpec((PREP_HALO, d), lambda j: ((j + 1) * (PREP_ROWS // PREP_HALO), 0)),
        ],
        out_specs=pl.BlockSpec((PREP_ROWS, d), lambda j: (j, 0)),
        out_shape=jax.ShapeDtypeStruct((N_MAIN, d), BF16),
        compiler_params=pltpu.CompilerParams(
            dimension_semantics=("arbitrary",), vmem_limit_bytes=VMEM_LIMIT),
        name="w_prep",
    )(wt, wt)


def _dot_nt(a, b):
    return lax.dot_general(a, b, (((1,), (1,)), ((), ())), preferred_element_type=F32)


IN_PROJ_TM = 1024
IN_PROJ_TN = 1536


def _in_proj_kernel(x_ref, g_ref, w_ref, ws_ref, o_ref, os_ref, h_ref):
    @pl.when(pl.program_id(1) == 0)
    def _():
        x = x_ref[...]
        inv = lax.rsqrt(jnp.mean(x * x, axis=-1, keepdims=True) + EPS)
        h = (x * inv * g_ref[...]).astype(BF16)
        h_ref[...] = h
        os_ref[...] = _dot_nt(h, ws_ref[...].astype(BF16))

    o_ref[...] = _dot_nt(h_ref[...], w_ref[...])


def _in_proj(x2d, ln_g, w_main, w_small):
    n_tok, d = x2d.shape
    tm = min(IN_PROJ_TM, n_tok)
    tn = IN_PROJ_TN
    assert N_MAIN % tn == 0
    return pl.pallas_call(
        _in_proj_kernel,
        grid=(n_tok // tm, N_MAIN // tn),
        in_specs=[
            pl.BlockSpec((tm, d), lambda i, j: (i, 0)),
            pl.BlockSpec((1, d), lambda i, j: (0, 0)),
            pl.BlockSpec((tn, d), lambda i, j: (j, 0)),
            pl.BlockSpec((LANES, d), lambda i, j: (0, 0)),
        ],
        out_specs=[
            pl.BlockSpec((tm, tn), lambda i, j: (i, j)),
            pl.BlockSpec((tm, LANES), lambda i, j: (i, 0)),
        ],
        out_shape=[jax.ShapeDtypeStruct((n_tok, N_MAIN), F32),
                   jax.ShapeDtypeStruct((n_tok, LANES), F32)],
        scratch_shapes=[pltpu.VMEM((tm, d), BF16)],
        compiler_params=pltpu.CompilerParams(
            dimension_semantics=("arbitrary", "arbitrary"), vmem_limit_bytes=VMEM_LIMIT),
        name="in_proj",
    )(x2d, ln_g, w_main, w_small)


GLA_PROMPT_ROWS = 512
GLA_SAMPLE_ROWS = 32


def _gla_kernel(*refs, seq_len, rows, carry):
    n_in = 8 if carry else 9
    q_ref, k_ref, v_ref, gate_ref, sm_ref, wal_ref, bal_ref, ng_ref = refs[:8]
    s0_ref = None if carry else refs[8]
    o_ref, so_ref = refs[n_in:n_in + 2]
    qd_ref, ke_ref, ebl_ref, oacc_ref = refs[n_in + 2:n_in + 6]
    s_ref = refs[n_in + 6] if carry else None
    r, L = rows, seq_len
    nsub = r // L
    g = min(r, CHUNK)

    if carry:
        @pl.when(pl.program_id(1) == 0)
        def _():
            s_ref[...] = jnp.zeros_like(s_ref)

    ri = lax.broadcasted_iota(jnp.int32, (r, r), 0)
    ci = lax.broadcasted_iota(jnp.int32, (r, r), 1)
    incl = ((ri // L) == (ci // L)) & (ri >= ci)
    gi = lax.broadcasted_iota(jnp.int32, (g, 2 * LANES), 0)
    gl = lax.broadcasted_iota(jnp.int32, (g, 2 * LANES), 1)
    gc = gl % g
    tri3 = ((gl < 3 * g) & ((gi // L) == (gc // L)) & (gi >= gc)).astype(BF16)

    def seg_cumsum(x):
        outs = []
        for gq in range(r // g):
            xg = x[gq * g:(gq + 1) * g]
            hi, r1 = _split_f32(xg)
            mid, r2 = _split_f32(r1)
            parts = [hi, mid, r2.astype(BF16)]
            if 3 * g < 2 * LANES:
                parts.append(jnp.zeros((2 * LANES - 3 * g, xg.shape[1]), BF16))
            outs.append(_dot(tri3, jnp.concatenate(parts, axis=0)))
        return outs[0] if len(outs) == 1 else jnp.concatenate(outs, axis=0)

    for h in range(GLA_HEADS):
        lk = slice(h * GLA_DK, (h + 1) * GLA_DK)
        lv = slice(h * GLA_DV, (h + 1) * GLA_DV)
        q = q_ref[:, lk] * (GLA_DK ** -0.5)
        k = k_ref[:, lk]
        lr = sm_ref[:, SM_LR:SM_LR + GLA_RANK]
        z = _bdot(lr, wal_ref[:, lk]) + bal_ref[:, lk]
        la = -_softplus(-z) / GLA_TAU
        b = seg_cumsum(la)
        lasts = [jnp.broadcast_to(b[(qi + 1) * L - 1:(qi + 1) * L, :], (L, GLA_DK))
                 for qi in range(nsub)]
        b_last = lasts[0] if nsub == 1 else jnp.concatenate(lasts, axis=0)
        q_d = q * jnp.exp(b)
        k_d = k * jnp.exp(-b)
        att = jnp.where(incl, _bdot_nt(q_d, k_d), 0.0)
        oacc_ref[h] = _bdot(att, v_ref[:, lv])
        qd_ref[h] = q_d
        ke_ref[h] = k * jnp.exp(b_last - b)
        ebl_ref[h] = jnp.exp(b_last)

    for qi in range(nsub):
        rs = slice(qi * L, (qi + 1) * L)
        for h in range(GLA_HEADS):
            lv = slice(h * GLA_DV, (h + 1) * GLA_DV)
            s = s_ref[h] if carry else s0_ref[qi, h]
            oacc_ref[h, rs, :] = oacc_ref[h, rs, :] + _bdot(qd_ref[h, rs, :], s)
            kv = _bdot_tn(ke_ref[h, rs, :], v_ref[rs, lv])
            eb = ebl_ref[h, qi * L:qi * L + 1, :]
            col = jnp.transpose(jnp.broadcast_to(eb, (LANES, GLA_DK)))
            s_new = s * jnp.concatenate([col] * (GLA_DV // LANES), axis=1) + kv
            if carry:
                s_ref[h] = s_new
            else:
                so_ref[qi, h] = s_new

    for h in range(GLA_HEADS):
        lv = slice(h * GLA_DV, (h + 1) * GLA_DV)
        o = oacc_ref[h]
        inv = lax.rsqrt(jnp.mean(o * o, axis=-1, keepdims=True) + EPS)
        on = o * inv * ng_ref[...]
        o_ref[:, lv] = (on * _silu(gate_ref[:, lv])).astype(o_ref.dtype)

    if carry:
        @pl.when(pl.program_id(1) == pl.num_programs(1) - 1)
        def _():
            so_ref[...] = s_ref[...]


def _gla(proj, small, w_alpha2, b_alpha, norm_g, s0, *, bsz, t, rows):
    carry = s0 is None
    r = rows
    n_tok = proj.shape[0]
    if carry:
        seq_len = min(CHUNK, t)
        assert t % r == 0 and r % seq_len == 0
        grid = (bsz, t // r)
        row = lambda b, n: b * (t // r) + n
    else:
        assert t <= CHUNK and r % t == 0 and bsz % (r // t) == 0
        seq_len = t
        grid = (bsz // (r // t), 1)
        row = lambda b, n: b
    nsub = r // seq_len
    qb, kb = COL_QA // GLA_QK, COL_KA // GLA_QK
    vb, gb = COL_VA // GLA_V, COL_GATE_A // GLA_V
    in_specs = [
        pl.BlockSpec((r, GLA_QK), lambda b, n: (row(b, n), qb)),
        pl.BlockSpec((r, GLA_QK), lambda b, n: (row(b, n), kb)),
        pl.BlockSpec((r, GLA_V), lambda b, n: (row(b, n), vb)),
        pl.BlockSpec((r, GLA_V), lambda b, n: (row(b, n), gb)),
        pl.BlockSpec((r, LANES), lambda b, n: (row(b, n), 0)),
        pl.BlockSpec((GLA_RANK, GLA_QK), lambda b, n: (0, 0)),
        pl.BlockSpec((1, GLA_QK), lambda b, n: (0, 0)),
        pl.BlockSpec((1, GLA_DV), lambda b, n: (0, 0)),
    ]
    args = [proj, proj, proj, proj, small, w_alpha2, b_alpha, norm_g]
    scratch = [pltpu.VMEM((GLA_HEADS, r, GLA_DK), F32)] * 3 + [
        pltpu.VMEM((GLA_HEADS, r, GLA_DV), F32)]
    if carry:
        state = pl.BlockSpec((None, GLA_HEADS, GLA_DK, GLA_DV), lambda b, n: (b, 0, 0, 0))
        scratch += [pltpu.VMEM((GLA_HEADS, GLA_DK, GLA_DV), F32)]
    else:
        state = pl.BlockSpec((nsub, GLA_HEADS, GLA_DK, GLA_DV), lambda b, n: (b, 0, 0, 0))
        in_specs += [state]
        args += [s0]
    return pl.pallas_call(
        functools.partial(_gla_kernel, seq_len=seq_len, rows=r, carry=carry),
        grid=grid,
        in_specs=in_specs,
        out_specs=[pl.BlockSpec((r, GLA_V), lambda b, n: (row(b, n), 0)), state],
        out_shape=[jax.ShapeDtypeStruct((n_tok, GLA_V), BF16),
                   jax.ShapeDtypeStruct((bsz, GLA_HEADS, GLA_DK, GLA_DV), F32)],
        scratch_shapes=scratch,
        compiler_params=pltpu.CompilerParams(
            dimension_semantics=("arbitrary", "arbitrary"), vmem_limit_bytes=VMEM_LIMIT),
        name="gla",
    )(*args)


GDN_ROWS = 64
HALF = LANES // 2
GDN_GROUP_CARRY = GDN_HEADS
GDN_GROUP_SEQS = 1


def _split_f32(x):
    hi = x.astype(BF16)
    return hi, x - hi.astype(F32)


def _lhs3(x, low_half):
    hi, lo = _split_f32(x)
    t0 = jnp.where(low_half, x, lo).astype(BF16)
    t1 = jnp.where(low_half, hi, jnp.zeros_like(hi))
    return jnp.concatenate([t0, t1], axis=1)


def _rhs3(hi, lo):
    return jnp.concatenate([hi, hi, lo, lo], axis=0)


def _gdn_kernel(*refs, seq_len, carry, group):
    r = GDN_ROWS
    L = seq_len
    nseq = r // L
    tail = CONV_W - 1
    pad = 8
    n_in = 11 if carry else 15
    (q_ref, k_ref, v_ref, gate_ref, sm_ref, alog_ref, dtb_ref, ng_ref, cwq_ref, cwk_ref,
     cwv_ref) = refs[:11]
    cq0_ref, ck0_ref, cv0_ref, s0_ref = (None,) * 4 if carry else refs[11:15]
    o_ref, so_ref = refs[n_in:n_in + 2]
    (nb_ref, tb_ref, nlhs_ref, p_ref, att_ref, rhs_ref, x0_ref, qg_ref, ke_ref, u_ref, w_ref,
     os_ref, eq_ref, ek_ref, ev_ref, cq_ref, ck_ref, cv_ref) = refs[n_in + 2:n_in + 20]
    ri = lax.broadcasted_iota(jnp.int32, (r, LANES), 0)
    li = lax.broadcasted_iota(jnp.int32, (r, LANES), 1)
    cj = li % HALF
    low_half = li < HALF
    if carry:
        (s_ref,) = refs[n_in + 20:]
        hq_ref = hk_ref = hv_ref = None

        @pl.when(pl.program_id(1) == 0)
        def _():
            s_ref[...] = jnp.zeros_like(s_ref)
            for e_ref in (eq_ref, ek_ref, ev_ref):
                e_ref[0:pad, :] = jnp.zeros((pad, GDN_QK), F32)
    else:
        s_ref = None
        hq_ref, hk_ref, hv_ref = refs[n_in + 20:]

        @pl.when(pl.program_id(0) == 0)
        def _():
            for e_ref in (eq_ref, ek_ref, ev_ref, hq_ref, hk_ref, hv_ref):
                e_ref[...] = jnp.zeros_like(e_ref)

    row_t = lax.broadcasted_iota(jnp.int32, (r, 1), 0) % L

    def conv(e_ref, x_ref, w_ref, out_ref, h_ref, c0_ref):
        e_ref[pad:pad + r, :] = x_ref[...]
        if not carry:
            for qi in range(nseq):
                h_ref[qi * L + pad - tail:qi * L + pad, :] = c0_ref[qi]
        acc = None
        for i in range(CONV_W):
            sh = tail - i
            xs = e_ref[pad - sh:pad - sh + r, :]
            if not carry and sh > 0:
                xs = jnp.where(row_t >= sh, xs, h_ref[pad - sh:pad - sh + r, :])
            term = xs * w_ref[i:i + 1, :]
            acc = term if acc is None else acc + term
        if carry:
            e_ref[pad - tail:pad, :] = x_ref[r - tail:r, :]
        out_ref[...] = _silu(acc)

    conv(eq_ref, q_ref, cwq_ref, cq_ref, hq_ref, cq0_ref)
    conv(ek_ref, k_ref, cwk_ref, ck_ref, hk_ref, ck0_ref)
    conv(ev_ref, v_ref, cwv_ref, cv_ref, hv_ref, cv0_ref)
    sm = sm_ref[...]
    same = (ri // L) == (cj // L)
    incl = same & (ri >= cj)
    strict = same & (ri > cj)
    eye = (ri == cj).astype(F32)

    beta_all = _sigmoid(sm)
    g_all = -jnp.exp(alog_ref[...]) * _softplus(sm + dtb_ref[...])
    g_hi, g_r1 = _split_f32(g_all)
    g_mid, g_r2 = _split_f32(g_r1)
    g_lo = g_r2.astype(BF16)
    tri = incl.astype(BF16)
    tri_lhs = jnp.concatenate([tri, jnp.where(low_half, tri, jnp.zeros_like(tri))], axis=1)
    gh_all = _dot(tri_lhs, jnp.concatenate([g_hi, g_mid, g_lo, g_lo], axis=0))
    gh_t = jnp.transpose(jnp.concatenate([gh_all, gh_all], axis=0))
    glast_all = jnp.concatenate(
        [jnp.broadcast_to(gh_all[(qi + 1) * L - 1:(qi + 1) * L, :], (L, LANES))
         for qi in range(nseq)], axis=0)
    egh_all = jnp.exp(gh_all)
    kend_all = jnp.exp(glast_all - gh_all)

    def chunk_matrices(h):
        ls = slice(h * GDN_DK, (h + 1) * GDN_DK)
        q = cq_ref[:, ls]
        k = ck_ref[:, ls]
        v = cv_ref[:, ls]
        q = q * lax.rsqrt(jnp.sum(q * q, axis=-1, keepdims=True) + EPS) * (GDN_DK ** -0.5)
        k = k * lax.rsqrt(jnp.sum(k * k, axis=-1, keepdims=True) + EPS)
        col = slice(SM_DEC + h, SM_DEC + h + 1)
        beta = beta_all[:, SM_BETA + h:SM_BETA + h + 1]
        gh = gh_all[:, col]
        egh = egh_all[:, col]
        decay = jnp.where(incl, jnp.exp(gh - gh_t[col, :]), 0.0)
        kb = k.astype(BF16)
        qkk = lax.dot_general(jnp.concatenate([q.astype(BF16), kb], axis=0),
                              jnp.concatenate([kb, kb], axis=0),
                              (((1,), (1,)), ((), ())), preferred_element_type=F32)
        att_ref[h] = (qkk[:r] * decay).astype(BF16)
        nm = jnp.where(strict, -(beta * qkk[r:] * decay), 0.0)
        nlhs_ref[h] = _lhs3(nm, low_half)
        nb_ref[h] = nm[:, :HALF].astype(BF16)
        p_ref[h] = (eye + nm)[:, :HALF]
        rhs_ref[h] = jnp.concatenate([v * beta, k * (beta * egh)], axis=1)
        qg_ref[h] = q * egh
        ke_ref[h] = k * kend_all[:, col]

    def square_power(h):
        n_pow = nb_ref[h]
        nb_ref[h] = _dot(n_pow, n_pow).astype(BF16)

    def extend_inverse(h):
        pm = p_ref[h]
        p_ref[h] = pm + _dot(pm.astype(BF16), nb_ref[h])

    def first_solve(h):
        tb = p_ref[h].astype(BF16)
        tb_ref[h] = tb
        x0_ref[h] = _dot(tb, rhs_ref[h].astype(BF16))

    def residual(h):
        x0 = x0_ref[h]
        x_hi, x_lo = _split_f32(x0)
        rhs_ref[h] = rhs_ref[h] - x0 + _dot(nlhs_ref[h], _rhs3(x_hi, x_lo.astype(BF16)))

    def refine(h):
        sol = x0_ref[h] + _dot(tb_ref[h], rhs_ref[h].astype(BF16))
        u_ref[h] = sol[:, :GDN_DV]
        w_ref[h] = sol[:, GDN_DV:]

    def state(qi, h):
        return s_ref[h] if carry else s0_ref[qi, h]

    def state_products(h):
        for qi in range(nseq):
            rs = slice(qi * L, (qi + 1) * L)
            lhs = jnp.concatenate([w_ref[h, rs, :], qg_ref[h, rs, :]], axis=0).astype(BF16)
            res = _dot(lhs, state(qi, h).astype(BF16))
            u_ref[h, rs, :] = u_ref[h, rs, :] - res[:L]
            os_ref[h, rs, :] = res[L:]

    def outputs(h):
        ls = slice(h * GDN_DV, (h + 1) * GDN_DV)
        v_new = u_ref[h]
        o = os_ref[h] + _dot(att_ref[h][:, :HALF], v_new.astype(BF16))
        inv = lax.rsqrt(jnp.mean(o * o, axis=-1, keepdims=True) + EPS)
        on = o * inv * ng_ref[...]
        o_ref[:, ls] = (on * _silu(gate_ref[:, ls])).astype(o_ref.dtype)
        for qi in range(nseq):
            rs = slice(qi * L, (qi + 1) * L)
            kv = lax.dot_general(ke_ref[h, rs, :].astype(BF16), v_new[rs].astype(BF16),
                                 (((0,), (0,)), ((), ())), preferred_element_type=F32)
            eg = egh_all[(qi + 1) * L - 1:(qi + 1) * L, SM_DEC + h:SM_DEC + h + 1]
            s_new = state(qi, h) * eg + kv
            if carry:
                s_ref[h] = s_new
            else:
                so_ref[qi, h] = s_new

    stages = [chunk_matrices]
    p = 2
    while p < L:
        stages += [square_power, extend_inverse]
        p *= 2
    stages += [first_solve, residual, refine, state_products, outputs]
    ngroups = GDN_HEADS // group
    for t in range(ngroups + len(stages) - 1):
        for k, stage in enumerate(stages):
            if 0 <= t - k < ngroups:
                for h in range((t - k) * group, (t - k + 1) * group):
                    stage(h)

    if carry:
        @pl.when(pl.program_id(1) == pl.num_programs(1) - 1)
        def _():
            so_ref[...] = s_ref[...]


def _gdn(proj, small, conv_w, alog_pad, dtb_pad, norm_g, conv0, s0, *, bsz, t):
    carry = s0 is None
    r = GDN_ROWS
    n_tok = proj.shape[0]
    tail = CONV_W - 1
    if carry:
        assert t % r == 0
        seq_len, nseq, grid = r, 1, (bsz, t // r)
        row = lambda b, n: b * (t // r) + n
    else:
        assert r % t == 0 and bsz % (r // t) == 0
        seq_len, nseq, grid = t, r // t, (bsz // (r // t), 1)
        row = lambda b, n: b
    qb = COL_QKV_B // GDN_QK
    gb = COL_GATE_B // GDN_V
    rowspec = lambda cb: pl.BlockSpec((r, GDN_QK), lambda b, n: (row(b, n), cb))
    cwspec = lambda cb: pl.BlockSpec((CONV_W, GDN_QK), lambda b, n: (0, cb))
    vec = pl.BlockSpec((1, LANES), lambda b, n: (0, 0))
    in_specs = [rowspec(qb), rowspec(qb + 1), rowspec(qb + 2), rowspec(gb),
                pl.BlockSpec((r, LANES), lambda b, n: (row(b, n), 0)), vec, vec, vec,
                cwspec(0), cwspec(1), cwspec(2)]
    args = [proj, proj, proj, proj, small, alog_pad, dtb_pad, norm_g, conv_w, conv_w, conv_w]
    hshape = (GDN_HEADS, r, LANES)
    sq = (GDN_HEADS, r, r)
    wide = (GDN_HEADS, r, 2 * LANES)
    scratch = ([pltpu.VMEM(sq, BF16)] * 2 + [pltpu.VMEM(wide, BF16)] + [pltpu.VMEM(sq, F32)]
               + [pltpu.VMEM(hshape, BF16)]
               + [pltpu.VMEM(wide, F32)] * 2
               + [pltpu.VMEM(hshape, F32)] * 5
               + [pltpu.VMEM((8 + r, GDN_QK), F32)] * 3 + [pltpu.VMEM((r, GDN_QK), F32)] * 3)
    if carry:
        state = pl.BlockSpec((None, GDN_HEADS, GDN_DK, GDN_DV), lambda b, n: (b, 0, 0, 0))
        scratch += [pltpu.VMEM((GDN_HEADS, GDN_DK, GDN_DV), F32)]
    else:
        state = pl.BlockSpec((nseq, GDN_HEADS, GDN_DK, GDN_DV), lambda b, n: (b, 0, 0, 0))
        c0spec = lambda cb: pl.BlockSpec((nseq, tail, GDN_QK), lambda b, n: (b, 0, cb))
        in_specs += [c0spec(0), c0spec(1), c0spec(2), state]
        args += [conv0, conv0, conv0, s0]
        scratch += [pltpu.VMEM((8 + r, GDN_QK), F32)] * 3
    return pl.pallas_call(
        functools.partial(_gdn_kernel, seq_len=seq_len, carry=carry,
                          group=GDN_GROUP_CARRY if carry else GDN_GROUP_SEQS),
        grid=grid,
        in_specs=in_specs,
        out_specs=[pl.BlockSpec((r, GDN_V), lambda b, n: (row(b, n), 0)), state],
        out_shape=[jax.ShapeDtypeStruct((n_tok, GDN_V), BF16),
                   jax.ShapeDtypeStruct((bsz, GDN_HEADS, GDN_DK, GDN_DV), F32)],
        scratch_shapes=scratch,
        compiler_params=pltpu.CompilerParams(
            dimension_semantics=("arbitrary", "arbitrary"), vmem_limit_bytes=VMEM_LIMIT),
        name="gdn",
    )(*args)


def _merge_kernel(oa_ref, ob_ref, wa_ref, wb_ref, ma_ref, mb_ref, o_ref):
    ya = jnp.dot(oa_ref[...].astype(BF16), wa_ref[...], preferred_element_type=F32)
    yb = jnp.dot(ob_ref[...].astype(BF16), wb_ref[...], preferred_element_type=F32)
    merged = _sigmoid(ma_ref[...]) * ya + _sigmoid(mb_ref[...]) * yb
    o_ref[...] = merged.astype(o_ref.dtype)


def _merge(o_a, o_b, w_br_a, w_br_b, proj):
    n_tok = o_a.shape[0]
    tm = min(1024, n_tok)
    tn = 512
    ma, mb = COL_MA // tn, COL_MB // tn
    return pl.pallas_call(
        _merge_kernel,
        grid=(n_tok // tm, D_MODEL // tn),
        in_specs=[
            pl.BlockSpec((tm, GLA_V), lambda i, j: (i, 0)),
            pl.BlockSpec((tm, GDN_V), lambda i, j: (i, 0)),
            pl.BlockSpec((GLA_V, tn), lambda i, j: (0, j)),
            pl.BlockSpec((GDN_V, tn), lambda i, j: (0, j)),
            pl.BlockSpec((tm, tn), lambda i, j: (i, ma + j)),
            pl.BlockSpec((tm, tn), lambda i, j: (i, mb + j)),
        ],
        out_specs=pl.BlockSpec((tm, tn), lambda i, j: (i, j)),
        out_shape=jax.ShapeDtypeStruct((n_tok, D_MODEL), BF16),
        compiler_params=pltpu.CompilerParams(
            dimension_semantics=("arbitrary", "arbitrary"), vmem_limit_bytes=VMEM_LIMIT),
        name="merge",
    )(o_a, o_b, w_br_a, w_br_b, proj, proj)


def _out_kernel(m_ref, w_ref, x_ref, g_ref, y_ref):
    out = x_ref[...] + jnp.dot(m_ref[...], w_ref[...], preferred_element_type=F32)
    inv = lax.rsqrt(jnp.mean(out * out, axis=-1, keepdims=True) + EPS)
    y_ref[...] = out * inv * g_ref[...]


def _out_proj(merged, w_out, x2d, final_g):
    n_tok = merged.shape[0]
    tm = min(512, n_tok)
    return pl.pallas_call(
        _out_kernel,
        grid=(n_tok // tm,),
        in_specs=[
            pl.BlockSpec((tm, D_MODEL), lambda i: (i, 0)),
            pl.BlockSpec((D_MODEL, D_MODEL), lambda i: (0, 0)),
            pl.BlockSpec((tm, D_MODEL), lambda i: (i, 0)),
            pl.BlockSpec((1, D_MODEL), lambda i: (0, 0)),
        ],
        out_specs=pl.BlockSpec((tm, D_MODEL), lambda i: (i, 0)),
        out_shape=jax.ShapeDtypeStruct((n_tok, D_MODEL), F32),
        compiler_params=pltpu.CompilerParams(
            dimension_semantics=("arbitrary",), vmem_limit_bytes=VMEM_LIMIT),
        name="out_proj",
    )(merged, w_out, x2d, final_g)


def _layer(x, s_gla, s_gdn, conv_buf, p, *, gla_rows):
    bsz, t, d = x.shape
    x2d = x.reshape(bsz * t, d)
    tail = CONV_W - 1
    proj, small = _in_proj(x2d, p["ln_g"], p["w_main"], p["w_small"])
    conv_new = proj.reshape(bsz, t, N_MAIN)[:, t - tail:, COL_QKV_B:COL_QKV_B + CONV_CH]
    o_a, s_gla_new = _gla(proj, small, p["w_alpha2"], p["b_alpha"], p["gla_norm_g"], s_gla,
                          bsz=bsz, t=t, rows=gla_rows)
    o_b, s_gdn_new = _gdn(proj, small, p["conv_w"], p["alog_pad"], p["dtb_pad"],
                          p["gdn_norm_g"], conv_buf, s_gdn, bsz=bsz, t=t)
    merged = _merge(o_a, o_b, p["w_br_a"], p["w_br_b"], proj)
    y = _out_proj(merged, p["w_out"], x2d, p["final_g"])
    return y.reshape(bsz, t, d), s_gla_new, s_gdn_new, conv_new


def _prep_params(ln_in_g, w_in, w_alpha2, b_alpha, conv_w, a_log, dt_bias, gla_norm_g,
                 gdn_norm_g, w_br_a, w_br_b, w_out, final_norm_g):
    wt = jnp.transpose(w_in)
    w_main = _prep_w_main(wt)
    n_small = GLA_RANK + 2 * GDN_HEADS
    w_small = jnp.concatenate(
        [wt[W_IN_OFFS[3]:W_IN_OFFS[4]], wt[W_IN_OFFS[6]:W_IN_OFFS[8]],
         jnp.zeros((LANES - n_small, D_MODEL), F32)], axis=0)
    lane_pad = lambda v: jnp.zeros((1, LANES), F32).at[0, SM_DEC:SM_DEC + GDN_HEADS].set(v)
    return {
        "ln_g": ln_in_g.reshape(1, D_MODEL),
        "w_main": w_main,
        "w_small": w_small,
        "w_alpha2": w_alpha2,
        "b_alpha": b_alpha.reshape(1, GLA_QK),
        "conv_w": conv_w,
        "alog_pad": lane_pad(a_log),
        "dtb_pad": lane_pad(dt_bias),
        "gla_norm_g": gla_norm_g.reshape(1, GLA_DV),
        "gdn_norm_g": gdn_norm_g.reshape(1, GDN_DV),
        "w_br_a": w_br_a.astype(BF16),
        "w_br_b": w_br_b.astype(BF16),
        "w_out": w_out.astype(BF16),
        "final_g": final_norm_g.reshape(1, D_MODEL),
    }


def kernel(x_prompt, x_sample, state_gla, state_gdn, state_conv, ln_in_g, w_in, w_alpha2,
           b_alpha, conv_w, a_log, dt_bias, gla_norm_g, gdn_norm_g, w_br_a, w_br_b, w_out,
           final_norm_g):
    assert ln_in_g.shape[0] == 1, "single layer"
    p = _prep_params(ln_in_g[0], w_in[0], w_alpha2[0], b_alpha[0], conv_w[0], a_log[0],
                     dt_bias[0], gla_norm_g[0], gdn_norm_g[0], w_br_a[0], w_br_b[0], w_out[0],
                     final_norm_g)
    yp, gla_p, gdn_p, conv_p = _layer(x_prompt, None, None, None, p, gla_rows=GLA_PROMPT_ROWS)
    ys, gla_s, gdn_s, conv_s = _layer(x_sample, state_gla[0], state_gdn[0], state_conv[0], p,
                                      gla_rows=GLA_SAMPLE_ROWS)
    return (yp, ys, gla_p[None], gdn_p[None], conv_p[None], gla_s[None], gdn_s[None],
            conv_s[None])
```

```python
import functools

import jax
import jax.numpy as jnp
from jax import lax
from jax.experimental import pallas as pl
from jax.experimental.pallas import tpu as pltpu

F32 = jnp.float32
BF16 = jnp.bfloat16

D_MODEL = 2048
GLA_HEADS = 4
GLA_DK = 256
GLA_DV = 512
GLA_RANK = 16
GLA_TAU = 16.0
GDN_HEADS = 16
GDN_DK = 128
GDN_DV = 128
CONV_W = 4
CHUNK = 64
EPS = 1e-6

GLA_QK = GLA_HEADS * GLA_DK
GLA_V = GLA_HEADS * GLA_DV
GDN_QK = GDN_HEADS * GDN_DK
GDN_V = GDN_HEADS * GDN_DV
CONV_CH = 2 * GDN_QK + GDN_V

LANES = 128
COL_QA = 0
COL_KA = COL_QA + GLA_QK
COL_VA = COL_KA + GLA_QK
COL_GATE_A = COL_VA + GLA_V
COL_QKV_B = COL_GATE_A + GLA_V
COL_GATE_B = COL_QKV_B + CONV_CH
COL_MA = COL_GATE_B + GDN_V
COL_MB = COL_MA + D_MODEL
N_MAIN = COL_MB + D_MODEL
SM_LR = 0
SM_BETA = GLA_RANK
SM_DEC = SM_BETA + GDN_HEADS

VMEM_LIMIT = 56 * 1024 * 1024


def _bdot(a, b):
    return jnp.dot(a.astype(BF16), b.astype(BF16), preferred_element_type=F32)


def _bdot_nt(a, b):
    return lax.dot_general(a.astype(BF16), b.astype(BF16), (((1,), (1,)), ((), ())),
                           preferred_element_type=F32)


def _bdot_tn(a, b):
    return lax.dot_general(a.astype(BF16), b.astype(BF16), (((0,), (0,)), ((), ())),
                           preferred_element_type=F32)


def _dot(a, b):
    return jnp.dot(a, b, preferred_element_type=F32)


def _sigmoid(x):
    return 1.0 / (1.0 + jnp.exp(-x))


def _silu(x):
    return x * _sigmoid(x)


def _softplus(x):
    return jnp.maximum(x, 0.0) + jnp.log(1.0 + jnp.exp(-jnp.abs(x)))


W_IN_SIZES = (GLA_QK, GLA_QK, GLA_V, GLA_RANK, GLA_V, CONV_CH, GDN_HEADS, GDN_HEADS, GDN_V,
              D_MODEL, D_MODEL)
W_IN_OFFS = tuple(sum(W_IN_SIZES[:i]) for i in range(len(W_IN_SIZES) + 1))
PREP_STARTS = (0, COL_GATE_A, COL_GATE_B)
PREP_SHIFTS = (0, W_IN_OFFS[4] - COL_GATE_A, W_IN_OFFS[8] - COL_GATE_B)
PREP_ROWS = 1024
PREP_HALO = 64


def _wprep_kernel(a_ref, nxt_ref, o_ref):
    j = pl.program_id(0)
    bounds = tuple(s // PREP_ROWS for s in PREP_STARTS) + (N_MAIN // PREP_ROWS,)
    for run, shift in enumerate(PREP_SHIFTS):
        @pl.when((j >= bounds[run]) & (j < bounds[run + 1]))
        def _(shift=shift):
            if shift == 0:
                o_ref[...] = a_ref[...].astype(BF16)
            else:
                x = jnp.concatenate([a_ref[...], nxt_ref[...]], axis=0)
                o_ref[...] = x[shift:shift + PREP_ROWS, :].astype(BF16)


def _prep_w_main(wt):
    d = wt.shape[1]
    assert all(s % PREP_ROWS == 0 for s in PREP_STARTS)
    assert max(PREP_SHIFTS) <= PREP_HALO and all(s % 16 == 0 for s in PREP_SHIFTS)
    return pl.pallas_call(
        _wprep_kernel,
        grid=(N_MAIN // PREP_ROWS,),
        in_specs=[
            pl.BlockSpec((PREP_ROWS, d), lambda j: (j, 0)),
            pl.BlockSpec((PREP_HALO, d), lambda j: ((j + 1) * (PREP_ROWS // PREP_HALO), 0)),
        ],
        out_specs=pl.BlockSpec((PREP_ROWS, d), lambda j: (j, 0)),
        out_shape=jax.ShapeDtypeStruct((N_MAIN, d), BF16),
        compiler_params=pltpu.CompilerParams(
            dimension_semantics=("arbitrary",), vmem_limit_bytes=VMEM_LIMIT),
        name="w_prep",
    )(wt, wt)


def _dot_nt(a, b):
    return lax.dot_general(a, b, (((1,), (1,)), ((), ())), preferred_element_type=F32)


IN_PROJ_TM = 1024
IN_PROJ_TN = 1536


def _in_proj_kernel(x_ref, g_ref, w_ref, ws_ref, o_ref, os_ref, h_ref):
    @pl.when(pl.program_id(1) == 0)
    def _():
        x = x_ref[...]
        inv = lax.rsqrt(jnp.mean(x * x, axis=-1, keepdims=True) + EPS)
        h = (x * inv * g_ref[...]).astype(BF16)
        h_ref[...] = h
        os_ref[...] = _dot_nt(h, ws_ref[...].astype(BF16))

    o_ref[...] = _dot_nt(h_ref[...], w_ref[...])


def _in_proj(x2d, ln_g, w_main, w_small):
    n_tok, d = x2d.shape
    tm = min(IN_PROJ_TM, n_tok)
    tn = IN_PROJ_TN
    assert N_MAIN % tn == 0
    return pl.pallas_call(
        _in_proj_kernel,
        grid=(n_tok // tm, N_MAIN // tn),
        in_specs=[
            pl.BlockSpec((tm, d), lambda i, j: (i, 0)),
            pl.BlockSpec((1, d), lambda i, j: (0, 0)),
            pl.BlockSpec((tn, d), lambda i, j: (j, 0)),
            pl.BlockSpec((LANES, d), lambda i, j: (0, 0)),
        ],
        out_specs=[
            pl.BlockSpec((tm, tn), lambda i, j: (i, j)),
            pl.BlockSpec((tm, LANES), lambda i, j: (i, 0)),
        ],
        out_shape=[jax.ShapeDtypeStruct((n_tok, N_MAIN), F32),
                   jax.ShapeDtypeStruct((n_tok, LANES), F32)],
        scratch_shapes=[pltpu.VMEM((tm, d), BF16)],
        compiler_params=pltpu.CompilerParams(
            dimension_semantics=("arbitrary", "arbitrary"), vmem_limit_bytes=VMEM_LIMIT),
        name="in_proj",
    )(x2d, ln_g, w_main, w_small)


GLA_PROMPT_ROWS = 512
GLA_SAMPLE_ROWS = 32
GLA_ATT_ROWS = 256


def _gla_kernel(*refs, seq_len, rows, carry):
    n_in = 8 if carry else 9
    q_ref, k_ref, v_ref, gate_ref, sm_ref, wal_ref, bal_ref, ng_ref = refs[:8]
    s0_ref = None if carry else refs[8]
    o_ref, so_ref = refs[n_in:n_in + 2]
    qd_ref, ke_ref, ebl_ref, oacc_ref = refs[n_in + 2:n_in + 6]
    s_ref = refs[n_in + 6] if carry else None
    r, L = rows, seq_len
    nsub = r // L
    g = min(r, CHUNK)

    if carry:
        @pl.when(pl.program_id(1) == 0)
        def _():
            s_ref[...] = jnp.zeros_like(s_ref)

    ga = min(r, GLA_ATT_ROWS)
    ri = lax.broadcasted_iota(jnp.int32, (ga, ga), 0)
    ci = lax.broadcasted_iota(jnp.int32, (ga, ga), 1)
    incl = ((ri // L) == (ci // L)) & (ri >= ci)
    gi = lax.broadcasted_iota(jnp.int32, (g, 2 * LANES), 0)
    gl = lax.broadcasted_iota(jnp.int32, (g, 2 * LANES), 1)
    gc = gl % g
    tri3 = ((gl < 3 * g) & ((gi // L) == (gc // L)) & (gi >= gc)).astype(BF16)

    def seg_cumsum(x):
        outs = []
        for gq in range(r // g):
            xg = x[gq * g:(gq + 1) * g]
            hi, r1 = _split_f32(xg)
            mid, r2 = _split_f32(r1)
            parts = [hi, mid, r2.astype(BF16)]
            if 3 * g < 2 * LANES:
                parts.append(jnp.zeros((2 * LANES - 3 * g, xg.shape[1]), BF16))
            outs.append(_dot(tri3, jnp.concatenate(parts, axis=0)))
        return outs[0] if len(outs) == 1 else jnp.concatenate(outs, axis=0)

    for h in range(GLA_HEADS):
        lk = slice(h * GLA_DK, (h + 1) * GLA_DK)
        lv = slice(h * GLA_DV, (h + 1) * GLA_DV)
        q = q_ref[:, lk] * (GLA_DK ** -0.5)
        k = k_ref[:, lk]
        lr = sm_ref[:, SM_LR:SM_LR + GLA_RANK]
        z = _bdot(lr, wal_ref[:, lk]) + bal_ref[:, lk]
        la = -_softplus(-z) / GLA_TAU
        b = seg_cumsum(la)
        lasts = [jnp.broadcast_to(b[(qi + 1) * L - 1:(qi + 1) * L, :], (L, GLA_DK))
                 for qi in range(nsub)]
        b_last = lasts[0] if nsub == 1 else jnp.concatenate(lasts, axis=0)
        q_d = q * jnp.exp(b)
        k_d = k * jnp.exp(-b)
        for gq in range(r // ga):
            gs = slice(gq * ga, (gq + 1) * ga)
            att = jnp.where(incl, _bdot_nt(q_d[gs], k_d[gs]), 0.0)
            oacc_ref[h, gs, :] = _bdot(att, v_ref[gs, lv])
        qd_ref[h] = q_d
        ke_ref[h] = k * jnp.exp(b_last - b)
        ebl_ref[h] = jnp.exp(b_last)

    for qi in range(nsub):
        rs = slice(qi * L, (qi + 1) * L)
        for h in range(GLA_HEADS):
            lv = slice(h * GLA_DV, (h + 1) * GLA_DV)
            s = s_ref[h] if carry else s0_ref[qi, h]
            oacc_ref[h, rs, :] = oacc_ref[h, rs, :] + _bdot(qd_ref[h, rs, :], s)
            kv = _bdot_tn(ke_ref[h, rs, :], v_ref[rs, lv])
            eb = ebl_ref[h, qi * L:qi * L + 1, :]
            col = jnp.transpose(jnp.broadcast_to(eb, (LANES, GLA_DK)))
            s_new = s * jnp.concatenate([col] * (GLA_DV // LANES), axis=1) + kv
            if carry:
                s_ref[h] = s_new
            else:
                so_ref[qi, h] = s_new

    for h in range(GLA_HEADS):
        lv = slice(h * GLA_DV, (h + 1) * GLA_DV)
        o = oacc_ref[h]
        inv = lax.rsqrt(jnp.mean(o * o, axis=-1, keepdims=True) + EPS)
        on = o * inv * ng_ref[...]
        o_ref[:, lv] = (on * _silu(gate_ref[:, lv])).astype(o_ref.dtype)

    if carry:
        @pl.when(pl.program_id(1) == pl.num_programs(1) - 1)
        def _():
            so_ref[...] = s_ref[...]


def _gla(proj, small, w_alpha2, b_alpha, norm_g, s0, *, bsz, t, rows):
    carry = s0 is None
    r = rows
    n_tok = proj.shape[0]
    if carry:
        seq_len = min(CHUNK, t)
        assert t % r == 0 and r % seq_len == 0
        grid = (bsz, t // r)
        row = lambda b, n: b * (t // r) + n
    else:
        assert t <= CHUNK and r % t == 0 and bsz % (r // t) == 0
        seq_len = t
        grid = (bsz // (r // t), 1)
        row = lambda b, n: b
    nsub = r // seq_len
    qb, kb = COL_QA // GLA_QK, COL_KA // GLA_QK
    vb, gb = COL_VA // GLA_V, COL_GATE_A // GLA_V
    in_specs = [
        pl.BlockSpec((r, GLA_QK), lambda b, n: (row(b, n), qb)),
        pl.BlockSpec((r, GLA_QK), lambda b, n: (row(b, n), kb)),
        pl.BlockSpec((r, GLA_V), lambda b, n: (row(b, n), vb)),
        pl.BlockSpec((r, GLA_V), lambda b, n: (row(b, n), gb)),
        pl.BlockSpec((r, LANES), lambda b, n: (row(b, n), 0)),
        pl.BlockSpec((GLA_RANK, GLA_QK), lambda b, n: (0, 0)),
        pl.BlockSpec((1, GLA_QK), lambda b, n: (0, 0)),
        pl.BlockSpec((1, GLA_DV), lambda b, n: (0, 0)),
    ]
    args = [proj, proj, proj, proj, small, w_alpha2, b_alpha, norm_g]
    scratch = [pltpu.VMEM((GLA_HEADS, r, GLA_DK), F32)] * 3 + [
        pltpu.VMEM((GLA_HEADS, r, GLA_DV), F32)]
    if carry:
        state = pl.BlockSpec((None, GLA_HEADS, GLA_DK, GLA_DV), lambda b, n: (b, 0, 0, 0))
        scratch += [pltpu.VMEM((GLA_HEADS, GLA_DK, GLA_DV), F32)]
    else:
        state = pl.BlockSpec((nsub, GLA_HEADS, GLA_DK, GLA_DV), lambda b, n: (b, 0, 0, 0))
        in_specs += [state]
        args += [s0]
    return pl.pallas_call(
        functools.partial(_gla_kernel, seq_len=seq_len, rows=r, carry=carry),
        grid=grid,
        in_specs=in_specs,
        out_specs=[pl.BlockSpec((r, GLA_V), lambda b, n: (row(b, n), 0)), state],
        out_shape=[jax.ShapeDtypeStruct((n_tok, GLA_V), BF16),
                   jax.ShapeDtypeStruct((bsz, GLA_HEADS, GLA_DK, GLA_DV), F32)],
        scratch_shapes=scratch,
        compiler_params=pltpu.CompilerParams(
            dimension_semantics=("arbitrary", "arbitrary"), vmem_limit_bytes=VMEM_LIMIT),
        name="gla",
    )(*args)


GDN_ROWS = 64
HALF = LANES // 2
GDN_GROUP_CARRY = GDN_HEADS
GDN_GROUP_SEQS = 1


def _split_f32(x):
    hi = x.astype(BF16)
    return hi, x - hi.astype(F32)


def _lhs3(x, low_half):
    hi, lo = _split_f32(x)
    t0 = jnp.where(low_half, x, lo).astype(BF16)
    t1 = jnp.where(low_half, hi, jnp.zeros_like(hi))
    return jnp.concatenate([t0, t1], axis=1)


def _rhs3(hi, lo):
    return jnp.concatenate([hi, hi, lo, lo], axis=0)


def _gdn_kernel(*refs, seq_len, carry, group):
    r = GDN_ROWS
    L = seq_len
    nseq = r // L
    tail = CONV_W - 1
    pad = 8
    n_in = 11 if carry else 15
    (q_ref, k_ref, v_ref, gate_ref, sm_ref, alog_ref, dtb_ref, ng_ref, cwq_ref, cwk_ref,
     cwv_ref) = refs[:11]
    cq0_ref, ck0_ref, cv0_ref, s0_ref = (None,) * 4 if carry else refs[11:15]
    o_ref, so_ref = refs[n_in:n_in + 2]
    (nb_ref, tb_ref, nlhs_ref, p_ref, att_ref, rhs_ref, x0_ref, qg_ref, ke_ref, u_ref, w_ref,
     os_ref, eq_ref, ek_ref, ev_ref, cq_ref, ck_ref, cv_ref) = refs[n_in + 2:n_in + 20]
    ri = lax.broadcasted_iota(jnp.int32, (r, LANES), 0)
    li = lax.broadcasted_iota(jnp.int32, (r, LANES), 1)
    cj = li % HALF
    low_half = li < HALF
    if carry:
        (s_ref,) = refs[n_in + 20:]
        hq_ref = hk_ref = hv_ref = None

        @pl.when(pl.program_id(1) == 0)
        def _():
            s_ref[...] = jnp.zeros_like(s_ref)
            for e_ref in (eq_ref, ek_ref, ev_ref):
                e_ref[0:pad, :] = jnp.zeros((pad, GDN_QK), F32)
    else:
        s_ref = None
        hq_ref, hk_ref, hv_ref = refs[n_in + 20:]

        @pl.when(pl.program_id(0) == 0)
        def _():
            for e_ref in (eq_ref, ek_ref, ev_ref, hq_ref, hk_ref, hv_ref):
                e_ref[...] = jnp.zeros_like(e_ref)

    row_t = lax.broadcasted_iota(jnp.int32, (r, 1), 0) % L

    def conv(e_ref, x_ref, w_ref, out_ref, h_ref, c0_ref):
        e_ref[pad:pad + r, :] = x_ref[...]
        if not carry:
            for qi in range(nseq):
                h_ref[qi * L + pad - tail:qi * L + pad, :] = c0_ref[qi]
        acc = None
        for i in range(CONV_W):
            sh = tail - i
            xs = e_ref[pad - sh:pad - sh + r, :]
            if not carry and sh > 0:
                xs = jnp.where(row_t >= sh, xs, h_ref[pad - sh:pad - sh + r, :])
            term = xs * w_ref[i:i + 1, :]
            acc = term if acc is None else acc + term
        if carry:
            e_ref[pad - tail:pad, :] = x_ref[r - tail:r, :]
        out_ref[...] = _silu(acc)

    conv(eq_ref, q_ref, cwq_ref, cq_ref, hq_ref, cq0_ref)
    conv(ek_ref, k_ref, cwk_ref, ck_ref, hk_ref, ck0_ref)
    conv(ev_ref, v_ref, cwv_ref, cv_ref, hv_ref, cv0_ref)
    sm = sm_ref[...]
    same = (ri // L) == (cj // L)
    incl = same & (ri >= cj)
    strict = same & (ri > cj)
    eye = (ri == cj).astype(F32)

    beta_all = _sigmoid(sm)
    g_all = -jnp.exp(alog_ref[...]) * _softplus(sm + dtb_ref[...])
    g_hi, g_r1 = _split_f32(g_all)
    g_mid, g_r2 = _split_f32(g_r1)
    g_lo = g_r2.astype(BF16)
    tri = incl.astype(BF16)
    tri_lhs = jnp.concatenate([tri, jnp.where(low_half, tri, jnp.zeros_like(tri))], axis=1)
    gh_all = _dot(tri_lhs, jnp.concatenate([g_hi, g_mid, g_lo, g_lo], axis=0))
    gh_t = jnp.transpose(jnp.concatenate([gh_all, gh_all], axis=0))
    glast_all = jnp.concatenate(
        [jnp.broadcast_to(gh_all[(qi + 1) * L - 1:(qi + 1) * L, :], (L, LANES))
         for qi in range(nseq)], axis=0)
    egh_all = jnp.exp(gh_all)
    kend_all = jnp.exp(glast_all - gh_all)

    def chunk_matrices(h):
        ls = slice(h * GDN_DK, (h + 1) * GDN_DK)
        q = cq_ref[:, ls]
        k = ck_ref[:, ls]
        v = cv_ref[:, ls]
        q = q * lax.rsqrt(jnp.sum(q * q, axis=-1, keepdims=True) + EPS) * (GDN_DK ** -0.5)
        k = k * lax.rsqrt(jnp.sum(k * k, axis=-1, keepdims=True) + EPS)
        col = slice(SM_DEC + h, SM_DEC + h + 1)
        beta = beta_all[:, SM_BETA + h:SM_BETA + h + 1]
        gh = gh_all[:, col]
        egh = egh_all[:, col]
        decay = jnp.where(incl, jnp.exp(gh - gh_t[col, :]), 0.0)
        kb = k.astype(BF16)
        qkk = lax.dot_general(jnp.concatenate([q.astype(BF16), kb], axis=0),
                              jnp.concatenate([kb, kb], axis=0),
                              (((1,), (1,)), ((), ())), preferred_element_type=F32)
        att_ref[h] = (qkk[:r] * decay).astype(BF16)
        nm = jnp.where(strict, -(beta * qkk[r:] * decay), 0.0)
        nlhs_ref[h] = _lhs3(nm, low_half)
        nb_ref[h] = nm[:, :HALF].astype(BF16)
        p_ref[h] = (eye + nm)[:, :HALF]
        rhs_ref[h] = jnp.concatenate([v * beta, k * (beta * egh)], axis=1)
        qg_ref[h] = q * egh
        ke_ref[h] = k * kend_all[:, col]

    def square_power(h):
        n_pow = nb_ref[h]
        nb_ref[h] = _dot(n_pow, n_pow).astype(BF16)

    def extend_inverse(h):
        pm = p_ref[h]
        p_ref[h] = pm + _dot(pm.astype(BF16), nb_ref[h])

    def first_solve(h):
        tb = p_ref[h].astype(BF16)
        tb_ref[h] = tb
        x0_ref[h] = _dot(tb, rhs_ref[h].astype(BF16))

    def residual(h):
        x0 = x0_ref[h]
        x_hi, x_lo = _split_f32(x0)
        rhs_ref[h] = rhs_ref[h] - x0 + _dot(nlhs_ref[h], _rhs3(x_hi, x_lo.astype(BF16)))

    def refine(h):
        sol = x0_ref[h] + _dot(tb_ref[h], rhs_ref[h].astype(BF16))
        u_ref[h] = sol[:, :GDN_DV]
        w_ref[h] = sol[:, GDN_DV:]

    def state(qi, h):
        return s_ref[h] if carry else s0_ref[qi, h]

    def state_products(h):
        for qi in range(nseq):
            rs = slice(qi * L, (qi + 1) * L)
            lhs = jnp.concatenate([w_ref[h, rs, :], qg_ref[h, rs, :]], axis=0).astype(BF16)
            res = _dot(lhs, state(qi, h).astype(BF16))
            u_ref[h, rs, :] = u_ref[h, rs, :] - res[:L]
            os_ref[h, rs, :] = res[L:]

    def outputs(h):
        ls = slice(h * GDN_DV, (h + 1) * GDN_DV)
        v_new = u_ref[h]
        o = os_ref[h] + _dot(att_ref[h][:, :HALF], v_new.astype(BF16))
        inv = lax.rsqrt(jnp.mean(o * o, axis=-1, keepdims=True) + EPS)
        on = o * inv * ng_ref[...]
        o_ref[:, ls] = (on * _silu(gate_ref[:, ls])).astype(o_ref.dtype)
        for qi in range(nseq):
            rs = slice(qi * L, (qi + 1) * L)
            kv = lax.dot_general(ke_ref[h, rs, :].astype(BF16), v_new[rs].astype(BF16),
                                 (((0,), (0,)), ((), ())), preferred_element_type=F32)
            eg = egh_all[(qi + 1) * L - 1:(qi + 1) * L, SM_DEC + h:SM_DEC + h + 1]
            s_new = state(qi, h) * eg + kv
            if carry:
                s_ref[h] = s_new
            else:
                so_ref[qi, h] = s_new

    stages = [chunk_matrices]
    p = 2
    while p < L:
        stages += [square_power, extend_inverse]
        p *= 2
    stages += [first_solve, residual, refine, state_products, outputs]
    ngroups = GDN_HEADS // group
    for t in range(ngroups + len(stages) - 1):
        for k, stage in enumerate(stages):
            if 0 <= t - k < ngroups:
                for h in range((t - k) * group, (t - k + 1) * group):
                    stage(h)

    if carry:
        @pl.when(pl.program_id(1) == pl.num_programs(1) - 1)
        def _():
            so_ref[...] = s_ref[...]


def _gdn(proj, small, conv_w, alog_pad, dtb_pad, norm_g, conv0, s0, *, bsz, t):
    carry = s0 is None
    r = GDN_ROWS
    n_tok = proj.shape[0]
    tail = CONV_W - 1
    if carry:
        assert t % r == 0
        seq_len, nseq, grid = r, 1, (bsz, t // r)
        row = lambda b, n: b * (t // r) + n
    else:
        assert r % t == 0 and bsz % (r // t) == 0
        seq_len, nseq, grid = t, r // t, (bsz // (r // t), 1)
        row = lambda b, n: b
    qb = COL_QKV_B // GDN_QK
    gb = COL_GATE_B // GDN_V
    rowspec = lambda cb: pl.BlockSpec((r, GDN_QK), lambda b, n: (row(b, n), cb))
    cwspec = lambda cb: pl.BlockSpec((CONV_W, GDN_QK), lambda b, n: (0, cb))
    vec = pl.BlockSpec((1, LANES), lambda b, n: (0, 0))
    in_specs = [rowspec(qb), rowspec(qb + 1), rowspec(qb + 2), rowspec(gb),
                pl.BlockSpec((r, LANES), lambda b, n: (row(b, n), 0)), vec, vec, vec,
                cwspec(0), cwspec(1), cwspec(2)]
    args = [proj, proj, proj, proj, small, alog_pad, dtb_pad, norm_g, conv_w, conv_w, conv_w]
    hshape = (GDN_HEADS, r, LANES)
    sq = (GDN_HEADS, r, r)
    wide = (GDN_HEADS, r, 2 * LANES)
    scratch = ([pltpu.VMEM(sq, BF16)] * 2 + [pltpu.VMEM(wide, BF16)] + [pltpu.VMEM(sq, F32)]
               + [pltpu.VMEM(hshape, BF16)]
               + [pltpu.VMEM(wide, F32)] * 2
               + [pltpu.VMEM(hshape, F32)] * 5
               + [pltpu.VMEM((8 + r, GDN_QK), F32)] * 3 + [pltpu.VMEM((r, GDN_QK), F32)] * 3)
    if carry:
        state = pl.BlockSpec((None, GDN_HEADS, GDN_DK, GDN_DV), lambda b, n: (b, 0, 0, 0))
        scratch += [pltpu.VMEM((GDN_HEADS, GDN_DK, GDN_DV), F32)]
    else:
        state = pl.BlockSpec((nseq, GDN_HEADS, GDN_DK, GDN_DV), lambda b, n: (b, 0, 0, 0))
        c0spec = lambda cb: pl.BlockSpec((nseq, tail, GDN_QK), lambda b, n: (b, 0, cb))
        in_specs += [c0spec(0), c0spec(1), c0spec(2), state]
        args += [conv0, conv0, conv0, s0]
        scratch += [pltpu.VMEM((8 + r, GDN_QK), F32)] * 3
    return pl.pallas_call(
        functools.partial(_gdn_kernel, seq_len=seq_len, carry=carry,
                          group=GDN_GROUP_CARRY if carry else GDN_GROUP_SEQS),
        grid=grid,
        in_specs=in_specs,
        out_specs=[pl.BlockSpec((r, GDN_V), lambda b, n: (row(b, n), 0)), state],
        out_shape=[jax.ShapeDtypeStruct((n_tok, GDN_V), BF16),
                   jax.ShapeDtypeStruct((bsz, GDN_HEADS, GDN_DK, GDN_DV), F32)],
        scratch_shapes=scratch,
        compiler_params=pltpu.CompilerParams(
            dimension_semantics=("arbitrary", "arbitrary"), vmem_limit_bytes=VMEM_LIMIT),
        name="gdn",
    )(*args)


def _merge_kernel(oa_ref, ob_ref, wa_ref, wb_ref, ma_ref, mb_ref, o_ref):
    ya = jnp.dot(oa_ref[...].astype(BF16), wa_ref[...], preferred_element_type=F32)
    yb = jnp.dot(ob_ref[...].astype(BF16), wb_ref[...], preferred_element_type=F32)
    merged = _sigmoid(ma_ref[...]) * ya + _sigmoid(mb_ref[...]) * yb
    o_ref[...] = merged.astype(o_ref.dtype)


def _merge(o_a, o_b, w_br_a, w_br_b, proj):
    n_tok = o_a.shape[0]
    tm = min(1024, n_tok)
    tn = 512
    ma, mb = COL_MA // tn, COL_MB // tn
    return pl.pallas_call(
        _merge_kernel,
        grid=(n_tok // tm, D_MODEL // tn),
        in_specs=[
            pl.BlockSpec((tm, GLA_V), lambda i, j: (i, 0)),
            pl.BlockSpec((tm, GDN_V), lambda i, j: (i, 0)),
            pl.BlockSpec((GLA_V, tn), lambda i, j: (0, j)),
            pl.BlockSpec((GDN_V, tn), lambda i, j: (0, j)),
            pl.BlockSpec((tm, tn), lambda i, j: (i, ma + j)),
            pl.BlockSpec((tm, tn), lambda i, j: (i, mb + j)),
        ],
        out_specs=pl.BlockSpec((tm, tn), lambda i, j: (i, j)),
        out_shape=jax.ShapeDtypeStruct((n_tok, D_MODEL), BF16),
        compiler_params=pltpu.CompilerParams(
            dimension_semantics=("arbitrary", "arbitrary"), vmem_limit_bytes=VMEM_LIMIT),
        name="merge",
    )(o_a, o_b, w_br_a, w_br_b, proj, proj)


def _out_kernel(m_ref, w_ref, x_ref, g_ref, y_ref):
    out = x_ref[...] + jnp.dot(m_ref[...], w_ref[...], preferred_element_type=F32)
    inv = lax.rsqrt(jnp.mean(out * out, axis=-1, keepdims=True) + EPS)
    y_ref[...] = out * inv * g_ref[...]


def _out_proj(merged, w_out, x2d, final_g):
    n_tok = merged.shape[0]
    tm = min(512, n_tok)
    return pl.pallas_call(
        _out_kernel,
        grid=(n_tok // tm,),
        in_specs=[
            pl.BlockSpec((tm, D_MODEL), lambda i: (i, 0)),
            pl.BlockSpec((D_MODEL, D_MODEL), lambda i: (0, 0)),
            pl.BlockSpec((tm, D_MODEL), lambda i: (i, 0)),
            pl.BlockSpec((1, D_MODEL), lambda i: (0, 0)),
        ],
        out_specs=pl.BlockSpec((tm, D_MODEL), lambda i: (i, 0)),
        out_shape=jax.ShapeDtypeStruct((n_tok, D_MODEL), F32),
        compiler_params=pltpu.CompilerParams(
            dimension_semantics=("arbitrary",), vmem_limit_bytes=VMEM_LIMIT),
        name="out_proj",
    )(merged, w_out, x2d, final_g)


def _layer(x, s_gla, s_gdn, conv_buf, p, *, gla_rows):
    bsz, t, d = x.shape
    x2d = x.reshape(bsz * t, d)
    tail = CONV_W - 1
    proj, small = _in_proj(x2d, p["ln_g"], p["w_main"], p["w_small"])
    conv_new = proj.reshape(bsz, t, N_MAIN)[:, t - tail:, COL_QKV_B:COL_QKV_B + CONV_CH]
    o_a, s_gla_new = _gla(proj, small, p["w_alpha2"], p["b_alpha"], p["gla_norm_g"], s_gla,
                          bsz=bsz, t=t, rows=gla_rows)
    o_b, s_gdn_new = _gdn(proj, small, p["conv_w"], p["alog_pad"], p["dtb_pad"],
                          p["gdn_norm_g"], conv_buf, s_gdn, bsz=bsz, t=t)
    merged = _merge(o_a, o_b, p["w_br_a"], p["w_br_b"], proj)
    y = _out_proj(merged, p["w_out"], x2d, p["final_g"])
    return y.reshape(bsz, t, d), s_gla_new, s_gdn_new, conv_new


def _prep_params(ln_in_g, w_in, w_alpha2, b_alpha, conv_w, a_log, dt_bias, gla_norm_g,
                 gdn_norm_g, w_br_a, w_br_b, w_out, final_norm_g):
    wt = jnp.transpose(w_in)
    w_main = _prep_w_main(wt)
    n_small = GLA_RANK + 2 * GDN_HEADS
    w_small = jnp.concatenate(
        [wt[W_IN_OFFS[3]:W_IN_OFFS[4]], wt[W_IN_OFFS[6]:W_IN_OFFS[8]],
         jnp.zeros((LANES - n_small, D_MODEL), F32)], axis=0)
    lane_pad = lambda v: jnp.zeros((1, LANES), F32).at[0, SM_DEC:SM_DEC + GDN_HEADS].set(v)
    return {
        "ln_g": ln_in_g.reshape(1, D_MODEL),
        "w_main": w_main,
        "w_small": w_small,
        "w_alpha2": w_alpha2,
        "b_alpha": b_alpha.reshape(1, GLA_QK),
        "conv_w": conv_w,
        "alog_pad": lane_pad(a_log),
        "dtb_pad": lane_pad(dt_bias),
        "gla_norm_g": gla_norm_g.reshape(1, GLA_DV),
        "gdn_norm_g": gdn_norm_g.reshape(1, GDN_DV),
        "w_br_a": w_br_a.astype(BF16),
        "w_br_b": w_br_b.astype(BF16),
        "w_out": w_out.astype(BF16),
        "final_g": final_norm_g.reshape(1, D_MODEL),
    }


def kernel(x_prompt, x_sample, state_gla, state_gdn, state_conv, ln_in_g, w_in, w_alpha2,
           b_alpha, conv_w, a_log, dt_bias, gla_norm_g, gdn_norm_g, w_br_a, w_br_b, w_out,
           final_norm_g):
    assert ln_in_g.shape[0] == 1, "single layer"
    p = _prep_params(ln_in_g[0], w_in[0], w_alpha2[0], b_alpha[0], conv_w[0], a_log[0],
                     dt_bias[0], gla_norm_g[0], gdn_norm_g[0], w_br_a[0], w_br_b[0], w_out[0],
                     final_norm_g)
    yp, gla_p, gdn_p, conv_p = _layer(x_prompt, None, None, None, p, gla_rows=GLA_PROMPT_ROWS)
    ys, gla_s, gdn_s, conv_s = _layer(x_sample, state_gla[0], state_gdn[0], state_conv[0], p,
                                      gla_rows=GLA_SAMPLE_ROWS)
    return (yp, ys, gla_p[None], gdn_p[None], conv_p[None], gla_s[None], gdn_s[None],
            conv_s[None])
```

```python
import functools

import jax
import jax.numpy as jnp
from jax import lax
from jax.experimental import pallas as pl
from jax.experimental.pallas import tpu as pltpu

F32 = jnp.float32
BF16 = jnp.bfloat16

D_MODEL = 2048
GLA_HEADS = 4
GLA_DK = 256
GLA_DV = 512
GLA_RANK = 16
GLA_TAU = 16.0
GDN_HEADS = 16
GDN_DK = 128
GDN_DV = 128
CONV_W = 4
CHUNK = 64
EPS = 1e-6

GLA_QK = GLA_HEADS * GLA_DK
GLA_V = GLA_HEADS * GLA_DV
GDN_QK = GDN_HEADS * GDN_DK
GDN_V = GDN_HEADS * GDN_DV
CONV_CH = 2 * GDN_QK + GDN_V

LANES = 128
COL_QA = 0
COL_KA = COL_QA + GLA_QK
COL_VA = COL_KA + GLA_QK
COL_GATE_A = COL_VA + GLA_V
COL_QKV_B = COL_GATE_A + GLA_V
COL_GATE_B = COL_QKV_B + CONV_CH
COL_MA = COL_GATE_B + GDN_V
COL_MB = COL_MA + D_MODEL
N_MAIN = COL_MB + D_MODEL
SM_LR = 0
SM_BETA = GLA_RANK
SM_DEC = SM_BETA + GDN_HEADS

VMEM_LIMIT = 56 * 1024 * 1024


def _bdot(a, b):
    return jnp.dot(a.astype(BF16), b.astype(BF16), preferred_element_type=F32)


def _bdot_nt(a, b):
    return lax.dot_general(a.astype(BF16), b.astype(BF16), (((1,), (1,)), ((), ())),
                           preferred_element_type=F32)


def _bdot_tn(a, b):
    return lax.dot_general(a.astype(BF16), b.astype(BF16), (((0,), (0,)), ((), ())),
                           preferred_element_type=F32)


def _dot(a, b):
    return jnp.dot(a, b, preferred_element_type=F32)


def _sigmoid(x):
    return 1.0 / (1.0 + jnp.exp(-x))


def _silu(x):
    return x * _sigmoid(x)


def _softplus(x):
    return jnp.maximum(x, 0.0) + jnp.log(1.0 + jnp.exp(-jnp.abs(x)))


W_IN_SIZES = (GLA_QK, GLA_QK, GLA_V, GLA_RANK, GLA_V, CONV_CH, GDN_HEADS, GDN_HEADS, GDN_V,
              D_MODEL, D_MODEL)
W_IN_OFFS = tuple(sum(W_IN_SIZES[:i]) for i in range(len(W_IN_SIZES) + 1))
PREP_STARTS = (0, COL_GATE_A, COL_GATE_B)
PREP_SHIFTS = (0, W_IN_OFFS[4] - COL_GATE_A, W_IN_OFFS[8] - COL_GATE_B)
PREP_ROWS = 1024
PREP_HALO = 64


def _wprep_kernel(a_ref, nxt_ref, o_ref):
    j = pl.program_id(0)
    bounds = tuple(s // PREP_ROWS for s in PREP_STARTS) + (N_MAIN // PREP_ROWS,)
    for run, shift in enumerate(PREP_SHIFTS):
        @pl.when((j >= bounds[run]) & (j < bounds[run + 1]))
        def _(shift=shift):
            if shift == 0:
                o_ref[...] = a_ref[...].astype(BF16)
            else:
                x = jnp.concatenate([a_ref[...], nxt_ref[...]], axis=0)
                o_ref[...] = x[shift:shift + PREP_ROWS, :].astype(BF16)


def _prep_w_main(wt):
    d = wt.shape[1]
    assert all(s % PREP_ROWS == 0 for s in PREP_STARTS)
    assert max(PREP_SHIFTS) <= PREP_HALO and all(s % 16 == 0 for s in PREP_SHIFTS)
    return pl.pallas_call(
        _wprep_kernel,
        grid=(N_MAIN // PREP_ROWS,),
        in_specs=[
            pl.BlockSpec((PREP_ROWS, d), lambda j: (j, 0)),
            pl.BlockSpec((PREP_HALO, d), lambda j: ((j + 1) * (PREP_ROWS // PREP_HALO), 0)),
        ],
        out_specs=pl.BlockSpec((PREP_ROWS, d), lambda j: (j, 0)),
        out_shape=jax.ShapeDtypeStruct((N_MAIN, d), BF16),
        compiler_params=pltpu.CompilerParams(
            dimension_semantics=("arbitrary",), vmem_limit_bytes=VMEM_LIMIT),
        name="w_prep",
    )(wt, wt)


def _dot_nt(a, b):
    return lax.dot_general(a, b, (((1,), (1,)), ((), ())), preferred_element_type=F32)


IN_PROJ_TM = 1024
IN_PROJ_TN = 1536


def _in_proj_kernel(x_ref, g_ref, w_ref, ws_ref, o_ref, os_ref, h_ref):
    @pl.when(pl.program_id(1) == 0)
    def _():
        x = x_ref[...]
        inv = lax.rsqrt(jnp.mean(x * x, axis=-1, keepdims=True) + EPS)
        h = (x * inv * g_ref[...]).astype(BF16)
        h_ref[...] = h
        os_ref[...] = _dot_nt(h, ws_ref[...].astype(BF16))

    o_ref[...] = _dot_nt(h_ref[...], w_ref[...])


def _in_proj(x2d, ln_g, w_main, w_small):
    n_tok, d = x2d.shape
    tm = min(IN_PROJ_TM, n_tok)
    tn = IN_PROJ_TN
    assert N_MAIN % tn == 0
    return pl.pallas_call(
        _in_proj_kernel,
        grid=(n_tok // tm, N_MAIN // tn),
        in_specs=[
            pl.BlockSpec((tm, d), lambda i, j: (i, 0)),
            pl.BlockSpec((1, d), lambda i, j: (0, 0)),
            pl.BlockSpec((tn, d), lambda i, j: (j, 0)),
            pl.BlockSpec((LANES, d), lambda i, j: (0, 0)),
        ],
        out_specs=[
            pl.BlockSpec((tm, tn), lambda i, j: (i, j)),
            pl.BlockSpec((tm, LANES), lambda i, j: (i, 0)),
        ],
        out_shape=[jax.ShapeDtypeStruct((n_tok, N_MAIN), F32),
                   jax.ShapeDtypeStruct((n_tok, LANES), F32)],
        scratch_shapes=[pltpu.VMEM((tm, d), BF16)],
        compiler_params=pltpu.CompilerParams(
            dimension_semantics=("arbitrary", "arbitrary"), vmem_limit_bytes=VMEM_LIMIT),
        name="in_proj",
    )(x2d, ln_g, w_main, w_small)


GLA_PROMPT_ROWS = 512
GLA_SAMPLE_ROWS = 32
GLA_ATT_ROWS = 256


def _gla_kernel(*refs, seq_len, rows, carry):
    n_in = 8 if carry else 9
    q_ref, k_ref, v_ref, gate_ref, sm_ref, wal_ref, bal_ref, ng_ref = refs[:8]
    s0_ref = None if carry else refs[8]
    o_ref, so_ref = refs[n_in:n_in + 2]
    qd_ref, ke_ref, ebl_ref, oacc_ref = refs[n_in + 2:n_in + 6]
    s_ref = refs[n_in + 6] if carry else None
    r, L = rows, seq_len
    nsub = r // L
    g = min(r, CHUNK)

    if carry:
        @pl.when(pl.program_id(1) == 0)
        def _():
            s_ref[...] = jnp.zeros_like(s_ref)

    ga = min(r, GLA_ATT_ROWS)
    ri = lax.broadcasted_iota(jnp.int32, (ga, ga), 0)
    ci = lax.broadcasted_iota(jnp.int32, (ga, ga), 1)
    incl = ((ri // L) == (ci // L)) & (ri >= ci)
    gi = lax.broadcasted_iota(jnp.int32, (g, 2 * LANES), 0)
    gl = lax.broadcasted_iota(jnp.int32, (g, 2 * LANES), 1)
    gc = gl % g
    tri3 = ((gl < 3 * g) & ((gi // L) == (gc // L)) & (gi >= gc)).astype(BF16)

    def seg_cumsum(x):
        outs = []
        for gq in range(r // g):
            xg = x[gq * g:(gq + 1) * g]
            hi, r1 = _split_f32(xg)
            mid, r2 = _split_f32(r1)
            parts = [hi, mid, r2.astype(BF16)]
            if 3 * g < 2 * LANES:
                parts.append(jnp.zeros((2 * LANES - 3 * g, xg.shape[1]), BF16))
            outs.append(_dot(tri3, jnp.concatenate(parts, axis=0)))
        return outs[0] if len(outs) == 1 else jnp.concatenate(outs, axis=0)

    for h in range(GLA_HEADS):
        lk = slice(h * GLA_DK, (h + 1) * GLA_DK)
        lv = slice(h * GLA_DV, (h + 1) * GLA_DV)
        q = q_ref[:, lk] * (GLA_DK ** -0.5)
        k = k_ref[:, lk]
        lr = sm_ref[:, SM_LR:SM_LR + GLA_RANK]
        z = _bdot(lr, wal_ref[:, lk]) + bal_ref[:, lk]
        la = -_softplus(-z) / GLA_TAU
        b = seg_cumsum(la)
        lasts = [jnp.broadcast_to(b[(qi + 1) * L - 1:(qi + 1) * L, :], (L, GLA_DK))
                 for qi in range(nsub)]
        b_last = lasts[0] if nsub == 1 else jnp.concatenate(lasts, axis=0)
        q_d = q * jnp.exp(b)
        k_d = k * jnp.exp(-b)
        for gq in range(r // ga):
            gs = slice(gq * ga, (gq + 1) * ga)
            att = jnp.where(incl, _bdot_nt(q_d[gs], k_d[gs]), 0.0)
            oacc_ref[h, gs, :] = _bdot(att, v_ref[gs, lv])
        qd_ref[h] = q_d
        ke_ref[h] = k * jnp.exp(b_last - b)
        ebl_ref[h] = jnp.exp(b_last)

    for qi in range(nsub):
        rs = slice(qi * L, (qi + 1) * L)
        for h in range(GLA_HEADS):
            lv = slice(h * GLA_DV, (h + 1) * GLA_DV)
            s = s_ref[h] if carry else s0_ref[qi, h]
            oacc_ref[h, rs, :] = oacc_ref[h, rs, :] + _bdot(qd_ref[h, rs, :], s)
            kv = _bdot_tn(ke_ref[h, rs, :], v_ref[rs, lv])
            eb = ebl_ref[h, qi * L:qi * L + 1, :]
            col = jnp.transpose(jnp.broadcast_to(eb, (LANES, GLA_DK)))
            s_new = s * jnp.concatenate([col] * (GLA_DV // LANES), axis=1) + kv
            if carry:
                s_ref[h] = s_new
            else:
                so_ref[qi, h] = s_new

    for h in range(GLA_HEADS):
        lv = slice(h * GLA_DV, (h + 1) * GLA_DV)
        o = oacc_ref[h]
        inv = lax.rsqrt(jnp.mean(o * o, axis=-1, keepdims=True) + EPS)
        on = o * inv * ng_ref[...]
        o_ref[:, lv] = (on * _silu(gate_ref[:, lv])).astype(o_ref.dtype)

    if carry:
        @pl.when(pl.program_id(1) == pl.num_programs(1) - 1)
        def _():
            so_ref[...] = s_ref[...]


def _gla(proj, small, w_alpha2, b_alpha, norm_g, s0, *, bsz, t, rows):
    carry = s0 is None
    r = rows
    n_tok = proj.shape[0]
    if carry:
        seq_len = min(CHUNK, t)
        assert t % r == 0 and r % seq_len == 0
        grid = (bsz, t // r)
        row = lambda b, n: b * (t // r) + n
    else:
        assert t <= CHUNK and r % t == 0 and bsz % (r // t) == 0
        seq_len = t
        grid = (bsz // (r // t), 1)
        row = lambda b, n: b
    nsub = r // seq_len
    qb, kb = COL_QA // GLA_QK, COL_KA // GLA_QK
    vb, gb = COL_VA // GLA_V, COL_GATE_A // GLA_V
    in_specs = [
        pl.BlockSpec((r, GLA_QK), lambda b, n: (row(b, n), qb)),
        pl.BlockSpec((r, GLA_QK), lambda b, n: (row(b, n), kb)),
        pl.BlockSpec((r, GLA_V), lambda b, n: (row(b, n), vb)),
        pl.BlockSpec((r, GLA_V), lambda b, n: (row(b, n), gb)),
        pl.BlockSpec((r, LANES), lambda b, n: (row(b, n), 0)),
        pl.BlockSpec((GLA_RANK, GLA_QK), lambda b, n: (0, 0)),
        pl.BlockSpec((1, GLA_QK), lambda b, n: (0, 0)),
        pl.BlockSpec((1, GLA_DV), lambda b, n: (0, 0)),
    ]
    args = [proj, proj, proj, proj, small, w_alpha2, b_alpha, norm_g]
    scratch = [pltpu.VMEM((GLA_HEADS, r, GLA_DK), F32)] * 3 + [
        pltpu.VMEM((GLA_HEADS, r, GLA_DV), F32)]
    if carry:
        state = pl.BlockSpec((None, GLA_HEADS, GLA_DK, GLA_DV), lambda b, n: (b, 0, 0, 0))
        scratch += [pltpu.VMEM((GLA_HEADS, GLA_DK, GLA_DV), F32)]
    else:
        state = pl.BlockSpec((nsub, GLA_HEADS, GLA_DK, GLA_DV), lambda b, n: (b, 0, 0, 0))
        in_specs += [state]
        args += [s0]
    return pl.pallas_call(
        functools.partial(_gla_kernel, seq_len=seq_len, rows=r, carry=carry),
        grid=grid,
        in_specs=in_specs,
        out_specs=[pl.BlockSpec((r, GLA_V), lambda b, n: (row(b, n), 0)), state],
        out_shape=[jax.ShapeDtypeStruct((n_tok, GLA_V), BF16),
                   jax.ShapeDtypeStruct((bsz, GLA_HEADS, GLA_DK, GLA_DV), F32)],
        scratch_shapes=scratch,
        compiler_params=pltpu.CompilerParams(
            dimension_semantics=("arbitrary", "arbitrary"), vmem_limit_bytes=VMEM_LIMIT),
        name="gla",
    )(*args)


GDN_ROWS = 64
HALF = LANES // 2
GDN_GROUP_CARRY = GDN_HEADS
GDN_GROUP_SEQS = 1


def _split_f32(x):
    hi = x.astype(BF16)
    return hi, x - hi.astype(F32)


def _lhs3(x, low_half):
    hi, lo = _split_f32(x)
    t0 = jnp.where(low_half, x, lo).astype(BF16)
    t1 = jnp.where(low_half, hi, jnp.zeros_like(hi))
    return jnp.concatenate([t0, t1], axis=1)


def _rhs3(hi, lo):
    return jnp.concatenate([hi, hi, lo, lo], axis=0)


def _gdn_kernel(*refs, seq_len, carry, group):
    r = GDN_ROWS
    L = seq_len
    nseq = r // L
    tail = CONV_W - 1
    pad = 8
    n_in = 11 if carry else 15
    (q_ref, k_ref, v_ref, gate_ref, sm_ref, alog_ref, dtb_ref, ng_ref, cwq_ref, cwk_ref,
     cwv_ref) = refs[:11]
    cq0_ref, ck0_ref, cv0_ref, s0_ref = (None,) * 4 if carry else refs[11:15]
    o_ref, so_ref = refs[n_in:n_in + 2]
    (nb_ref, tb_ref, nlhs_ref, p_ref, att_ref, rhs_ref, x0_ref, qg_ref, ke_ref, u_ref, w_ref,
     os_ref, eq_ref, ek_ref, ev_ref, cq_ref, ck_ref, cv_ref) = refs[n_in + 2:n_in + 20]
    ri = lax.broadcasted_iota(jnp.int32, (r, LANES), 0)
    li = lax.broadcasted_iota(jnp.int32, (r, LANES), 1)
    cj = li % HALF
    low_half = li < HALF
    if carry:
        (s_ref,) = refs[n_in + 20:]

        @pl.when(pl.program_id(1) == 0)
        def _():
            s_ref[...] = jnp.zeros_like(s_ref)
            for e_ref in (eq_ref, ek_ref, ev_ref):
                e_ref[:, pad:2 * pad, :] = jnp.zeros((tail, pad, GDN_QK), F32)
    else:
        s_ref = None

    def conv(e_ref, x_ref, w_ref, out_ref, c0_ref):
        acc = None
        for i in range(CONV_W):
            sh = tail - i
            if sh == 0:
                xs = x_ref[...]
            else:
                e_ref[sh - 1, pad + sh:pad + sh + r, :] = x_ref[...]
                if not carry:
                    for qi in range(nseq):
                        e_ref[sh - 1, pad + qi * L:pad + qi * L + sh, :] = c0_ref[qi, tail - sh:tail, :]
                xs = e_ref[sh - 1, pad:pad + r, :]
            term = xs * w_ref[i:i + 1, :]
            acc = term if acc is None else acc + term
        if carry:
            for sh in range(1, tail + 1):
                e_ref[sh - 1, pad:pad + sh, :] = x_ref[r - sh:r, :]
        out_ref[...] = _silu(acc)

    conv(eq_ref, q_ref, cwq_ref, cq_ref, cq0_ref)
    conv(ek_ref, k_ref, cwk_ref, ck_ref, ck0_ref)
    conv(ev_ref, v_ref, cwv_ref, cv_ref, cv0_ref)
    sm = sm_ref[...]
    same = (ri // L) == (cj // L)
    incl = same & (ri >= cj)
    strict = same & (ri > cj)
    eye = (ri == cj).astype(F32)

    beta_all = _sigmoid(sm)
    g_all = -jnp.exp(alog_ref[...]) * _softplus(sm + dtb_ref[...])
    g_hi, g_r1 = _split_f32(g_all)
    g_mid, g_r2 = _split_f32(g_r1)
    g_lo = g_r2.astype(BF16)
    tri = incl.astype(BF16)
    tri_lhs = jnp.concatenate([tri, jnp.where(low_half, tri, jnp.zeros_like(tri))], axis=1)
    gh_all = _dot(tri_lhs, jnp.concatenate([g_hi, g_mid, g_lo, g_lo], axis=0))
    gh_t = jnp.transpose(jnp.concatenate([gh_all, gh_all], axis=0))
    glast_all = jnp.concatenate(
        [jnp.broadcast_to(gh_all[(qi + 1) * L - 1:(qi + 1) * L, :], (L, LANES))
         for qi in range(nseq)], axis=0)
    egh_all = jnp.exp(gh_all)
    kend_all = jnp.exp(glast_all - gh_all)

    def chunk_matrices(h):
        ls = slice(h * GDN_DK, (h + 1) * GDN_DK)
        q = cq_ref[:, ls]
        k = ck_ref[:, ls]
        v = cv_ref[:, ls]
        q = q * lax.rsqrt(jnp.sum(q * q, axis=-1, keepdims=True) + EPS) * (GDN_DK ** -0.5)
        k = k * lax.rsqrt(jnp.sum(k * k, axis=-1, keepdims=True) + EPS)
        col = slice(SM_DEC + h, SM_DEC + h + 1)
        beta = beta_all[:, SM_BETA + h:SM_BETA + h + 1]
        gh = gh_all[:, col]
        egh = egh_all[:, col]
        decay = jnp.where(incl, jnp.exp(gh - gh_t[col, :]), 0.0)
        kb = k.astype(BF16)
        qkk = lax.dot_general(jnp.concatenate([q.astype(BF16), kb], axis=0),
                              jnp.concatenate([kb, kb], axis=0),
                              (((1,), (1,)), ((), ())), preferred_element_type=F32)
        att_ref[h] = (qkk[:r] * decay).astype(BF16)
        nm = jnp.where(strict, -(beta * qkk[r:] * decay), 0.0)
        nlhs_ref[h] = _lhs3(nm, low_half)
        nb_ref[h] = nm[:, :HALF].astype(BF16)
        p_ref[h] = (eye + nm)[:, :HALF]
        rhs_ref[h] = jnp.concatenate([v * beta, k * (beta * egh)], axis=1)
        qg_ref[h] = q * egh
        ke_ref[h] = k * kend_all[:, col]

    def square_power(h):
        n_pow = nb_ref[h]
        nb_ref[h] = _dot(n_pow, n_pow).astype(BF16)

    def extend_inverse(h):
        pm = p_ref[h]
        p_ref[h] = pm + _dot(pm.astype(BF16), nb_ref[h])

    def first_solve(h):
        tb = p_ref[h].astype(BF16)
        tb_ref[h] = tb
        x0_ref[h] = _dot(tb, rhs_ref[h].astype(BF16))

    def residual(h):
        x0 = x0_ref[h]
        x_hi, x_lo = _split_f32(x0)
        rhs_ref[h] = rhs_ref[h] - x0 + _dot(nlhs_ref[h], _rhs3(x_hi, x_lo.astype(BF16)))

    def refine(h):
        sol = x0_ref[h] + _dot(tb_ref[h], rhs_ref[h].astype(BF16))
        u_ref[h] = sol[:, :GDN_DV]
        w_ref[h] = sol[:, GDN_DV:]

    def state(qi, h):
        return s_ref[h] if carry else s0_ref[qi, h]

    def state_products(h):
        for qi in range(nseq):
            rs = slice(qi * L, (qi + 1) * L)
            lhs = jnp.concatenate([w_ref[h, rs, :], qg_ref[h, rs, :]], axis=0).astype(BF16)
            res = _dot(lhs, state(qi, h).astype(BF16))
            u_ref[h, rs, :] = u_ref[h, rs, :] - res[:L]
            os_ref[h, rs, :] = res[L:]

    def outputs(h):
        ls = slice(h * GDN_DV, (h + 1) * GDN_DV)
        v_new = u_ref[h]
        o = os_ref[h] + _dot(att_ref[h][:, :HALF], v_new.astype(BF16))
        inv = lax.rsqrt(jnp.mean(o * o, axis=-1, keepdims=True) + EPS)
        on = o * inv * ng_ref[...]
        o_ref[:, ls] = (on * _silu(gate_ref[:, ls])).astype(o_ref.dtype)
        for qi in range(nseq):
            rs = slice(qi * L, (qi + 1) * L)
            kv = lax.dot_general(ke_ref[h, rs, :].astype(BF16), v_new[rs].astype(BF16),
                                 (((0,), (0,)), ((), ())), preferred_element_type=F32)
            eg = egh_all[(qi + 1) * L - 1:(qi + 1) * L, SM_DEC + h:SM_DEC + h + 1]
            s_new = state(qi, h) * eg + kv
            if carry:
                s_ref[h] = s_new
            else:
                so_ref[qi, h] = s_new

    stages = [chunk_matrices]
    p = 2
    while p < L:
        stages += [square_power, extend_inverse]
        p *= 2
    stages += [first_solve, residual, refine, state_products, outputs]
    ngroups = GDN_HEADS // group
    for t in range(ngroups + len(stages) - 1):
        for k, stage in enumerate(stages):
            if 0 <= t - k < ngroups:
                for h in range((t - k) * group, (t - k + 1) * group):
                    stage(h)

    if carry:
        @pl.when(pl.program_id(1) == pl.num_programs(1) - 1)
        def _():
            so_ref[...] = s_ref[...]


def _gdn(proj, small, conv_w, alog_pad, dtb_pad, norm_g, conv0, s0, *, bsz, t):
    carry = s0 is None
    r = GDN_ROWS
    n_tok = proj.shape[0]
    tail = CONV_W - 1
    if carry:
        assert t % r == 0
        seq_len, nseq, grid = r, 1, (bsz, t // r)
        row = lambda b, n: b * (t // r) + n
    else:
        assert r % t == 0 and bsz % (r // t) == 0
        seq_len, nseq, grid = t, r // t, (bsz // (r // t), 1)
        row = lambda b, n: b
    qb = COL_QKV_B // GDN_QK
    gb = COL_GATE_B // GDN_V
    rowspec = lambda cb: pl.BlockSpec((r, GDN_QK), lambda b, n: (row(b, n), cb))
    cwspec = lambda cb: pl.BlockSpec((CONV_W, GDN_QK), lambda b, n: (0, cb))
    vec = pl.BlockSpec((1, LANES), lambda b, n: (0, 0))
    in_specs = [rowspec(qb), rowspec(qb + 1), rowspec(qb + 2), rowspec(gb),
                pl.BlockSpec((r, LANES), lambda b, n: (row(b, n), 0)), vec, vec, vec,
                cwspec(0), cwspec(1), cwspec(2)]
    args = [proj, proj, proj, proj, small, alog_pad, dtb_pad, norm_g, conv_w, conv_w, conv_w]
    hshape = (GDN_HEADS, r, LANES)
    sq = (GDN_HEADS, r, r)
    wide = (GDN_HEADS, r, 2 * LANES)
    scratch = ([pltpu.VMEM(sq, BF16)] * 2 + [pltpu.VMEM(wide, BF16)] + [pltpu.VMEM(sq, F32)]
               + [pltpu.VMEM(hshape, BF16)]
               + [pltpu.VMEM(wide, F32)] * 2
               + [pltpu.VMEM(hshape, F32)] * 5
               + [pltpu.VMEM((tail, 8 + r + 8, GDN_QK), F32)] * 3
               + [pltpu.VMEM((r, GDN_QK), F32)] * 3)
    if carry:
        state = pl.BlockSpec((None, GDN_HEADS, GDN_DK, GDN_DV), lambda b, n: (b, 0, 0, 0))
        scratch += [pltpu.VMEM((GDN_HEADS, GDN_DK, GDN_DV), F32)]
    else:
        state = pl.BlockSpec((nseq, GDN_HEADS, GDN_DK, GDN_DV), lambda b, n: (b, 0, 0, 0))
        c0spec = lambda cb: pl.BlockSpec((nseq, tail, GDN_QK), lambda b, n: (b, 0, cb))
        in_specs += [c0spec(0), c0spec(1), c0spec(2), state]
        args += [conv0, conv0, conv0, s0]
    return pl.pallas_call(
        functools.partial(_gdn_kernel, seq_len=seq_len, carry=carry,
                          group=GDN_GROUP_CARRY if carry else GDN_GROUP_SEQS),
        grid=grid,
        in_specs=in_specs,
        out_specs=[pl.BlockSpec((r, GDN_V), lambda b, n: (row(b, n), 0)), state],
        out_shape=[jax.ShapeDtypeStruct((n_tok, GDN_V), BF16),
                   jax.ShapeDtypeStruct((bsz, GDN_HEADS, GDN_DK, GDN_DV), F32)],
        scratch_shapes=scratch,
        compiler_params=pltpu.CompilerParams(
            dimension_semantics=("arbitrary", "arbitrary"), vmem_limit_bytes=VMEM_LIMIT),
        name="gdn",
    )(*args)


def _merge_kernel(oa_ref, ob_ref, wa_ref, wb_ref, ma_ref, mb_ref, o_ref):
    ya = jnp.dot(oa_ref[...].astype(BF16), wa_ref[...], preferred_element_type=F32)
    yb = jnp.dot(ob_ref[...].astype(BF16), wb_ref[...], preferred_element_type=F32)
    merged = _sigmoid(ma_ref[...]) * ya + _sigmoid(mb_ref[...]) * yb
    o_ref[...] = merged.astype(o_ref.dtype)


def _merge(o_a, o_b, w_br_a, w_br_b, proj):
    n_tok = o_a.shape[0]
    tm = min(1024, n_tok)
    tn = 512
    ma, mb = COL_MA // tn, COL_MB // tn
    return pl.pallas_call(
        _merge_kernel,
        grid=(n_tok // tm, D_MODEL // tn),
        in_specs=[
            pl.BlockSpec((tm, GLA_V), lambda i, j: (i, 0)),
            pl.BlockSpec((tm, GDN_V), lambda i, j: (i, 0)),
            pl.BlockSpec((GLA_V, tn), lambda i, j: (0, j)),
            pl.BlockSpec((GDN_V, tn), lambda i, j: (0, j)),
            pl.BlockSpec((tm, tn), lambda i, j: (i, ma + j)),
            pl.BlockSpec((tm, tn), lambda i, j: (i, mb + j)),
        ],
        out_specs=pl.BlockSpec((tm, tn), lambda i, j: (i, j)),
        out_shape=jax.ShapeDtypeStruct((n_tok, D_MODEL), BF16),
        compiler_params=pltpu.CompilerParams(
            dimension_semantics=("arbitrary", "arbitrary"), vmem_limit_bytes=VMEM_LIMIT),
        name="merge",
    )(o_a, o_b, w_br_a, w_br_b, proj, proj)


def _out_kernel(m_ref, w_ref, x_ref, g_ref, y_ref):
    out = x_ref[...] + jnp.dot(m_ref[...], w_ref[...], preferred_element_type=F32)
    inv = lax.rsqrt(jnp.mean(out * out, axis=-1, keepdims=True) + EPS)
    y_ref[...] = out * inv * g_ref[...]


def _out_proj(merged, w_out, x2d, final_g):
    n_tok = merged.shape[0]
    tm = min(512, n_tok)
    return pl.pallas_call(
        _out_kernel,
        grid=(n_tok // tm,),
        in_specs=[
            pl.BlockSpec((tm, D_MODEL), lambda i: (i, 0)),
            pl.BlockSpec((D_MODEL, D_MODEL), lambda i: (0, 0)),
            pl.BlockSpec((tm, D_MODEL), lambda i: (i, 0)),
            pl.BlockSpec((1, D_MODEL), lambda i: (0, 0)),
        ],
        out_specs=pl.BlockSpec((tm, D_MODEL), lambda i: (i, 0)),
        out_shape=jax.ShapeDtypeStruct((n_tok, D_MODEL), F32),
        compiler_params=pltpu.CompilerParams(
            dimension_semantics=("arbitrary",), vmem_limit_bytes=VMEM_LIMIT),
        name="out_proj",
    )(merged, w_out, x2d, final_g)


def _layer(x, s_gla, s_gdn, conv_buf, p, *, gla_rows):
    bsz, t, d = x.shape
    x2d = x.reshape(bsz * t, d)
    tail = CONV_W - 1
    proj, small = _in_proj(x2d, p["ln_g"], p["w_main"], p["w_small"])
    conv_new = proj.reshape(bsz, t, N_MAIN)[:, t - tail:, COL_QKV_B:COL_QKV_B + CONV_CH]
    o_a, s_gla_new = _gla(proj, small, p["w_alpha2"], p["b_alpha"], p["gla_norm_g"], s_gla,
                          bsz=bsz, t=t, rows=gla_rows)
    o_b, s_gdn_new = _gdn(proj, small, p["conv_w"], p["alog_pad"], p["dtb_pad"],
                          p["gdn_norm_g"], conv_buf, s_gdn, bsz=bsz, t=t)
    merged = _merge(o_a, o_b, p["w_br_a"], p["w_br_b"], proj)
    y = _out_proj(merged, p["w_out"], x2d, p["final_g"])
    return y.reshape(bsz, t, d), s_gla_new, s_gdn_new, conv_new


def _prep_params(ln_in_g, w_in, w_alpha2, b_alpha, conv_w, a_log, dt_bias, gla_norm_g,
                 gdn_norm_g, w_br_a, w_br_b, w_out, final_norm_g):
    wt = jnp.transpose(w_in)
    w_main = _prep_w_main(wt)
    n_small = GLA_RANK + 2 * GDN_HEADS
    w_small = jnp.concatenate(
        [wt[W_IN_OFFS[3]:W_IN_OFFS[4]], wt[W_IN_OFFS[6]:W_IN_OFFS[8]],
         jnp.zeros((LANES - n_small, D_MODEL), F32)], axis=0)
    lane_pad = lambda v: jnp.zeros((1, LANES), F32).at[0, SM_DEC:SM_DEC + GDN_HEADS].set(v)
    return {
        "ln_g": ln_in_g.reshape(1, D_MODEL),
        "w_main": w_main,
        "w_small": w_small,
        "w_alpha2": w_alpha2,
        "b_alpha": b_alpha.reshape(1, GLA_QK),
        "conv_w": conv_w,
        "alog_pad": lane_pad(a_log),
        "dtb_pad": lane_pad(dt_bias),
        "gla_norm_g": gla_norm_g.reshape(1, GLA_DV),
        "gdn_norm_g": gdn_norm_g.reshape(1, GDN_DV),
        "w_br_a": w_br_a.astype(BF16),
        "w_br_b": w_br_b.astype(BF16),
        "w_out": w_out.astype(BF16),
        "final_g": final_norm_g.reshape(1, D_MODEL),
    }


def kernel(x_prompt, x_sample, state_gla, state_gdn, state_conv, ln_in_g, w_in, w_alpha2,
           b_alpha, conv_w, a_log, dt_bias, gla_norm_g, gdn_norm_g, w_br_a, w_br_b, w_out,
           final_norm_g):
    assert ln_in_g.shape[0] == 1, "single layer"
    p = _prep_params(ln_in_g[0], w_in[0], w_alpha2[0], b_alpha[0], conv_w[0], a_log[0],
                     dt_bias[0], gla_norm_g[0], gdn_norm_g[0], w_br_a[0], w_br_b[0], w_out[0],
                     final_norm_g)
    yp, gla_p, gdn_p, conv_p = _layer(x_prompt, None, None, None, p, gla_rows=GLA_PROMPT_ROWS)
    ys, gla_s, gdn_s, conv_s = _layer(x_sample, state_gla[0], state_gdn[0], state_conv[0], p,
                                      gla_rows=GLA_SAMPLE_ROWS)
    return (yp, ys, gla_p[None], gdn_p[None], conv_p[None], gla_s[None], gdn_s[None],
            conv_s[None])
```

```python
import functools

import jax
import jax.numpy as jnp
from jax import lax
from jax.experimental import pallas as pl
from jax.experimental.pallas import tpu as pltpu

F32 = jnp.float32
BF16 = jnp.bfloat16

D_MODEL = 2048
GLA_HEADS = 4
GLA_DK = 256
GLA_DV = 512
GLA_RANK = 16
GLA_TAU = 16.0
GDN_HEADS = 16
GDN_DK = 128
GDN_DV = 128
CONV_W = 4
CHUNK = 64
EPS = 1e-6

GLA_QK = GLA_HEADS * GLA_DK
GLA_V = GLA_HEADS * GLA_DV
GDN_QK = GDN_HEADS * GDN_DK
GDN_V = GDN_HEADS * GDN_DV
CONV_CH = 2 * GDN_QK + GDN_V

LANES = 128
COL_QA = 0
COL_KA = COL_QA + GLA_QK
COL_VA = COL_KA + GLA_QK
COL_GATE_A = COL_VA + GLA_V
COL_QKV_B = COL_GATE_A + GLA_V
COL_GATE_B = COL_QKV_B + CONV_CH
COL_MA = COL_GATE_B + GDN_V
COL_MB = COL_MA + D_MODEL
N_MAIN = COL_MB + D_MODEL
SM_LR = 0
SM_BETA = GLA_RANK
SM_DEC = SM_BETA + GDN_HEADS

VMEM_LIMIT = 56 * 1024 * 1024


def _bdot(a, b):
    return jnp.dot(a.astype(BF16), b.astype(BF16), preferred_element_type=F32)


def _bdot_nt(a, b):
    return lax.dot_general(a.astype(BF16), b.astype(BF16), (((1,), (1,)), ((), ())),
                           preferred_element_type=F32)


def _bdot_tn(a, b):
    return lax.dot_general(a.astype(BF16), b.astype(BF16), (((0,), (0,)), ((), ())),
                           preferred_element_type=F32)


def _dot(a, b):
    return jnp.dot(a, b, preferred_element_type=F32)


def _sigmoid(x):
    return 1.0 / (1.0 + jnp.exp(-x))


def _silu(x):
    return x * _sigmoid(x)


def _softplus(x):
    return jnp.maximum(x, 0.0) + jnp.log(1.0 + jnp.exp(-jnp.abs(x)))


W_IN_SIZES = (GLA_QK, GLA_QK, GLA_V, GLA_RANK, GLA_V, CONV_CH, GDN_HEADS, GDN_HEADS, GDN_V,
              D_MODEL, D_MODEL)
W_IN_OFFS = tuple(sum(W_IN_SIZES[:i]) for i in range(len(W_IN_SIZES) + 1))
PREP_STARTS = (0, COL_GATE_A, COL_GATE_B)
PREP_SHIFTS = (0, W_IN_OFFS[4] - COL_GATE_A, W_IN_OFFS[8] - COL_GATE_B)
PREP_ROWS = 1024
PREP_HALO = 64


def _wprep_kernel(a_ref, nxt_ref, o_ref):
    j = pl.program_id(0)
    bounds = tuple(s // PREP_ROWS for s in PREP_STARTS) + (N_MAIN // PREP_ROWS,)
    for run, shift in enumerate(PREP_SHIFTS):
        @pl.when((j >= bounds[run]) & (j < bounds[run + 1]))
        def _(shift=shift):
            if shift == 0:
                o_ref[...] = a_ref[...].astype(BF16)
            else:
                x = jnp.concatenate([a_ref[...], nxt_ref[...]], axis=0)
                o_ref[...] = x[shift:shift + PREP_ROWS, :].astype(BF16)


def _prep_w_main(wt):
    d = wt.shape[1]
    assert all(s % PREP_ROWS == 0 for s in PREP_STARTS)
    assert max(PREP_SHIFTS) <= PREP_HALO and all(s % 16 == 0 for s in PREP_SHIFTS)
    return pl.pallas_call(
        _wprep_kernel,
        grid=(N_MAIN // PREP_ROWS,),
        in_specs=[
            pl.BlockSpec((PREP_ROWS, d), lambda j: (j, 0)),
            pl.BlockSpec((PREP_HALO, d), lambda j: ((j + 1) * (PREP_ROWS // PREP_HALO), 0)),
        ],
        out_specs=pl.BlockSpec((PREP_ROWS, d), lambda j: (j, 0)),
        out_shape=jax.ShapeDtypeStruct((N_MAIN, d), BF16),
        compiler_params=pltpu.CompilerParams(
            dimension_semantics=("arbitrary",), vmem_limit_bytes=VMEM_LIMIT),
        name="w_prep",
    )(wt, wt)


def _dot_nt(a, b):
    return lax.dot_general(a, b, (((1,), (1,)), ((), ())), preferred_element_type=F32)


IN_PROJ_TM = 1024
IN_PROJ_TN = 1536


def _in_proj_kernel(x_ref, g_ref, w_ref, ws_ref, o_ref, os_ref, h_ref):
    @pl.when(pl.program_id(1) == 0)
    def _():
        x = x_ref[...]
        inv = lax.rsqrt(jnp.mean(x * x, axis=-1, keepdims=True) + EPS)
        h = (x * inv * g_ref[...]).astype(BF16)
        h_ref[...] = h
        os_ref[...] = _dot_nt(h, ws_ref[...].astype(BF16))

    o_ref[...] = _dot_nt(h_ref[...], w_ref[...])


def _in_proj(x2d, ln_g, w_main, w_small):
    n_tok, d = x2d.shape
    tm = min(IN_PROJ_TM, n_tok)
    tn = IN_PROJ_TN
    assert N_MAIN % tn == 0
    return pl.pallas_call(
        _in_proj_kernel,
        grid=(n_tok // tm, N_MAIN // tn),
        in_specs=[
            pl.BlockSpec((tm, d), lambda i, j: (i, 0)),
            pl.BlockSpec((1, d), lambda i, j: (0, 0)),
            pl.BlockSpec((tn, d), lambda i, j: (j, 0)),
            pl.BlockSpec((LANES, d), lambda i, j: (0, 0)),
        ],
        out_specs=[
            pl.BlockSpec((tm, tn), lambda i, j: (i, j)),
            pl.BlockSpec((tm, LANES), lambda i, j: (i, 0)),
        ],
        out_shape=[jax.ShapeDtypeStruct((n_tok, N_MAIN), F32),
                   jax.ShapeDtypeStruct((n_tok, LANES), F32)],
        scratch_shapes=[pltpu.VMEM((tm, d), BF16)],
        compiler_params=pltpu.CompilerParams(
            dimension_semantics=("arbitrary", "arbitrary"), vmem_limit_bytes=VMEM_LIMIT),
        name="in_proj",
    )(x2d, ln_g, w_main, w_small)


GLA_PROMPT_ROWS = 512
GLA_SAMPLE_ROWS = 32
GLA_ATT_ROWS = 256


def _gla_kernel(*refs, seq_len, rows, carry):
    n_in = 8 if carry else 9
    q_ref, k_ref, v_ref, gate_ref, sm_ref, wal_ref, bal_ref, ng_ref = refs[:8]
    s0_ref = None if carry else refs[8]
    o_ref, so_ref = refs[n_in:n_in + 2]
    qd_ref, ke_ref, ebl_ref, oacc_ref = refs[n_in + 2:n_in + 6]
    s_ref = refs[n_in + 6] if carry else None
    r, L = rows, seq_len
    nsub = r // L
    g = min(r, CHUNK)

    if carry:
        @pl.when(pl.program_id(1) == 0)
        def _():
            s_ref[...] = jnp.zeros_like(s_ref)

    ga = min(r, GLA_ATT_ROWS)
    ri = lax.broadcasted_iota(jnp.int32, (ga, ga), 0)
    ci = lax.broadcasted_iota(jnp.int32, (ga, ga), 1)
    incl = ((ri // L) == (ci // L)) & (ri >= ci)
    gi = lax.broadcasted_iota(jnp.int32, (g, 2 * LANES), 0)
    gl = lax.broadcasted_iota(jnp.int32, (g, 2 * LANES), 1)
    gc = gl % g
    tri3 = ((gl < 3 * g) & ((gi // L) == (gc // L)) & (gi >= gc)).astype(BF16)

    def seg_cumsum(x):
        outs = []
        for gq in range(r // g):
            xg = x[gq * g:(gq + 1) * g]
            hi, r1 = _split_f32(xg)
            mid, r2 = _split_f32(r1)
            parts = [hi, mid, r2.astype(BF16)]
            if 3 * g < 2 * LANES:
                parts.append(jnp.zeros((2 * LANES - 3 * g, xg.shape[1]), BF16))
            outs.append(_dot(tri3, jnp.concatenate(parts, axis=0)))
        return outs[0] if len(outs) == 1 else jnp.concatenate(outs, axis=0)

    for h in range(GLA_HEADS):
        lk = slice(h * GLA_DK, (h + 1) * GLA_DK)
        lv = slice(h * GLA_DV, (h + 1) * GLA_DV)
        q = q_ref[:, lk] * (GLA_DK ** -0.5)
        k = k_ref[:, lk]
        lr = sm_ref[:, SM_LR:SM_LR + GLA_RANK]
        z = _bdot(lr, wal_ref[:, lk]) + bal_ref[:, lk]
        la = -_softplus(-z) / GLA_TAU
        b = seg_cumsum(la)
        lasts = [jnp.broadcast_to(b[(qi + 1) * L - 1:(qi + 1) * L, :], (L, GLA_DK))
                 for qi in range(nsub)]
        b_last = lasts[0] if nsub == 1 else jnp.concatenate(lasts, axis=0)
        q_d = q * jnp.exp(b)
        k_d = k * jnp.exp(-b)
        for gq in range(r // ga):
            gs = slice(gq * ga, (gq + 1) * ga)
            att = jnp.where(incl, _bdot_nt(q_d[gs], k_d[gs]), 0.0)
            oacc_ref[h, gs, :] = _bdot(att, v_ref[gs, lv])
        qd_ref[h] = q_d
        ke_ref[h] = k * jnp.exp(b_last - b)
        ebl_ref[h] = jnp.exp(b_last)

    for qi in range(nsub):
        rs = slice(qi * L, (qi + 1) * L)
        for h in range(GLA_HEADS):
            lv = slice(h * GLA_DV, (h + 1) * GLA_DV)
            s = s_ref[h] if carry else s0_ref[qi, h]
            oacc_ref[h, rs, :] = oacc_ref[h, rs, :] + _bdot(qd_ref[h, rs, :], s)
            kv = _bdot_tn(ke_ref[h, rs, :], v_ref[rs, lv])
            eb = ebl_ref[h, qi * L:qi * L + 1, :]
            col = jnp.transpose(jnp.broadcast_to(eb, (LANES, GLA_DK)))
            s_new = s * jnp.concatenate([col] * (GLA_DV // LANES), axis=1) + kv
            if carry:
                s_ref[h] = s_new
            else:
                so_ref[qi, h] = s_new

    for h in range(GLA_HEADS):
        lv = slice(h * GLA_DV, (h + 1) * GLA_DV)
        o = oacc_ref[h]
        inv = lax.rsqrt(jnp.mean(o * o, axis=-1, keepdims=True) + EPS)
        on = o * inv * ng_ref[...]
        o_ref[:, lv] = (on * _silu(gate_ref[:, lv])).astype(o_ref.dtype)

    if carry:
        @pl.when(pl.program_id(1) == pl.num_programs(1) - 1)
        def _():
            so_ref[...] = s_ref[...]


def _gla(proj, small, w_alpha2, b_alpha, norm_g, s0, *, bsz, t, rows):
    carry = s0 is None
    r = rows
    n_tok = proj.shape[0]
    if carry:
        seq_len = min(CHUNK, t)
        assert t % r == 0 and r % seq_len == 0
        grid = (bsz, t // r)
        row = lambda b, n: b * (t // r) + n
    else:
        assert t <= CHUNK and r % t == 0 and bsz % (r // t) == 0
        seq_len = t
        grid = (bsz // (r // t), 1)
        row = lambda b, n: b
    nsub = r // seq_len
    qb, kb = COL_QA // GLA_QK, COL_KA // GLA_QK
    vb, gb = COL_VA // GLA_V, COL_GATE_A // GLA_V
    in_specs = [
        pl.BlockSpec((r, GLA_QK), lambda b, n: (row(b, n), qb)),
        pl.BlockSpec((r, GLA_QK), lambda b, n: (row(b, n), kb)),
        pl.BlockSpec((r, GLA_V), lambda b, n: (row(b, n), vb)),
        pl.BlockSpec((r, GLA_V), lambda b, n: (row(b, n), gb)),
        pl.BlockSpec((r, LANES), lambda b, n: (row(b, n), 0)),
        pl.BlockSpec((GLA_RANK, GLA_QK), lambda b, n: (0, 0)),
        pl.BlockSpec((1, GLA_QK), lambda b, n: (0, 0)),
        pl.BlockSpec((1, GLA_DV), lambda b, n: (0, 0)),
    ]
    args = [proj, proj, proj, proj, small, w_alpha2, b_alpha, norm_g]
    scratch = [pltpu.VMEM((GLA_HEADS, r, GLA_DK), F32)] * 3 + [
        pltpu.VMEM((GLA_HEADS, r, GLA_DV), F32)]
    if carry:
        state = pl.BlockSpec((None, GLA_HEADS, GLA_DK, GLA_DV), lambda b, n: (b, 0, 0, 0))
        scratch += [pltpu.VMEM((GLA_HEADS, GLA_DK, GLA_DV), F32)]
    else:
        state = pl.BlockSpec((nsub, GLA_HEADS, GLA_DK, GLA_DV), lambda b, n: (b, 0, 0, 0))
        in_specs += [state]
        args += [s0]
    return pl.pallas_call(
        functools.partial(_gla_kernel, seq_len=seq_len, rows=r, carry=carry),
        grid=grid,
        in_specs=in_specs,
        out_specs=[pl.BlockSpec((r, GLA_V), lambda b, n: (row(b, n), 0)), state],
        out_shape=[jax.ShapeDtypeStruct((n_tok, GLA_V), BF16),
                   jax.ShapeDtypeStruct((bsz, GLA_HEADS, GLA_DK, GLA_DV), F32)],
        scratch_shapes=scratch,
        compiler_params=pltpu.CompilerParams(
            dimension_semantics=("arbitrary", "arbitrary"), vmem_limit_bytes=VMEM_LIMIT),
        name="gla",
    )(*args)


GDN_ROWS = 64
HALF = LANES // 2
GDN_GROUP_CARRY = GDN_HEADS
GDN_GROUP_SEQS = 1


def _split_f32(x):
    hi = x.astype(BF16)
    return hi, x - hi.astype(F32)


def _lhs3(x, low_half):
    hi, lo = _split_f32(x)
    t0 = jnp.where(low_half, x, lo).astype(BF16)
    t1 = jnp.where(low_half, hi, jnp.zeros_like(hi))
    return jnp.concatenate([t0, t1], axis=1)


def _rhs3(hi, lo):
    return jnp.concatenate([hi, hi, lo, lo], axis=0)


def _gdn_kernel(*refs, seq_len, carry, group):
    r = GDN_ROWS
    L = seq_len
    nseq = r // L
    tail = CONV_W - 1
    pad = 8
    n_in = 11 if carry else 15
    (q_ref, k_ref, v_ref, gate_ref, sm_ref, alog_ref, dtb_ref, ng_ref, cwq_ref, cwk_ref,
     cwv_ref) = refs[:11]
    cq0_ref, ck0_ref, cv0_ref, s0_ref = (None,) * 4 if carry else refs[11:15]
    o_ref, so_ref = refs[n_in:n_in + 2]
    (nb_ref, tb_ref, nlhs_ref, p_ref, att_ref, rhs_ref, x0_ref, qg_ref, ke_ref, u_ref, w_ref,
     os_ref, eq_ref, ek_ref, ev_ref, cq_ref, ck_ref, cv_ref) = refs[n_in + 2:n_in + 20]
    ri = lax.broadcasted_iota(jnp.int32, (r, LANES), 0)
    li = lax.broadcasted_iota(jnp.int32, (r, LANES), 1)
    cj = li % HALF
    low_half = li < HALF
    if carry:
        (s_ref,) = refs[n_in + 20:]

        @pl.when(pl.program_id(1) == 0)
        def _():
            s_ref[...] = jnp.zeros_like(s_ref)
            for e_ref in (eq_ref, ek_ref, ev_ref):
                e_ref[:, pad:2 * pad, :] = jnp.zeros((tail, pad, GDN_QK), F32)
    else:
        s_ref = None

    def conv_stream(e_ref, x_ref, w_ref, out_ref, c0_ref, ls):
        acc = None
        for i in range(CONV_W):
            sh = tail - i
            if sh == 0:
                xs = x_ref[:, ls]
            else:
                for a in range(0, r, 8):
                    e_ref[sh - 1, pad + sh + a:pad + sh + a + 8, ls] = x_ref[a:a + 8, ls]
                if not carry:
                    for qi in range(nseq):
                        e_ref[sh - 1, pad + qi * L:pad + qi * L + sh, ls] = (
                            c0_ref[qi, tail - sh:tail, ls])
                xs = e_ref[sh - 1, pad:pad + r, ls]
            term = xs * w_ref[i:i + 1, ls]
            acc = term if acc is None else acc + term
        if carry:
            for sh in range(1, tail + 1):
                e_ref[sh - 1, pad:pad + sh, ls] = x_ref[r - sh:r, ls]
        out_ref[:, ls] = _silu(acc)

    def conv(h):
        ls = slice(h * GDN_DK, (h + 1) * GDN_DK)
        conv_stream(eq_ref, q_ref, cwq_ref, cq_ref, cq0_ref, ls)
        conv_stream(ek_ref, k_ref, cwk_ref, ck_ref, ck0_ref, ls)
        conv_stream(ev_ref, v_ref, cwv_ref, cv_ref, cv0_ref, ls)

    sm = sm_ref[...]
    same = (ri // L) == (cj // L)
    incl = same & (ri >= cj)
    strict = same & (ri > cj)
    eye = (ri == cj).astype(F32)

    beta_all = _sigmoid(sm)
    g_all = -jnp.exp(alog_ref[...]) * _softplus(sm + dtb_ref[...])
    g_hi, g_r1 = _split_f32(g_all)
    g_mid, g_r2 = _split_f32(g_r1)
    g_lo = g_r2.astype(BF16)
    tri = incl.astype(BF16)
    tri_lhs = jnp.concatenate([tri, jnp.where(low_half, tri, jnp.zeros_like(tri))], axis=1)
    gh_all = _dot(tri_lhs, jnp.concatenate([g_hi, g_mid, g_lo, g_lo], axis=0))
    gh_t = jnp.transpose(jnp.concatenate([gh_all, gh_all], axis=0))
    glast_all = jnp.concatenate(
        [jnp.broadcast_to(gh_all[(qi + 1) * L - 1:(qi + 1) * L, :], (L, LANES))
         for qi in range(nseq)], axis=0)
    egh_all = jnp.exp(gh_all)
    kend_all = jnp.exp(glast_all - gh_all)

    def normalize(h):
        ls = slice(h * GDN_DK, (h + 1) * GDN_DK)
        q = cq_ref[:, ls]
        k = ck_ref[:, ls]
        cq_ref[:, ls] = (q * lax.rsqrt(jnp.sum(q * q, axis=-1, keepdims=True) + EPS)
                         * (GDN_DK ** -0.5))
        ck_ref[:, ls] = k * lax.rsqrt(jnp.sum(k * k, axis=-1, keepdims=True) + EPS)

    def chunk_matrices(h):
        ls = slice(h * GDN_DK, (h + 1) * GDN_DK)
        q = cq_ref[:, ls]
        k = ck_ref[:, ls]
        col = slice(SM_DEC + h, SM_DEC + h + 1)
        beta = beta_all[:, SM_BETA + h:SM_BETA + h + 1]
        gh = gh_all[:, col]
        decay = jnp.where(incl, jnp.exp(gh - gh_t[col, :]), 0.0)
        kb = k.astype(BF16)
        qkk = lax.dot_general(jnp.concatenate([q.astype(BF16), kb], axis=0),
                              jnp.concatenate([kb, kb], axis=0),
                              (((1,), (1,)), ((), ())), preferred_element_type=F32)
        att_ref[h] = (qkk[:r] * decay).astype(BF16)
        nm = jnp.where(strict, -(beta * qkk[r:] * decay), 0.0)
        nlhs_ref[h] = _lhs3(nm, low_half)
        nb_ref[h] = nm[:, :HALF].astype(BF16)
        p_ref[h] = (eye + nm)[:, :HALF]

    def row_scalings(h):
        ls = slice(h * GDN_DK, (h + 1) * GDN_DK)
        q = cq_ref[:, ls]
        k = ck_ref[:, ls]
        v = cv_ref[:, ls]
        col = slice(SM_DEC + h, SM_DEC + h + 1)
        beta = beta_all[:, SM_BETA + h:SM_BETA + h + 1]
        egh = egh_all[:, col]
        rhs_ref[h] = jnp.concatenate([v * beta, k * (beta * egh)], axis=1)
        qg_ref[h] = q * egh
        ke_ref[h] = k * kend_all[:, col]

    def square_power(h):
        n_pow = nb_ref[h]
        nb_ref[h] = _dot(n_pow, n_pow).astype(BF16)

    def extend_inverse(h):
        pm = p_ref[h]
        p_ref[h] = pm + _dot(pm.astype(BF16), nb_ref[h])

    def first_solve(h):
        tb = p_ref[h].astype(BF16)
        tb_ref[h] = tb
        x0_ref[h] = _dot(tb, rhs_ref[h].astype(BF16))

    def residual(h):
        x0 = x0_ref[h]
        x_hi, x_lo = _split_f32(x0)
        rhs_ref[h] = rhs_ref[h] - x0 + _dot(nlhs_ref[h], _rhs3(x_hi, x_lo.astype(BF16)))

    def refine(h):
        sol = x0_ref[h] + _dot(tb_ref[h], rhs_ref[h].astype(BF16))
        u_ref[h] = sol[:, :GDN_DV]
        w_ref[h] = sol[:, GDN_DV:]

    def state(qi, h):
        return s_ref[h] if carry else s0_ref[qi, h]

    def state_products(h):
        for qi in range(nseq):
            rs = slice(qi * L, (qi + 1) * L)
            lhs = jnp.concatenate([w_ref[h, rs, :], qg_ref[h, rs, :]], axis=0).astype(BF16)
            res = _dot(lhs, state(qi, h).astype(BF16))
            u_ref[h, rs, :] = u_ref[h, rs, :] - res[:L]
            os_ref[h, rs, :] = res[L:]

    def outputs(h):
        ls = slice(h * GDN_DV, (h + 1) * GDN_DV)
        v_new = u_ref[h]
        o = os_ref[h] + _dot(att_ref[h][:, :HALF], v_new.astype(BF16))
        inv = lax.rsqrt(jnp.mean(o * o, axis=-1, keepdims=True) + EPS)
        on = o * inv * ng_ref[...]
        o_ref[:, ls] = (on * _silu(gate_ref[:, ls])).astype(o_ref.dtype)
        for qi in range(nseq):
            rs = slice(qi * L, (qi + 1) * L)
            kv = lax.dot_general(ke_ref[h, rs, :].astype(BF16), v_new[rs].astype(BF16),
                                 (((0,), (0,)), ((), ())), preferred_element_type=F32)
            eg = egh_all[(qi + 1) * L - 1:(qi + 1) * L, SM_DEC + h:SM_DEC + h + 1]
            s_new = state(qi, h) * eg + kv
            if carry:
                s_ref[h] = s_new
            else:
                so_ref[qi, h] = s_new

    stages = [conv, normalize, chunk_matrices, row_scalings]
    p = 2
    while p < L:
        stages += [square_power, extend_inverse]
        p *= 2
    stages += [first_solve, residual, refine, state_products, outputs]
    ngroups = GDN_HEADS // group
    for t in range(ngroups + len(stages) - 1):
        for k, stage in enumerate(stages):
            if 0 <= t - k < ngroups:
                for h in range((t - k) * group, (t - k + 1) * group):
                    stage(h)

    if carry:
        @pl.when(pl.program_id(1) == pl.num_programs(1) - 1)
        def _():
            so_ref[...] = s_ref[...]


def _gdn(proj, small, conv_w, alog_pad, dtb_pad, norm_g, conv0, s0, *, bsz, t):
    carry = s0 is None
    r = GDN_ROWS
    n_tok = proj.shape[0]
    tail = CONV_W - 1
    if carry:
        assert t % r == 0
        seq_len, nseq, grid = r, 1, (bsz, t // r)
        row = lambda b, n: b * (t // r) + n
    else:
        assert r % t == 0 and bsz % (r // t) == 0
        seq_len, nseq, grid = t, r // t, (bsz // (r // t), 1)
        row = lambda b, n: b
    qb = COL_QKV_B // GDN_QK
    gb = COL_GATE_B // GDN_V
    rowspec = lambda cb: pl.BlockSpec((r, GDN_QK), lambda b, n: (row(b, n), cb))
    cwspec = lambda cb: pl.BlockSpec((CONV_W, GDN_QK), lambda b, n: (0, cb))
    vec = pl.BlockSpec((1, LANES), lambda b, n: (0, 0))
    in_specs = [rowspec(qb), rowspec(qb + 1), rowspec(qb + 2), rowspec(gb),
                pl.BlockSpec((r, LANES), lambda b, n: (row(b, n), 0)), vec, vec, vec,
                cwspec(0), cwspec(1), cwspec(2)]
    args = [proj, proj, proj, proj, small, alog_pad, dtb_pad, norm_g, conv_w, conv_w, conv_w]
    hshape = (GDN_HEADS, r, LANES)
    sq = (GDN_HEADS, r, r)
    wide = (GDN_HEADS, r, 2 * LANES)
    scratch = ([pltpu.VMEM(sq, BF16)] * 2 + [pltpu.VMEM(wide, BF16)] + [pltpu.VMEM(sq, F32)]
               + [pltpu.VMEM(hshape, BF16)]
               + [pltpu.VMEM(wide, F32)] * 2
               + [pltpu.VMEM(hshape, F32)] * 5
               + [pltpu.VMEM((tail, 8 + r + 8, GDN_QK), F32)] * 3
               + [pltpu.VMEM((r, GDN_QK), F32)] * 3)
    if carry:
        state = pl.BlockSpec((None, GDN_HEADS, GDN_DK, GDN_DV), lambda b, n: (b, 0, 0, 0))
        scratch += [pltpu.VMEM((GDN_HEADS, GDN_DK, GDN_DV), F32)]
    else:
        state = pl.BlockSpec((nseq, GDN_HEADS, GDN_DK, GDN_DV), lambda b, n: (b, 0, 0, 0))
        c0spec = lambda cb: pl.BlockSpec((nseq, tail, GDN_QK), lambda b, n: (b, 0, cb))
        in_specs += [c0spec(0), c0spec(1), c0spec(2), state]
        args += [conv0, conv0, conv0, s0]
    return pl.pallas_call(
        functools.partial(_gdn_kernel, seq_len=seq_len, carry=carry,
                          group=GDN_GROUP_CARRY if carry else GDN_GROUP_SEQS),
        grid=grid,
        in_specs=in_specs,
        out_specs=[pl.BlockSpec((r, GDN_V), lambda b, n: (row(b, n), 0)), state],
        out_shape=[jax.ShapeDtypeStruct((n_tok, GDN_V), BF16),
                   jax.ShapeDtypeStruct((bsz, GDN_HEADS, GDN_DK, GDN_DV), F32)],
        scratch_shapes=scratch,
        compiler_params=pltpu.CompilerParams(
            dimension_semantics=("arbitrary", "arbitrary"), vmem_limit_bytes=VMEM_LIMIT),
        name="gdn",
    )(*args)


def _merge_kernel(oa_ref, ob_ref, wa_ref, wb_ref, ma_ref, mb_ref, o_ref):
    ya = jnp.dot(oa_ref[...].astype(BF16), wa_ref[...], preferred_element_type=F32)
    yb = jnp.dot(ob_ref[...].astype(BF16), wb_ref[...], preferred_element_type=F32)
    merged = _sigmoid(ma_ref[...]) * ya + _sigmoid(mb_ref[...]) * yb
    o_ref[...] = merged.astype(o_ref.dtype)


def _merge(o_a, o_b, w_br_a, w_br_b, proj):
    n_tok = o_a.shape[0]
    tm = min(1024, n_tok)
    tn = 512
    ma, mb = COL_MA // tn, COL_MB // tn
    return pl.pallas_call(
        _merge_kernel,
        grid=(n_tok // tm, D_MODEL // tn),
        in_specs=[
            pl.BlockSpec((tm, GLA_V), lambda i, j: (i, 0)),
            pl.BlockSpec((tm, GDN_V), lambda i, j: (i, 0)),
            pl.BlockSpec((GLA_V, tn), lambda i, j: (0, j)),
            pl.BlockSpec((GDN_V, tn), lambda i, j: (0, j)),
            pl.BlockSpec((tm, tn), lambda i, j: (i, ma + j)),
            pl.BlockSpec((tm, tn), lambda i, j: (i, mb + j)),
        ],
        out_specs=pl.BlockSpec((tm, tn), lambda i, j: (i, j)),
        out_shape=jax.ShapeDtypeStruct((n_tok, D_MODEL), BF16),
        compiler_params=pltpu.CompilerParams(
            dimension_semantics=("arbitrary", "arbitrary"), vmem_limit_bytes=VMEM_LIMIT),
        name="merge",
    )(o_a, o_b, w_br_a, w_br_b, proj, proj)


def _out_kernel(m_ref, w_ref, x_ref, g_ref, y_ref):
    out = x_ref[...] + jnp.dot(m_ref[...], w_ref[...], preferred_element_type=F32)
    inv = lax.rsqrt(jnp.mean(out * out, axis=-1, keepdims=True) + EPS)
    y_ref[...] = out * inv * g_ref[...]


def _out_proj(merged, w_out, x2d, final_g):
    n_tok = merged.shape[0]
    tm = min(512, n_tok)
    return pl.pallas_call(
        _out_kernel,
        grid=(n_tok // tm,),
        in_specs=[
            pl.BlockSpec((tm, D_MODEL), lambda i: (i, 0)),
            pl.BlockSpec((D_MODEL, D_MODEL), lambda i: (0, 0)),
            pl.BlockSpec((tm, D_MODEL), lambda i: (i, 0)),
            pl.BlockSpec((1, D_MODEL), lambda i: (0, 0)),
        ],
        out_specs=pl.BlockSpec((tm, D_MODEL), lambda i: (i, 0)),
        out_shape=jax.ShapeDtypeStruct((n_tok, D_MODEL), F32),
        compiler_params=pltpu.CompilerParams(
            dimension_semantics=("arbitrary",), vmem_limit_bytes=VMEM_LIMIT),
        name="out_proj",
    )(merged, w_out, x2d, final_g)


def _layer(x, s_gla, s_gdn, conv_buf, p, *, gla_rows):
    bsz, t, d = x.shape
    x2d = x.reshape(bsz * t, d)
    tail = CONV_W - 1
    proj, small = _in_proj(x2d, p["ln_g"], p["w_main"], p["w_small"])
    conv_new = proj.reshape(bsz, t, N_MAIN)[:, t - tail:, COL_QKV_B:COL_QKV_B + CONV_CH]
    o_a, s_gla_new = _gla(proj, small, p["w_alpha2"], p["b_alpha"], p["gla_norm_g"], s_gla,
                          bsz=bsz, t=t, rows=gla_rows)
    o_b, s_gdn_new = _gdn(proj, small, p["conv_w"], p["alog_pad"], p["dtb_pad"],
                          p["gdn_norm_g"], conv_buf, s_gdn, bsz=bsz, t=t)
    merged = _merge(o_a, o_b, p["w_br_a"], p["w_br_b"], proj)
    y = _out_proj(merged, p["w_out"], x2d, p["final_g"])
    return y.reshape(bsz, t, d), s_gla_new, s_gdn_new, conv_new


def _prep_params(ln_in_g, w_in, w_alpha2, b_alpha, conv_w, a_log, dt_bias, gla_norm_g,
                 gdn_norm_g, w_br_a, w_br_b, w_out, final_norm_g):
    wt = jnp.transpose(w_in)
    w_main = _prep_w_main(wt)
    n_small = GLA_RANK + 2 * GDN_HEADS
    w_small = jnp.concatenate(
        [wt[W_IN_OFFS[3]:W_IN_OFFS[4]], wt[W_IN_OFFS[6]:W_IN_OFFS[8]],
         jnp.zeros((LANES - n_small, D_MODEL), F32)], axis=0)
    lane_pad = lambda v: jnp.zeros((1, LANES), F32).at[0, SM_DEC:SM_DEC + GDN_HEADS].set(v)
    return {
        "ln_g": ln_in_g.reshape(1, D_MODEL),
        "w_main": w_main,
        "w_small": w_small,
        "w_alpha2": w_alpha2,
        "b_alpha": b_alpha.reshape(1, GLA_QK),
        "conv_w": conv_w,
        "alog_pad": lane_pad(a_log),
        "dtb_pad": lane_pad(dt_bias),
        "gla_norm_g": gla_norm_g.reshape(1, GLA_DV),
        "gdn_norm_g": gdn_norm_g.reshape(1, GDN_DV),
        "w_br_a": w_br_a.astype(BF16),
        "w_br_b": w_br_b.astype(BF16),
        "w_out": w_out.astype(BF16),
        "final_g": final_norm_g.reshape(1, D_MODEL),
    }


def kernel(x_prompt, x_sample, state_gla, state_gdn, state_conv, ln_in_g, w_in, w_alpha2,
           b_alpha, conv_w, a_log, dt_bias, gla_norm_g, gdn_norm_g, w_br_a, w_br_b, w_out,
           final_norm_g):
    assert ln_in_g.shape[0] == 1, "single layer"
    p = _prep_params(ln_in_g[0], w_in[0], w_alpha2[0], b_alpha[0], conv_w[0], a_log[0],
                     dt_bias[0], gla_norm_g[0], gdn_norm_g[0], w_br_a[0], w_br_b[0], w_out[0],
                     final_norm_g)
    yp, gla_p, gdn_p, conv_p = _layer(x_prompt, None, None, None, p, gla_rows=GLA_PROMPT_ROWS)
    ys, gla_s, gdn_s, conv_s = _layer(x_sample, state_gla[0], state_gdn[0], state_conv[0], p,
                                      gla_rows=GLA_SAMPLE_ROWS)
    return (yp, ys, gla_p[None], gdn_p[None], conv_p[None], gla_s[None], gdn_s[None],
            conv_s[None])
```

```python
import functools

import jax
import jax.numpy as jnp
from jax import lax
from jax.experimental import pallas as pl
from jax.experimental.pallas import tpu as pltpu

F32 = jnp.float32
BF16 = jnp.bfloat16

D_MODEL = 2048
GLA_HEADS = 4
GLA_DK = 256
GLA_DV = 512
GLA_RANK = 16
GLA_TAU = 16.0
GDN_HEADS = 16
GDN_DK = 128
GDN_DV = 128
CONV_W = 4
CHUNK = 64
EPS = 1e-6

GLA_QK = GLA_HEADS * GLA_DK
GLA_V = GLA_HEADS * GLA_DV
GDN_QK = GDN_HEADS * GDN_DK
GDN_V = GDN_HEADS * GDN_DV
CONV_CH = 2 * GDN_QK + GDN_V

LANES = 128
COL_QA = 0
COL_KA = COL_QA + GLA_QK
COL_VA = COL_KA + GLA_QK
COL_GATE_A = COL_VA + GLA_V
COL_QKV_B = COL_GATE_A + GLA_V
COL_GATE_B = COL_QKV_B + CONV_CH
COL_MA = COL_GATE_B + GDN_V
COL_MB = COL_MA + D_MODEL
N_MAIN = COL_MB + D_MODEL
SM_LR = 0
SM_BETA = GLA_RANK
SM_DEC = SM_BETA + GDN_HEADS

VMEM_LIMIT = 56 * 1024 * 1024


def _bdot(a, b):
    return jnp.dot(a.astype(BF16), b.astype(BF16), preferred_element_type=F32)


def _bdot_nt(a, b):
    return lax.dot_general(a.astype(BF16), b.astype(BF16), (((1,), (1,)), ((), ())),
                           preferred_element_type=F32)


def _bdot_tn(a, b):
    return lax.dot_general(a.astype(BF16), b.astype(BF16), (((0,), (0,)), ((), ())),
                           preferred_element_type=F32)


def _dot(a, b):
    return jnp.dot(a, b, preferred_element_type=F32)


def _sigmoid(x):
    return 1.0 / (1.0 + jnp.exp(-x))


def _silu(x):
    return x * _sigmoid(x)


def _softplus(x):
    return jnp.maximum(x, 0.0) + jnp.log(1.0 + jnp.exp(-jnp.abs(x)))


W_IN_SIZES = (GLA_QK, GLA_QK, GLA_V, GLA_RANK, GLA_V, CONV_CH, GDN_HEADS, GDN_HEADS, GDN_V,
              D_MODEL, D_MODEL)
W_IN_OFFS = tuple(sum(W_IN_SIZES[:i]) for i in range(len(W_IN_SIZES) + 1))
PREP_STARTS = (0, COL_GATE_A, COL_GATE_B)
PREP_SHIFTS = (0, W_IN_OFFS[4] - COL_GATE_A, W_IN_OFFS[8] - COL_GATE_B)
PREP_ROWS = 1024
PREP_HALO = 64


def _wprep_kernel(a_ref, nxt_ref, o_ref):
    j = pl.program_id(0)
    bounds = tuple(s // PREP_ROWS for s in PREP_STARTS) + (N_MAIN // PREP_ROWS,)
    for run, shift in enumerate(PREP_SHIFTS):
        @pl.when((j >= bounds[run]) & (j < bounds[run + 1]))
        def _(shift=shift):
            if shift == 0:
                o_ref[...] = a_ref[...].astype(BF16)
            else:
                x = jnp.concatenate([a_ref[...], nxt_ref[...]], axis=0)
                o_ref[...] = x[shift:shift + PREP_ROWS, :].astype(BF16)


def _prep_w_main(wt):
    d = wt.shape[1]
    assert all(s % PREP_ROWS == 0 for s in PREP_STARTS)
    assert max(PREP_SHIFTS) <= PREP_HALO and all(s % 16 == 0 for s in PREP_SHIFTS)
    return pl.pallas_call(
        _wprep_kernel,
        grid=(N_MAIN // PREP_ROWS,),
        in_specs=[
            pl.BlockSpec((PREP_ROWS, d), lambda j: (j, 0)),
            pl.BlockSpec((PREP_HALO, d), lambda j: ((j + 1) * (PREP_ROWS // PREP_HALO), 0)),
        ],
        out_specs=pl.BlockSpec((PREP_ROWS, d), lambda j: (j, 0)),
        out_shape=jax.ShapeDtypeStruct((N_MAIN, d), BF16),
        compiler_params=pltpu.CompilerParams(
            dimension_semantics=("arbitrary",), vmem_limit_bytes=VMEM_LIMIT),
        name="w_prep",
    )(wt, wt)


def _dot_nt(a, b):
    return lax.dot_general(a, b, (((1,), (1,)), ((), ())), preferred_element_type=F32)


IN_PROJ_TM = 1024
IN_PROJ_TN = 1536


def _in_proj_kernel(x_ref, g_ref, w_ref, ws_ref, o_ref, os_ref, h_ref):
    @pl.when(pl.program_id(1) == 0)
    def _():
        x = x_ref[...]
        inv = lax.rsqrt(jnp.mean(x * x, axis=-1, keepdims=True) + EPS)
        h = (x * inv * g_ref[...]).astype(BF16)
        h_ref[...] = h
        os_ref[...] = _dot_nt(h, ws_ref[...].astype(BF16))

    o_ref[...] = _dot_nt(h_ref[...], w_ref[...])


def _in_proj(x2d, ln_g, w_main, w_small):
    n_tok, d = x2d.shape
    tm = min(IN_PROJ_TM, n_tok)
    tn = IN_PROJ_TN
    assert N_MAIN % tn == 0
    return pl.pallas_call(
        _in_proj_kernel,
        grid=(n_tok // tm, N_MAIN // tn),
        in_specs=[
            pl.BlockSpec((tm, d), lambda i, j: (i, 0)),
            pl.BlockSpec((1, d), lambda i, j: (0, 0)),
            pl.BlockSpec((tn, d), lambda i, j: (j, 0)),
            pl.BlockSpec((LANES, d), lambda i, j: (0, 0)),
        ],
        out_specs=[
            pl.BlockSpec((tm, tn), lambda i, j: (i, j)),
            pl.BlockSpec((tm, LANES), lambda i, j: (i, 0)),
        ],
        out_shape=[jax.ShapeDtypeStruct((n_tok, N_MAIN), F32),
                   jax.ShapeDtypeStruct((n_tok, LANES), F32)],
        scratch_shapes=[pltpu.VMEM((tm, d), BF16)],
        compiler_params=pltpu.CompilerParams(
            dimension_semantics=("arbitrary", "arbitrary"), vmem_limit_bytes=VMEM_LIMIT),
        name="in_proj",
    )(x2d, ln_g, w_main, w_small)


GLA_PROMPT_ROWS = 512
GLA_SAMPLE_ROWS = 32
GLA_ATT_ROWS = 256


def _gla_kernel(*refs, seq_len, rows, carry):
    n_in = 8 if carry else 9
    q_ref, k_ref, v_ref, gate_ref, sm_ref, wal_ref, bal_ref, ng_ref = refs[:8]
    s0_ref = None if carry else refs[8]
    o_ref, so_ref = refs[n_in:n_in + 2]
    qd_ref, ke_ref, ebl_ref, oacc_ref = refs[n_in + 2:n_in + 6]
    s_ref = refs[n_in + 6] if carry else None
    r, L = rows, seq_len
    nsub = r // L
    g = min(r, CHUNK)

    if carry:
        @pl.when(pl.program_id(1) == 0)
        def _():
            s_ref[...] = jnp.zeros_like(s_ref)

    ga = min(r, GLA_ATT_ROWS)
    ri = lax.broadcasted_iota(jnp.int32, (ga, ga), 0)
    ci = lax.broadcasted_iota(jnp.int32, (ga, ga), 1)
    incl = ((ri // L) == (ci // L)) & (ri >= ci)
    gi = lax.broadcasted_iota(jnp.int32, (g, 2 * LANES), 0)
    gl = lax.broadcasted_iota(jnp.int32, (g, 2 * LANES), 1)
    gc = gl % g
    tri3 = ((gl < 3 * g) & ((gi // L) == (gc // L)) & (gi >= gc)).astype(BF16)

    def seg_cumsum(x):
        outs = []
        for gq in range(r // g):
            xg = x[gq * g:(gq + 1) * g]
            hi, r1 = _split_f32(xg)
            mid, r2 = _split_f32(r1)
            parts = [hi, mid, r2.astype(BF16)]
            if 3 * g < 2 * LANES:
                parts.append(jnp.zeros((2 * LANES - 3 * g, xg.shape[1]), BF16))
            outs.append(_dot(tri3, jnp.concatenate(parts, axis=0)))
        return outs[0] if len(outs) == 1 else jnp.concatenate(outs, axis=0)

    for h in range(GLA_HEADS):
        lk = slice(h * GLA_DK, (h + 1) * GLA_DK)
        lv = slice(h * GLA_DV, (h + 1) * GLA_DV)
        q = q_ref[:, lk] * (GLA_DK ** -0.5)
        k = k_ref[:, lk]
        lr = sm_ref[:, SM_LR:SM_LR + GLA_RANK]
        z = _bdot(lr, wal_ref[:, lk]) + bal_ref[:, lk]
        la = -_softplus(-z) / GLA_TAU
        b = seg_cumsum(la)
        lasts = [jnp.broadcast_to(b[(qi + 1) * L - 1:(qi + 1) * L, :], (L, GLA_DK))
                 for qi in range(nsub)]
        b_last = lasts[0] if nsub == 1 else jnp.concatenate(lasts, axis=0)
        q_d = q * jnp.exp(b)
        k_d = k * jnp.exp(-b)
        for gq in range(r // ga):
            gs = slice(gq * ga, (gq + 1) * ga)
            att = jnp.where(incl, _bdot_nt(q_d[gs], k_d[gs]), 0.0)
            oacc_ref[h, gs, :] = _bdot(att, v_ref[gs, lv])
        qd_ref[h] = q_d
        ke_ref[h] = k * jnp.exp(b_last - b)
        ebl_ref[h] = jnp.exp(b_last)

    for qi in range(nsub):
        rs = slice(qi * L, (qi + 1) * L)
        for h in range(GLA_HEADS):
            lv = slice(h * GLA_DV, (h + 1) * GLA_DV)
            s = s_ref[h] if carry else s0_ref[qi, h]
            oacc_ref[h, rs, :] = oacc_ref[h, rs, :] + _bdot(qd_ref[h, rs, :], s)
            kv = _bdot_tn(ke_ref[h, rs, :], v_ref[rs, lv])
            eb = ebl_ref[h, qi * L:qi * L + 1, :]
            col = jnp.transpose(jnp.broadcast_to(eb, (LANES, GLA_DK)))
            s_new = s * jnp.concatenate([col] * (GLA_DV // LANES), axis=1) + kv
            if carry:
                s_ref[h] = s_new
            else:
                so_ref[qi, h] = s_new

    for h in range(GLA_HEADS):
        lv = slice(h * GLA_DV, (h + 1) * GLA_DV)
        o = oacc_ref[h]
        inv = lax.rsqrt(jnp.mean(o * o, axis=-1, keepdims=True) + EPS)
        on = o * inv * ng_ref[...]
        o_ref[:, lv] = (on * _silu(gate_ref[:, lv])).astype(o_ref.dtype)

    if carry:
        @pl.when(pl.program_id(1) == pl.num_programs(1) - 1)
        def _():
            so_ref[...] = s_ref[...]


def _gla(proj, small, w_alpha2, b_alpha, norm_g, s0, *, bsz, t, rows):
    carry = s0 is None
    r = rows
    n_tok = proj.shape[0]
    if carry:
        seq_len = min(CHUNK, t)
        assert t % r == 0 and r % seq_len == 0
        grid = (bsz, t // r)
        row = lambda b, n: b * (t // r) + n
    else:
        assert t <= CHUNK and r % t == 0 and bsz % (r // t) == 0
        seq_len = t
        grid = (bsz // (r // t), 1)
        row = lambda b, n: b
    nsub = r // seq_len
    qb, kb = COL_QA // GLA_QK, COL_KA // GLA_QK
    vb, gb = COL_VA // GLA_V, COL_GATE_A // GLA_V
    in_specs = [
        pl.BlockSpec((r, GLA_QK), lambda b, n: (row(b, n), qb)),
        pl.BlockSpec((r, GLA_QK), lambda b, n: (row(b, n), kb)),
        pl.BlockSpec((r, GLA_V), lambda b, n: (row(b, n), vb)),
        pl.BlockSpec((r, GLA_V), lambda b, n: (row(b, n), gb)),
        pl.BlockSpec((r, LANES), lambda b, n: (row(b, n), 0)),
        pl.BlockSpec((GLA_RANK, GLA_QK), lambda b, n: (0, 0)),
        pl.BlockSpec((1, GLA_QK), lambda b, n: (0, 0)),
        pl.BlockSpec((1, GLA_DV), lambda b, n: (0, 0)),
    ]
    args = [proj, proj, proj, proj, small, w_alpha2, b_alpha, norm_g]
    scratch = [pltpu.VMEM((GLA_HEADS, r, GLA_DK), F32)] * 3 + [
        pltpu.VMEM((GLA_HEADS, r, GLA_DV), F32)]
    if carry:
        state = pl.BlockSpec((None, GLA_HEADS, GLA_DK, GLA_DV), lambda b, n: (b, 0, 0, 0))
        scratch += [pltpu.VMEM((GLA_HEADS, GLA_DK, GLA_DV), F32)]
    else:
        state = pl.BlockSpec((nsub, GLA_HEADS, GLA_DK, GLA_DV), lambda b, n: (b, 0, 0, 0))
        in_specs += [state]
        args += [s0]
    return pl.pallas_call(
        functools.partial(_gla_kernel, seq_len=seq_len, rows=r, carry=carry),
        grid=grid,
        in_specs=in_specs,
        out_specs=[pl.BlockSpec((r, GLA_V), lambda b, n: (row(b, n), 0)), state],
        out_shape=[jax.ShapeDtypeStruct((n_tok, GLA_V), BF16),
                   jax.ShapeDtypeStruct((bsz, GLA_HEADS, GLA_DK, GLA_DV), F32)],
        scratch_shapes=scratch,
        compiler_params=pltpu.CompilerParams(
            dimension_semantics=("arbitrary", "arbitrary"), vmem_limit_bytes=VMEM_LIMIT),
        name="gla",
    )(*args)


GDN_ROWS = 64
HALF = LANES // 2
GDN_GROUP_CARRY = GDN_HEADS
GDN_GROUP_SEQS = 1


def _split_f32(x):
    hi = x.astype(BF16)
    return hi, x - hi.astype(F32)


def _lhs3(x, low_half):
    hi, lo = _split_f32(x)
    t0 = jnp.where(low_half, x, lo).astype(BF16)
    t1 = jnp.where(low_half, hi, jnp.zeros_like(hi))
    return jnp.concatenate([t0, t1], axis=1)


def _rhs3(hi, lo):
    return jnp.concatenate([hi, hi, lo, lo], axis=0)


def _gdn_kernel(*refs, seq_len, carry, group):
    r = GDN_ROWS
    L = seq_len
    nseq = r // L
    tail = CONV_W - 1
    pad = 8
    n_in = 11 if carry else 15
    (q_ref, k_ref, v_ref, gate_ref, sm_ref, alog_ref, dtb_ref, ng_ref, cwq_ref, cwk_ref,
     cwv_ref) = refs[:11]
    cq0_ref, ck0_ref, cv0_ref, s0_ref = (None,) * 4 if carry else refs[11:15]
    o_ref, so_ref = refs[n_in:n_in + 2]
    (nb_ref, tb_ref, nlhs_ref, p_ref, att_ref, rhs_ref, x0_ref, qg_ref, ke_ref, u_ref, w_ref,
     os_ref, eq_ref, ek_ref, ev_ref, cq_ref, ck_ref, cv_ref) = refs[n_in + 2:n_in + 20]
    ri = lax.broadcasted_iota(jnp.int32, (r, LANES), 0)
    li = lax.broadcasted_iota(jnp.int32, (r, LANES), 1)
    cj = li % HALF
    low_half = li < HALF
    if carry:
        (s_ref,) = refs[n_in + 20:]

        @pl.when(pl.program_id(1) == 0)
        def _():
            s_ref[...] = jnp.zeros_like(s_ref)
            for e_ref in (eq_ref, ek_ref, ev_ref):
                e_ref[:, pad:2 * pad, :] = jnp.zeros((tail, pad, GDN_QK), F32)
    else:
        s_ref = None

    def conv(e_ref, x_ref, w_ref, out_ref, c0_ref):
        acc = None
        for i in range(CONV_W):
            sh = tail - i
            if sh == 0:
                xs = x_ref[...]
            else:
                e_ref[sh - 1, pad + sh:pad + sh + r, :] = x_ref[...]
                if not carry:
                    for qi in range(nseq):
                        e_ref[sh - 1, pad + qi * L:pad + qi * L + sh, :] = c0_ref[qi, tail - sh:tail, :]
                xs = e_ref[sh - 1, pad:pad + r, :]
            term = xs * w_ref[i:i + 1, :]
            acc = term if acc is None else acc + term
        if carry:
            for sh in range(1, tail + 1):
                e_ref[sh - 1, pad:pad + sh, :] = x_ref[r - sh:r, :]
        out_ref[...] = _silu(acc)

    conv(eq_ref, q_ref, cwq_ref, cq_ref, cq0_ref)
    conv(ek_ref, k_ref, cwk_ref, ck_ref, ck0_ref)
    conv(ev_ref, v_ref, cwv_ref, cv_ref, cv0_ref)
    sm = sm_ref[...]
    same = (ri // L) == (cj // L)
    incl = same & (ri >= cj)
    strict = same & (ri > cj)
    eye = (ri == cj).astype(F32)

    beta_all = _sigmoid(sm)
    g_all = -jnp.exp(alog_ref[...]) * _softplus(sm + dtb_ref[...])
    g_hi, g_r1 = _split_f32(g_all)
    g_mid, g_r2 = _split_f32(g_r1)
    g_lo = g_r2.astype(BF16)
    tri = incl.astype(BF16)
    tri_lhs = jnp.concatenate([tri, jnp.where(low_half, tri, jnp.zeros_like(tri))], axis=1)
    gh_all = _dot(tri_lhs, jnp.concatenate([g_hi, g_mid, g_lo, g_lo], axis=0))
    gh_t = jnp.transpose(jnp.concatenate([gh_all, gh_all], axis=0))
    glast_all = jnp.concatenate(
        [jnp.broadcast_to(gh_all[(qi + 1) * L - 1:(qi + 1) * L, :], (L, LANES))
         for qi in range(nseq)], axis=0)
    egh_all = jnp.exp(gh_all)
    kend_all = jnp.exp(glast_all - gh_all)

    def normalize(h):
        ls = slice(h * GDN_DK, (h + 1) * GDN_DK)
        q = cq_ref[:, ls]
        k = ck_ref[:, ls]
        cq_ref[:, ls] = (q * lax.rsqrt(jnp.sum(q * q, axis=-1, keepdims=True) + EPS)
                         * (GDN_DK ** -0.5))
        ck_ref[:, ls] = k * lax.rsqrt(jnp.sum(k * k, axis=-1, keepdims=True) + EPS)

    def chunk_matrices(h):
        ls = slice(h * GDN_DK, (h + 1) * GDN_DK)
        q = cq_ref[:, ls]
        k = ck_ref[:, ls]
        col = slice(SM_DEC + h, SM_DEC + h + 1)
        beta = beta_all[:, SM_BETA + h:SM_BETA + h + 1]
        gh = gh_all[:, col]
        decay = jnp.where(incl, jnp.exp(gh - gh_t[col, :]), 0.0)
        kb = k.astype(BF16)
        qkk = lax.dot_general(jnp.concatenate([q.astype(BF16), kb], axis=0),
                              jnp.concatenate([kb, kb], axis=0),
                              (((1,), (1,)), ((), ())), preferred_element_type=F32)
        att_ref[h] = (qkk[:r] * decay).astype(BF16)
        nm = jnp.where(strict, -(beta * qkk[r:] * decay), 0.0)
        nlhs_ref[h] = _lhs3(nm, low_half)
        nb_ref[h] = nm[:, :HALF].astype(BF16)
        p_ref[h] = (eye + nm)[:, :HALF]

    def row_scalings(h):
        ls = slice(h * GDN_DK, (h + 1) * GDN_DK)
        q = cq_ref[:, ls]
        k = ck_ref[:, ls]
        v = cv_ref[:, ls]
        col = slice(SM_DEC + h, SM_DEC + h + 1)
        beta = beta_all[:, SM_BETA + h:SM_BETA + h + 1]
        egh = egh_all[:, col]
        rhs_ref[h] = jnp.concatenate([v * beta, k * (beta * egh)], axis=1)
        qg_ref[h] = q * egh
        ke_ref[h] = k * kend_all[:, col]

    def square_power(h):
        n_pow = nb_ref[h]
        nb_ref[h] = _dot(n_pow, n_pow).astype(BF16)

    def extend_inverse(h):
        pm = p_ref[h]
        p_ref[h] = pm + _dot(pm.astype(BF16), nb_ref[h])

    def first_solve(h):
        tb = p_ref[h].astype(BF16)
        tb_ref[h] = tb
        x0_ref[h] = _dot(tb, rhs_ref[h].astype(BF16))

    def residual(h):
        x0 = x0_ref[h]
        x_hi, x_lo = _split_f32(x0)
        rhs_ref[h] = rhs_ref[h] - x0 + _dot(nlhs_ref[h], _rhs3(x_hi, x_lo.astype(BF16)))

    def refine(h):
        sol = x0_ref[h] + _dot(tb_ref[h], rhs_ref[h].astype(BF16))
        u_ref[h] = sol[:, :GDN_DV]
        w_ref[h] = sol[:, GDN_DV:]

    def state(qi, h):
        return s_ref[h] if carry else s0_ref[qi, h]

    def state_products(h):
        for qi in range(nseq):
            rs = slice(qi * L, (qi + 1) * L)
            lhs = jnp.concatenate([w_ref[h, rs, :], qg_ref[h, rs, :]], axis=0).astype(BF16)
            res = _dot(lhs, state(qi, h).astype(BF16))
            u_ref[h, rs, :] = u_ref[h, rs, :] - res[:L]
            os_ref[h, rs, :] = res[L:]

    def outputs(h):
        ls = slice(h * GDN_DV, (h + 1) * GDN_DV)
        v_new = u_ref[h]
        o = os_ref[h] + _dot(att_ref[h][:, :HALF], v_new.astype(BF16))
        inv = lax.rsqrt(jnp.mean(o * o, axis=-1, keepdims=True) + EPS)
        on = o * inv * ng_ref[...]
        o_ref[:, ls] = (on * _silu(gate_ref[:, ls])).astype(o_ref.dtype)
        for qi in range(nseq):
            rs = slice(qi * L, (qi + 1) * L)
            kv = lax.dot_general(ke_ref[h, rs, :].astype(BF16), v_new[rs].astype(BF16),
                                 (((0,), (0,)), ((), ())), preferred_element_type=F32)
            eg = egh_all[(qi + 1) * L - 1:(qi + 1) * L, SM_DEC + h:SM_DEC + h + 1]
            s_new = state(qi, h) * eg + kv
            if carry:
                s_ref[h] = s_new
            else:
                so_ref[qi, h] = s_new

    stages = [normalize, chunk_matrices, row_scalings]
    p = 2
    while p < L:
        stages += [square_power, extend_inverse]
        p *= 2
    stages += [first_solve, residual, refine, state_products, outputs]
    ngroups = GDN_HEADS // group
    for t in range(ngroups + len(stages) - 1):
        for k, stage in enumerate(stages):
            if 0 <= t - k < ngroups:
                for h in range((t - k) * group, (t - k + 1) * group):
                    stage(h)

    if carry:
        @pl.when(pl.program_id(1) == pl.num_programs(1) - 1)
        def _():
            so_ref[...] = s_ref[...]


def _gdn(proj, small, conv_w, alog_pad, dtb_pad, norm_g, conv0, s0, *, bsz, t):
    carry = s0 is None
    r = GDN_ROWS
    n_tok = proj.shape[0]
    tail = CONV_W - 1
    if carry:
        assert t % r == 0
        seq_len, nseq, grid = r, 1, (bsz, t // r)
        row = lambda b, n: b * (t // r) + n
    else:
        assert r % t == 0 and bsz % (r // t) == 0
        seq_len, nseq, grid = t, r // t, (bsz // (r // t), 1)
        row = lambda b, n: b
    qb = COL_QKV_B // GDN_QK
    gb = COL_GATE_B // GDN_V
    rowspec = lambda cb: pl.BlockSpec((r, GDN_QK), lambda b, n: (row(b, n), cb))
    cwspec = lambda cb: pl.BlockSpec((CONV_W, GDN_QK), lambda b, n: (0, cb))
    vec = pl.BlockSpec((1, LANES), lambda b, n: (0, 0))
    in_specs = [rowspec(qb), rowspec(qb + 1), rowspec(qb + 2), rowspec(gb),
                pl.BlockSpec((r, LANES), lambda b, n: (row(b, n), 0)), vec, vec, vec,
                cwspec(0), cwspec(1), cwspec(2)]
    args = [proj, proj, proj, proj, small, alog_pad, dtb_pad, norm_g, conv_w, conv_w, conv_w]
    hshape = (GDN_HEADS, r, LANES)
    sq = (GDN_HEADS, r, r)
    wide = (GDN_HEADS, r, 2 * LANES)
    scratch = ([pltpu.VMEM(sq, BF16)] * 2 + [pltpu.VMEM(wide, BF16)] + [pltpu.VMEM(sq, F32)]
               + [pltpu.VMEM(hshape, BF16)]
               + [pltpu.VMEM(wide, F32)] * 2
               + [pltpu.VMEM(hshape, F32)] * 5
               + [pltpu.VMEM((tail, 8 + r + 8, GDN_QK), F32)] * 3
               + [pltpu.VMEM((r, GDN_QK), F32)] * 3)
    if carry:
        state = pl.BlockSpec((None, GDN_HEADS, GDN_DK, GDN_DV), lambda b, n: (b, 0, 0, 0))
        scratch += [pltpu.VMEM((GDN_HEADS, GDN_DK, GDN_DV), F32)]
    else:
        state = pl.BlockSpec((nseq, GDN_HEADS, GDN_DK, GDN_DV), lambda b, n: (b, 0, 0, 0))
        c0spec = lambda cb: pl.BlockSpec((nseq, tail, GDN_QK), lambda b, n: (b, 0, cb))
        in_specs += [c0spec(0), c0spec(1), c0spec(2), state]
        args += [conv0, conv0, conv0, s0]
    return pl.pallas_call(
        functools.partial(_gdn_kernel, seq_len=seq_len, carry=carry,
                          group=GDN_GROUP_CARRY if carry else GDN_GROUP_SEQS),
        grid=grid,
        in_specs=in_specs,
        out_specs=[pl.BlockSpec((r, GDN_V), lambda b, n: (row(b, n), 0)), state],
        out_shape=[jax.ShapeDtypeStruct((n_tok, GDN_V), BF16),
                   jax.ShapeDtypeStruct((bsz, GDN_HEADS, GDN_DK, GDN_DV), F32)],
        scratch_shapes=scratch,
        compiler_params=pltpu.CompilerParams(
            dimension_semantics=("arbitrary", "arbitrary"), vmem_limit_bytes=VMEM_LIMIT),
        name="gdn",
    )(*args)


def _merge_kernel(oa_ref, ob_ref, wa_ref, wb_ref, ma_ref, mb_ref, o_ref):
    ya = jnp.dot(oa_ref[...].astype(BF16), wa_ref[...], preferred_element_type=F32)
    yb = jnp.dot(ob_ref[...].astype(BF16), wb_ref[...], preferred_element_type=F32)
    merged = _sigmoid(ma_ref[...]) * ya + _sigmoid(mb_ref[...]) * yb
    o_ref[...] = merged.astype(o_ref.dtype)


def _merge(o_a, o_b, w_br_a, w_br_b, proj):
    n_tok = o_a.shape[0]
    tm = min(1024, n_tok)
    tn = 512
    ma, mb = COL_MA // tn, COL_MB // tn
    return pl.pallas_call(
        _merge_kernel,
        grid=(n_tok // tm, D_MODEL // tn),
        in_specs=[
            pl.BlockSpec((tm, GLA_V), lambda i, j: (i, 0)),
            pl.BlockSpec((tm, GDN_V), lambda i, j: (i, 0)),
            pl.BlockSpec((GLA_V, tn), lambda i, j: (0, j)),
            pl.BlockSpec((GDN_V, tn), lambda i, j: (0, j)),
            pl.BlockSpec((tm, tn), lambda i, j: (i, ma + j)),
            pl.BlockSpec((tm, tn), lambda i, j: (i, mb + j)),
        ],
        out_specs=pl.BlockSpec((tm, tn), lambda i, j: (i, j)),
        out_shape=jax.ShapeDtypeStruct((n_tok, D_MODEL), BF16),
        compiler_params=pltpu.CompilerParams(
            dimension_semantics=("arbitrary", "arbitrary"), vmem_limit_bytes=VMEM_LIMIT),
        name="merge",
    )(o_a, o_b, w_br_a, w_br_b, proj, proj)


def _out_kernel(m_ref, w_ref, x_ref, g_ref, y_ref):
    out = x_ref[...] + jnp.dot(m_ref[...], w_ref[...], preferred_element_type=F32)
    inv = lax.rsqrt(jnp.mean(out * out, axis=-1, keepdims=True) + EPS)
    y_ref[...] = out * inv * g_ref[...]


def _out_proj(merged, w_out, x2d, final_g):
    n_tok = merged.shape[0]
    tm = min(512, n_tok)
    return pl.pallas_call(
        _out_kernel,
        grid=(n_tok // tm,),
        in_specs=[
            pl.BlockSpec((tm, D_MODEL), lambda i: (i, 0)),
            pl.BlockSpec((D_MODEL, D_MODEL), lambda i: (0, 0)),
            pl.BlockSpec((tm, D_MODEL), lambda i: (i, 0)),
            pl.BlockSpec((1, D_MODEL), lambda i: (0, 0)),
        ],
        out_specs=pl.BlockSpec((tm, D_MODEL), lambda i: (i, 0)),
        out_shape=jax.ShapeDtypeStruct((n_tok, D_MODEL), F32),
        compiler_params=pltpu.CompilerParams(
            dimension_semantics=("arbitrary",), vmem_limit_bytes=VMEM_LIMIT),
        name="out_proj",
    )(merged, w_out, x2d, final_g)


def _layer(x, s_gla, s_gdn, conv_buf, p, *, gla_rows):
    bsz, t, d = x.shape
    x2d = x.reshape(bsz * t, d)
    tail = CONV_W - 1
    proj, small = _in_proj(x2d, p["ln_g"], p["w_main"], p["w_small"])
    conv_new = proj.reshape(bsz, t, N_MAIN)[:, t - tail:, COL_QKV_B:COL_QKV_B + CONV_CH]
    o_a, s_gla_new = _gla(proj, small, p["w_alpha2"], p["b_alpha"], p["gla_norm_g"], s_gla,
                          bsz=bsz, t=t, rows=gla_rows)
    o_b, s_gdn_new = _gdn(proj, small, p["conv_w"], p["alog_pad"], p["dtb_pad"],
                          p["gdn_norm_g"], conv_buf, s_gdn, bsz=bsz, t=t)
    merged = _merge(o_a, o_b, p["w_br_a"], p["w_br_b"], proj)
    y = _out_proj(merged, p["w_out"], x2d, p["final_g"])
    return y.reshape(bsz, t, d), s_gla_new, s_gdn_new, conv_new


def _prep_params(ln_in_g, w_in, w_alpha2, b_alpha, conv_w, a_log, dt_bias, gla_norm_g,
                 gdn_norm_g, w_br_a, w_br_b, w_out, final_norm_g):
    wt = jnp.transpose(w_in)
    w_main = _prep_w_main(wt)
    n_small = GLA_RANK + 2 * GDN_HEADS
    w_small = jnp.concatenate(
        [wt[W_IN_OFFS[3]:W_IN_OFFS[4]], wt[W_IN_OFFS[6]:W_IN_OFFS[8]],
         jnp.zeros((LANES - n_small, D_MODEL), F32)], axis=0)
    lane_pad = lambda v: jnp.zeros((1, LANES), F32).at[0, SM_DEC:SM_DEC + GDN_HEADS].set(v)
    return {
        "ln_g": ln_in_g.reshape(1, D_MODEL),
        "w_main": w_main,
        "w_small": w_small,
        "w_alpha2": w_alpha2,
        "b_alpha": b_alpha.reshape(1, GLA_QK),
        "conv_w": conv_w,
        "alog_pad": lane_pad(a_log),
        "dtb_pad": lane_pad(dt_bias),
        "gla_norm_g": gla_norm_g.reshape(1, GLA_DV),
        "gdn_norm_g": gdn_norm_g.reshape(1, GDN_DV),
        "w_br_a": w_br_a.astype(BF16),
        "w_br_b": w_br_b.astype(BF16),
        "w_out": w_out.astype(BF16),
        "final_g": final_norm_g.reshape(1, D_MODEL),
    }


def kernel(x_prompt, x_sample, state_gla, state_gdn, state_conv, ln_in_g, w_in, w_alpha2,
           b_alpha, conv_w, a_log, dt_bias, gla_norm_g, gdn_norm_g, w_br_a, w_br_b, w_out,
           final_norm_g):
    assert ln_in_g.shape[0] == 1, "single layer"
    p = _prep_params(ln_in_g[0], w_in[0], w_alpha2[0], b_alpha[0], conv_w[0], a_log[0],
                     dt_bias[0], gla_norm_g[0], gdn_norm_g[0], w_br_a[0], w_br_b[0], w_out[0],
                     final_norm_g)
    yp, gla_p, gdn_p, conv_p = _layer(x_prompt, None, None, None, p, gla_rows=GLA_PROMPT_ROWS)
    ys, gla_s, gdn_s, conv_s = _layer(x_sample, state_gla[0], state_gdn[0], state_conv[0], p,
                                      gla_rows=GLA_SAMPLE_ROWS)
    return (yp, ys, gla_p[None], gdn_p[None], conv_p[None], gla_s[None], gdn_s[None],
            conv_s[None])
```

```python
import functools

import jax
import jax.numpy as jnp
from jax import lax
from jax.experimental import pallas as pl
from jax.experimental.pallas import tpu as pltpu

F32 = jnp.float32
BF16 = jnp.bfloat16

D_MODEL = 2048
GLA_HEADS = 4
GLA_DK = 256
GLA_DV = 512
GLA_RANK = 16
GLA_TAU = 16.0
GDN_HEADS = 16
GDN_DK = 128
GDN_DV = 128
CONV_W = 4
CHUNK = 64
EPS = 1e-6

GLA_QK = GLA_HEADS * GLA_DK
GLA_V = GLA_HEADS * GLA_DV
GDN_QK = GDN_HEADS * GDN_DK
GDN_V = GDN_HEADS * GDN_DV
CONV_CH = 2 * GDN_QK + GDN_V

LANES = 128
COL_QA = 0
COL_KA = COL_QA + GLA_QK
COL_VA = COL_KA + GLA_QK
COL_GATE_A = COL_VA + GLA_V
COL_QKV_B = COL_GATE_A + GLA_V
COL_GATE_B = COL_QKV_B + CONV_CH
COL_MA = COL_GATE_B + GDN_V
COL_MB = COL_MA + D_MODEL
N_MAIN = COL_MB + D_MODEL
SM_LR = 0
SM_BETA = GLA_RANK
SM_DEC = SM_BETA + GDN_HEADS

VMEM_LIMIT = 56 * 1024 * 1024


def _bdot(a, b):
    return jnp.dot(a.astype(BF16), b.astype(BF16), preferred_element_type=F32)


def _bdot_nt(a, b):
    return lax.dot_general(a.astype(BF16), b.astype(BF16), (((1,), (1,)), ((), ())),
                           preferred_element_type=F32)


def _bdot_tn(a, b):
    return lax.dot_general(a.astype(BF16), b.astype(BF16), (((0,), (0,)), ((), ())),
                           preferred_element_type=F32)


def _dot(a, b):
    return jnp.dot(a, b, preferred_element_type=F32)


def _sigmoid(x):
    return 1.0 / (1.0 + jnp.exp(-x))


def _silu(x):
    return x * _sigmoid(x)


def _softplus(x):
    return jnp.maximum(x, 0.0) + jnp.log(1.0 + jnp.exp(-jnp.abs(x)))


W_IN_SIZES = (GLA_QK, GLA_QK, GLA_V, GLA_RANK, GLA_V, CONV_CH, GDN_HEADS, GDN_HEADS, GDN_V,
              D_MODEL, D_MODEL)
W_IN_OFFS = tuple(sum(W_IN_SIZES[:i]) for i in range(len(W_IN_SIZES) + 1))
PREP_STARTS = (0, COL_GATE_A, COL_GATE_B)
PREP_SHIFTS = (0, W_IN_OFFS[4] - COL_GATE_A, W_IN_OFFS[8] - COL_GATE_B)
PREP_ROWS = 1024
PREP_HALO = 64


def _wprep_kernel(a_ref, nxt_ref, o_ref):
    j = pl.program_id(0)
    bounds = tuple(s // PREP_ROWS for s in PREP_STARTS) + (N_MAIN // PREP_ROWS,)
    for run, shift in enumerate(PREP_SHIFTS):
        @pl.when((j >= bounds[run]) & (j < bounds[run + 1]))
        def _(shift=shift):
            if shift == 0:
                o_ref[...] = a_ref[...].astype(BF16)
            else:
                x = jnp.concatenate([a_ref[...], nxt_ref[...]], axis=0)
                o_ref[...] = x[shift:shift + PREP_ROWS, :].astype(BF16)


def _prep_w_main(wt):
    d = wt.shape[1]
    assert all(s % PREP_ROWS == 0 for s in PREP_STARTS)
    assert max(PREP_SHIFTS) <= PREP_HALO and all(s % 16 == 0 for s in PREP_SHIFTS)
    return pl.pallas_call(
        _wprep_kernel,
        grid=(N_MAIN // PREP_ROWS,),
        in_specs=[
            pl.BlockSpec((PREP_ROWS, d), lambda j: (j, 0)),
            pl.BlockSpec((PREP_HALO, d), lambda j: ((j + 1) * (PREP_ROWS // PREP_HALO), 0)),
        ],
        out_specs=pl.BlockSpec((PREP_ROWS, d), lambda j: (j, 0)),
        out_shape=jax.ShapeDtypeStruct((N_MAIN, d), BF16),
        compiler_params=pltpu.CompilerParams(
            dimension_semantics=("arbitrary",), vmem_limit_bytes=VMEM_LIMIT),
        name="w_prep",
    )(wt, wt)


def _dot_nt(a, b):
    return lax.dot_general(a, b, (((1,), (1,)), ((), ())), preferred_element_type=F32)


IN_PROJ_TM = 1024
IN_PROJ_TN = 1536


def _in_proj_kernel(x_ref, g_ref, w_ref, ws_ref, o_ref, os_ref, h_ref):
    @pl.when(pl.program_id(1) == 0)
    def _():
        x = x_ref[...]
        inv = lax.rsqrt(jnp.mean(x * x, axis=-1, keepdims=True) + EPS)
        h = (x * inv * g_ref[...]).astype(BF16)
        h_ref[...] = h
        os_ref[...] = _dot_nt(h, ws_ref[...].astype(BF16))

    o_ref[...] = _dot_nt(h_ref[...], w_ref[...])


def _in_proj(x2d, ln_g, w_main, w_small):
    n_tok, d = x2d.shape
    tm = min(IN_PROJ_TM, n_tok)
    tn = IN_PROJ_TN
    assert N_MAIN % tn == 0
    return pl.pallas_call(
        _in_proj_kernel,
        grid=(n_tok // tm, N_MAIN // tn),
        in_specs=[
            pl.BlockSpec((tm, d), lambda i, j: (i, 0)),
            pl.BlockSpec((1, d), lambda i, j: (0, 0)),
            pl.BlockSpec((tn, d), lambda i, j: (j, 0)),
            pl.BlockSpec((LANES, d), lambda i, j: (0, 0)),
        ],
        out_specs=[
            pl.BlockSpec((tm, tn), lambda i, j: (i, j)),
            pl.BlockSpec((tm, LANES), lambda i, j: (i, 0)),
        ],
        out_shape=[jax.ShapeDtypeStruct((n_tok, N_MAIN), F32),
                   jax.ShapeDtypeStruct((n_tok, LANES), F32)],
        scratch_shapes=[pltpu.VMEM((tm, d), BF16)],
        compiler_params=pltpu.CompilerParams(
            dimension_semantics=("arbitrary", "arbitrary"), vmem_limit_bytes=VMEM_LIMIT),
        name="in_proj",
    )(x2d, ln_g, w_main, w_small)


GLA_PROMPT_ROWS = 512
GLA_SAMPLE_ROWS = 32
GLA_ATT_ROWS = 256


def _gla_kernel(*refs, seq_len, rows, carry):
    n_in = 8 if carry else 9
    q_ref, k_ref, v_ref, gate_ref, sm_ref, wal_ref, bal_ref, ng_ref = refs[:8]
    s0_ref = None if carry else refs[8]
    o_ref, so_ref = refs[n_in:n_in + 2]
    qd_ref, ke_ref, ebl_ref, oacc_ref = refs[n_in + 2:n_in + 6]
    s_ref = refs[n_in + 6] if carry else None
    r, L = rows, seq_len
    nsub = r // L
    g = min(r, CHUNK)

    if carry:
        @pl.when(pl.program_id(1) == 0)
        def _():
            s_ref[...] = jnp.zeros_like(s_ref)

    ga = min(r, GLA_ATT_ROWS)
    ri = lax.broadcasted_iota(jnp.int32, (ga, ga), 0)
    ci = lax.broadcasted_iota(jnp.int32, (ga, ga), 1)
    incl = ((ri // L) == (ci // L)) & (ri >= ci)
    gi = lax.broadcasted_iota(jnp.int32, (g, 2 * LANES), 0)
    gl = lax.broadcasted_iota(jnp.int32, (g, 2 * LANES), 1)
    gc = gl % g
    tri3 = ((gl < 3 * g) & ((gi // L) == (gc // L)) & (gi >= gc)).astype(BF16)

    def seg_cumsum(x):
        outs = []
        for gq in range(r // g):
            xg = x[gq * g:(gq + 1) * g]
            hi, r1 = _split_f32(xg)
            mid, r2 = _split_f32(r1)
            parts = [hi, mid, r2.astype(BF16)]
            if 3 * g < 2 * LANES:
                parts.append(jnp.zeros((2 * LANES - 3 * g, xg.shape[1]), BF16))
            outs.append(_dot(tri3, jnp.concatenate(parts, axis=0)))
        return outs[0] if len(outs) == 1 else jnp.concatenate(outs, axis=0)

    for h in range(GLA_HEADS):
        lk = slice(h * GLA_DK, (h + 1) * GLA_DK)
        lr = sm_ref[:, SM_LR:SM_LR + GLA_RANK]
        z = _bdot(lr, wal_ref[:, lk]) + bal_ref[:, lk]
        la = -_softplus(-z) / GLA_TAU
        ebl_ref[h] = seg_cumsum(la)

    for h in range(GLA_HEADS):
        lk = slice(h * GLA_DK, (h + 1) * GLA_DK)
        lv = slice(h * GLA_DV, (h + 1) * GLA_DV)
        q = q_ref[:, lk] * (GLA_DK ** -0.5)
        k = k_ref[:, lk]
        b = ebl_ref[h]
        lasts = [jnp.broadcast_to(b[(qi + 1) * L - 1:(qi + 1) * L, :], (L, GLA_DK))
                 for qi in range(nsub)]
        b_last = lasts[0] if nsub == 1 else jnp.concatenate(lasts, axis=0)
        q_d = q * jnp.exp(b)
        k_d = k * jnp.exp(-b)
        for gq in range(r // ga):
            gs = slice(gq * ga, (gq + 1) * ga)
            att = jnp.where(incl, _bdot_nt(q_d[gs], k_d[gs]), 0.0)
            oacc_ref[h, gs, :] = _bdot(att, v_ref[gs, lv])
        qd_ref[h] = q_d
        ke_ref[h] = k * jnp.exp(b_last - b)
        ebl_ref[h] = jnp.exp(b_last)

    for qi in range(nsub):
        rs = slice(qi * L, (qi + 1) * L)
        for h in range(GLA_HEADS):
            lv = slice(h * GLA_DV, (h + 1) * GLA_DV)
            s = s_ref[h] if carry else s0_ref[qi, h]
            oacc_ref[h, rs, :] = oacc_ref[h, rs, :] + _bdot(qd_ref[h, rs, :], s)
            kv = _bdot_tn(ke_ref[h, rs, :], v_ref[rs, lv])
            eb = ebl_ref[h, qi * L:qi * L + 1, :]
            col = jnp.transpose(jnp.broadcast_to(eb, (LANES, GLA_DK)))
            s_new = s * jnp.concatenate([col] * (GLA_DV // LANES), axis=1) + kv
            if carry:
                s_ref[h] = s_new
            else:
                so_ref[qi, h] = s_new

    for h in range(GLA_HEADS):
        lv = slice(h * GLA_DV, (h + 1) * GLA_DV)
        o = oacc_ref[h]
        inv = lax.rsqrt(jnp.mean(o * o, axis=-1, keepdims=True) + EPS)
        on = o * inv * ng_ref[...]
        o_ref[:, lv] = (on * _silu(gate_ref[:, lv])).astype(o_ref.dtype)

    if carry:
        @pl.when(pl.program_id(1) == pl.num_programs(1) - 1)
        def _():
            so_ref[...] = s_ref[...]


def _gla(proj, small, w_alpha2, b_alpha, norm_g, s0, *, bsz, t, rows):
    carry = s0 is None
    r = rows
    n_tok = proj.shape[0]
    if carry:
        seq_len = min(CHUNK, t)
        assert t % r == 0 and r % seq_len == 0
        grid = (bsz, t // r)
        row = lambda b, n: b * (t // r) + n
    else:
        assert t <= CHUNK and r % t == 0 and bsz % (r // t) == 0
        seq_len = t
        grid = (bsz // (r // t), 1)
        row = lambda b, n: b
    nsub = r // seq_len
    qb, kb = COL_QA // GLA_QK, COL_KA // GLA_QK
    vb, gb = COL_VA // GLA_V, COL_GATE_A // GLA_V
    in_specs = [
        pl.BlockSpec((r, GLA_QK), lambda b, n: (row(b, n), qb)),
        pl.BlockSpec((r, GLA_QK), lambda b, n: (row(b, n), kb)),
        pl.BlockSpec((r, GLA_V), lambda b, n: (row(b, n), vb)),
        pl.BlockSpec((r, GLA_V), lambda b, n: (row(b, n), gb)),
        pl.BlockSpec((r, LANES), lambda b, n: (row(b, n), 0)),
        pl.BlockSpec((GLA_RANK, GLA_QK), lambda b, n: (0, 0)),
        pl.BlockSpec((1, GLA_QK), lambda b, n: (0, 0)),
        pl.BlockSpec((1, GLA_DV), lambda b, n: (0, 0)),
    ]
    args = [proj, proj, proj, proj, small, w_alpha2, b_alpha, norm_g]
    scratch = [pltpu.VMEM((GLA_HEADS, r, GLA_DK), F32)] * 3 + [
        pltpu.VMEM((GLA_HEADS, r, GLA_DV), F32)]
    if carry:
        state = pl.BlockSpec((None, GLA_HEADS, GLA_DK, GLA_DV), lambda b, n: (b, 0, 0, 0))
        scratch += [pltpu.VMEM((GLA_HEADS, GLA_DK, GLA_DV), F32)]
    else:
        state = pl.BlockSpec((nsub, GLA_HEADS, GLA_DK, GLA_DV), lambda b, n: (b, 0, 0, 0))
        in_specs += [state]
        args += [s0]
    return pl.pallas_call(
        functools.partial(_gla_kernel, seq_len=seq_len, rows=r, carry=carry),
        grid=grid,
        in_specs=in_specs,
        out_specs=[pl.BlockSpec((r, GLA_V), lambda b, n: (row(b, n), 0)), state],
        out_shape=[jax.ShapeDtypeStruct((n_tok, GLA_V), BF16),
                   jax.ShapeDtypeStruct((bsz, GLA_HEADS, GLA_DK, GLA_DV), F32)],
        scratch_shapes=scratch,
        compiler_params=pltpu.CompilerParams(
            dimension_semantics=("arbitrary", "arbitrary"), vmem_limit_bytes=VMEM_LIMIT),
        name="gla",
    )(*args)


GDN_ROWS = 64
HALF = LANES // 2
GDN_GROUP_CARRY = GDN_HEADS
GDN_GROUP_SEQS = 1


def _split_f32(x):
    hi = x.astype(BF16)
    return hi, x - hi.astype(F32)


def _lhs3(x, low_half):
    hi, lo = _split_f32(x)
    t0 = jnp.where(low_half, x, lo).astype(BF16)
    t1 = jnp.where(low_half, hi, jnp.zeros_like(hi))
    return jnp.concatenate([t0, t1], axis=1)


def _rhs3(hi, lo):
    return jnp.concatenate([hi, hi, lo, lo], axis=0)


def _gdn_kernel(*refs, seq_len, carry, group):
    r = GDN_ROWS
    L = seq_len
    nseq = r // L
    tail = CONV_W - 1
    pad = 8
    n_in = 11 if carry else 15
    (q_ref, k_ref, v_ref, gate_ref, sm_ref, alog_ref, dtb_ref, ng_ref, cwq_ref, cwk_ref,
     cwv_ref) = refs[:11]
    cq0_ref, ck0_ref, cv0_ref, s0_ref = (None,) * 4 if carry else refs[11:15]
    o_ref, so_ref = refs[n_in:n_in + 2]
    (nb_ref, tb_ref, nlhs_ref, p_ref, att_ref, rhs_ref, x0_ref, qg_ref, ke_ref, u_ref, w_ref,
     os_ref, eq_ref, ek_ref, ev_ref, cq_ref, ck_ref, cv_ref) = refs[n_in + 2:n_in + 20]
    ri = lax.broadcasted_iota(jnp.int32, (r, LANES), 0)
    li = lax.broadcasted_iota(jnp.int32, (r, LANES), 1)
    cj = li % HALF
    low_half = li < HALF
    if carry:
        (s_ref,) = refs[n_in + 20:]

        @pl.when(pl.program_id(1) == 0)
        def _():
            s_ref[...] = jnp.zeros_like(s_ref)
            for e_ref in (eq_ref, ek_ref, ev_ref):
                e_ref[:, pad:2 * pad, :] = jnp.zeros((tail, pad, GDN_QK), F32)
    else:
        s_ref = None

    def conv(e_ref, x_ref, w_ref, out_ref, c0_ref):
        acc = None
        for i in range(CONV_W):
            sh = tail - i
            if sh == 0:
                xs = x_ref[...]
            else:
                e_ref[sh - 1, pad + sh:pad + sh + r, :] = x_ref[...]
                if not carry:
                    for qi in range(nseq):
                        e_ref[sh - 1, pad + qi * L:pad + qi * L + sh, :] = c0_ref[qi, tail - sh:tail, :]
                xs = e_ref[sh - 1, pad:pad + r, :]
            term = xs * w_ref[i:i + 1, :]
            acc = term if acc is None else acc + term
        if carry:
            for sh in range(1, tail + 1):
                e_ref[sh - 1, pad:pad + sh, :] = x_ref[r - sh:r, :]
        out_ref[...] = _silu(acc)

    conv(eq_ref, q_ref, cwq_ref, cq_ref, cq0_ref)
    conv(ek_ref, k_ref, cwk_ref, ck_ref, ck0_ref)
    conv(ev_ref, v_ref, cwv_ref, cv_ref, cv0_ref)
    sm = sm_ref[...]
    same = (ri // L) == (cj // L)
    incl = same & (ri >= cj)
    strict = same & (ri > cj)
    eye = (ri == cj).astype(F32)

    beta_all = _sigmoid(sm)
    g_all = -jnp.exp(alog_ref[...]) * _softplus(sm + dtb_ref[...])
    g_hi, g_r1 = _split_f32(g_all)
    g_mid, g_r2 = _split_f32(g_r1)
    g_lo = g_r2.astype(BF16)
    tri = incl.astype(BF16)
    tri_lhs = jnp.concatenate([tri, jnp.where(low_half, tri, jnp.zeros_like(tri))], axis=1)
    gh_all = _dot(tri_lhs, jnp.concatenate([g_hi, g_mid, g_lo, g_lo], axis=0))
    gh_t = jnp.transpose(jnp.concatenate([gh_all, gh_all], axis=0))
    glast_all = jnp.concatenate(
        [jnp.broadcast_to(gh_all[(qi + 1) * L - 1:(qi + 1) * L, :], (L, LANES))
         for qi in range(nseq)], axis=0)
    egh_all = jnp.exp(gh_all)
    kend_all = jnp.exp(glast_all - gh_all)

    def normalize(h):
        ls = slice(h * GDN_DK, (h + 1) * GDN_DK)
        q = cq_ref[:, ls]
        k = ck_ref[:, ls]
        cq_ref[:, ls] = (q * lax.rsqrt(jnp.sum(q * q, axis=-1, keepdims=True) + EPS)
                         * (GDN_DK ** -0.5))
        ck_ref[:, ls] = k * lax.rsqrt(jnp.sum(k * k, axis=-1, keepdims=True) + EPS)

    def chunk_matrices(h):
        ls = slice(h * GDN_DK, (h + 1) * GDN_DK)
        q = cq_ref[:, ls]
        k = ck_ref[:, ls]
        col = slice(SM_DEC + h, SM_DEC + h + 1)
        beta = beta_all[:, SM_BETA + h:SM_BETA + h + 1]
        gh = gh_all[:, col]
        decay = jnp.where(incl, jnp.exp(gh - gh_t[col, :]), 0.0)
        kb = k.astype(BF16)
        qkk = lax.dot_general(jnp.concatenate([q.astype(BF16), kb], axis=0),
                              jnp.concatenate([kb, kb], axis=0),
                              (((1,), (1,)), ((), ())), preferred_element_type=F32)
        att_ref[h] = (qkk[:r] * decay).astype(BF16)
        nm = jnp.where(strict, -(beta * qkk[r:] * decay), 0.0)
        nlhs_ref[h] = _lhs3(nm, low_half)
        nb_ref[h] = nm[:, :HALF].astype(BF16)
        p_ref[h] = (eye + nm)[:, :HALF]

    def row_scalings(h):
        ls = slice(h * GDN_DK, (h + 1) * GDN_DK)
        q = cq_ref[:, ls]
        k = ck_ref[:, ls]
        v = cv_ref[:, ls]
        col = slice(SM_DEC + h, SM_DEC + h + 1)
        beta = beta_all[:, SM_BETA + h:SM_BETA + h + 1]
        egh = egh_all[:, col]
        rhs_ref[h] = jnp.concatenate([v * beta, k * (beta * egh)], axis=1)
        qg_ref[h] = q * egh
        ke_ref[h] = k * kend_all[:, col]

    def square_power(h):
        n_pow = nb_ref[h]
        nb_ref[h] = _dot(n_pow, n_pow).astype(BF16)

    def extend_inverse(h):
        pm = p_ref[h]
        p_ref[h] = pm + _dot(pm.astype(BF16), nb_ref[h])

    def first_solve(h):
        tb = p_ref[h].astype(BF16)
        tb_ref[h] = tb
        x0_ref[h] = _dot(tb, rhs_ref[h].astype(BF16))

    def residual(h):
        x0 = x0_ref[h]
        x_hi, x_lo = _split_f32(x0)
        rhs_ref[h] = rhs_ref[h] - x0 + _dot(nlhs_ref[h], _rhs3(x_hi, x_lo.astype(BF16)))

    def refine(h):
        sol = x0_ref[h] + _dot(tb_ref[h], rhs_ref[h].astype(BF16))
        u_ref[h] = sol[:, :GDN_DV]
        w_ref[h] = sol[:, GDN_DV:]

    def state(qi, h):
        return s_ref[h] if carry else s0_ref[qi, h]

    def state_products(h):
        for qi in range(nseq):
            rs = slice(qi * L, (qi + 1) * L)
            lhs = jnp.concatenate([w_ref[h, rs, :], qg_ref[h, rs, :]], axis=0).astype(BF16)
            res = _dot(lhs, state(qi, h).astype(BF16))
            u_ref[h, rs, :] = u_ref[h, rs, :] - res[:L]
            os_ref[h, rs, :] = res[L:]

    def outputs(h):
        ls = slice(h * GDN_DV, (h + 1) * GDN_DV)
        v_new = u_ref[h]
        o = os_ref[h] + _dot(att_ref[h][:, :HALF], v_new.astype(BF16))
        inv = lax.rsqrt(jnp.mean(o * o, axis=-1, keepdims=True) + EPS)
        on = o * inv * ng_ref[...]
        o_ref[:, ls] = (on * _silu(gate_ref[:, ls])).astype(o_ref.dtype)
        for qi in range(nseq):
            rs = slice(qi * L, (qi + 1) * L)
            kv = lax.dot_general(ke_ref[h, rs, :].astype(BF16), v_new[rs].astype(BF16),
                                 (((0,), (0,)), ((), ())), preferred_element_type=F32)
            eg = egh_all[(qi + 1) * L - 1:(qi + 1) * L, SM_DEC + h:SM_DEC + h + 1]
            s_new = state(qi, h) * eg + kv
            if carry:
                s_ref[h] = s_new
            else:
                so_ref[qi, h] = s_new

    stages = [normalize, chunk_matrices, row_scalings]
    p = 2
    while p < L:
        stages += [square_power, extend_inverse]
        p *= 2
    stages += [first_solve, residual, refine, state_products, outputs]
    ngroups = GDN_HEADS // group
    for t in range(ngroups + len(stages) - 1):
        for k, stage in enumerate(stages):
            if 0 <= t - k < ngroups:
                for h in range((t - k) * group, (t - k + 1) * group):
                    stage(h)

    if carry:
        @pl.when(pl.program_id(1) == pl.num_programs(1) - 1)
        def _():
            so_ref[...] = s_ref[...]


def _gdn(proj, small, conv_w, alog_pad, dtb_pad, norm_g, conv0, s0, *, bsz, t):
    carry = s0 is None
    r = GDN_ROWS
    n_tok = proj.shape[0]
    tail = CONV_W - 1
    if carry:
        assert t % r == 0
        seq_len, nseq, grid = r, 1, (bsz, t // r)
        row = lambda b, n: b * (t // r) + n
    else:
        assert r % t == 0 and bsz % (r // t) == 0
        seq_len, nseq, grid = t, r // t, (bsz // (r // t), 1)
        row = lambda b, n: b
    qb = COL_QKV_B // GDN_QK
    gb = COL_GATE_B // GDN_V
    rowspec = lambda cb: pl.BlockSpec((r, GDN_QK), lambda b, n: (row(b, n), cb))
    cwspec = lambda cb: pl.BlockSpec((CONV_W, GDN_QK), lambda b, n: (0, cb))
    vec = pl.BlockSpec((1, LANES), lambda b, n: (0, 0))
    in_specs = [rowspec(qb), rowspec(qb + 1), rowspec(qb + 2), rowspec(gb),
                pl.BlockSpec((r, LANES), lambda b, n: (row(b, n), 0)), vec, vec, vec,
                cwspec(0), cwspec(1), cwspec(2)]
    args = [proj, proj, proj, proj, small, alog_pad, dtb_pad, norm_g, conv_w, conv_w, conv_w]
    hshape = (GDN_HEADS, r, LANES)
    sq = (GDN_HEADS, r, r)
    wide = (GDN_HEADS, r, 2 * LANES)
    scratch = ([pltpu.VMEM(sq, BF16)] * 2 + [pltpu.VMEM(wide, BF16)] + [pltpu.VMEM(sq, F32)]
               + [pltpu.VMEM(hshape, BF16)]
               + [pltpu.VMEM(wide, F32)] * 2
               + [pltpu.VMEM(hshape, F32)] * 5
               + [pltpu.VMEM((tail, 8 + r + 8, GDN_QK), F32)] * 3
               + [pltpu.VMEM((r, GDN_QK), F32)] * 3)
    if carry:
        state = pl.BlockSpec((None, GDN_HEADS, GDN_DK, GDN_DV), lambda b, n: (b, 0, 0, 0))
        scratch += [pltpu.VMEM((GDN_HEADS, GDN_DK, GDN_DV), F32)]
    else:
        state = pl.BlockSpec((nseq, GDN_HEADS, GDN_DK, GDN_DV), lambda b, n: (b, 0, 0, 0))
        c0spec = lambda cb: pl.BlockSpec((nseq, tail, GDN_QK), lambda b, n: (b, 0, cb))
        in_specs += [c0spec(0), c0spec(1), c0spec(2), state]
        args += [conv0, conv0, conv0, s0]
    return pl.pallas_call(
        functools.partial(_gdn_kernel, seq_len=seq_len, carry=carry,
                          group=GDN_GROUP_CARRY if carry else GDN_GROUP_SEQS),
        grid=grid,
        in_specs=in_specs,
        out_specs=[pl.BlockSpec((r, GDN_V), lambda b, n: (row(b, n), 0)), state],
        out_shape=[jax.ShapeDtypeStruct((n_tok, GDN_V), BF16),
                   jax.ShapeDtypeStruct((bsz, GDN_HEADS, GDN_DK, GDN_DV), F32)],
        scratch_shapes=scratch,
        compiler_params=pltpu.CompilerParams(
            dimension_semantics=("arbitrary", "arbitrary"), vmem_limit_bytes=VMEM_LIMIT),
        name="gdn",
    )(*args)


def _merge_kernel(oa_ref, ob_ref, wa_ref, wb_ref, ma_ref, mb_ref, o_ref):
    ya = jnp.dot(oa_ref[...].astype(BF16), wa_ref[...], preferred_element_type=F32)
    yb = jnp.dot(ob_ref[...].astype(BF16), wb_ref[...], preferred_element_type=F32)
    merged = _sigmoid(ma_ref[...]) * ya + _sigmoid(mb_ref[...]) * yb
    o_ref[...] = merged.astype(o_ref.dtype)


def _merge(o_a, o_b, w_br_a, w_br_b, proj):
    n_tok = o_a.shape[0]
    tm = min(1024, n_tok)
    tn = 512
    ma, mb = COL_MA // tn, COL_MB // tn
    return pl.pallas_call(
        _merge_kernel,
        grid=(n_tok // tm, D_MODEL // tn),
        in_specs=[
            pl.BlockSpec((tm, GLA_V), lambda i, j: (i, 0)),
            pl.BlockSpec((tm, GDN_V), lambda i, j: (i, 0)),
            pl.BlockSpec((GLA_V, tn), lambda i, j: (0, j)),
            pl.BlockSpec((GDN_V, tn), lambda i, j: (0, j)),
            pl.BlockSpec((tm, tn), lambda i, j: (i, ma + j)),
            pl.BlockSpec((tm, tn), lambda i, j: (i, mb + j)),
        ],
        out_specs=pl.BlockSpec((tm, tn), lambda i, j: (i, j)),
        out_shape=jax.ShapeDtypeStruct((n_tok, D_MODEL), BF16),
        compiler_params=pltpu.CompilerParams(
            dimension_semantics=("arbitrary", "arbitrary"), vmem_limit_bytes=VMEM_LIMIT),
        name="merge",
    )(o_a, o_b, w_br_a, w_br_b, proj, proj)


def _out_kernel(m_ref, w_ref, x_ref, g_ref, y_ref):
    out = x_ref[...] + jnp.dot(m_ref[...], w_ref[...], preferred_element_type=F32)
    inv = lax.rsqrt(jnp.mean(out * out, axis=-1, keepdims=True) + EPS)
    y_ref[...] = out * inv * g_ref[...]


def _out_proj(merged, w_out, x2d, final_g):
    n_tok = merged.shape[0]
    tm = min(512, n_tok)
    return pl.pallas_call(
        _out_kernel,
        grid=(n_tok // tm,),
        in_specs=[
            pl.BlockSpec((tm, D_MODEL), lambda i: (i, 0)),
            pl.BlockSpec((D_MODEL, D_MODEL), lambda i: (0, 0)),
            pl.BlockSpec((tm, D_MODEL), lambda i: (i, 0)),
            pl.BlockSpec((1, D_MODEL), lambda i: (0, 0)),
        ],
        out_specs=pl.BlockSpec((tm, D_MODEL), lambda i: (i, 0)),
        out_shape=jax.ShapeDtypeStruct((n_tok, D_MODEL), F32),
        compiler_params=pltpu.CompilerParams(
            dimension_semantics=("arbitrary",), vmem_limit_bytes=VMEM_LIMIT),
        name="out_proj",
    )(merged, w_out, x2d, final_g)


def _layer(x, s_gla, s_gdn, conv_buf, p, *, gla_rows):
    bsz, t, d = x.shape
    x2d = x.reshape(bsz * t, d)
    tail = CONV_W - 1
    proj, small = _in_proj(x2d, p["ln_g"], p["w_main"], p["w_small"])
    conv_new = proj.reshape(bsz, t, N_MAIN)[:, t - tail:, COL_QKV_B:COL_QKV_B + CONV_CH]
    o_a, s_gla_new = _gla(proj, small, p["w_alpha2"], p["b_alpha"], p["gla_norm_g"], s_gla,
                          bsz=bsz, t=t, rows=gla_rows)
    o_b, s_gdn_new = _gdn(proj, small, p["conv_w"], p["alog_pad"], p["dtb_pad"],
                          p["gdn_norm_g"], conv_buf, s_gdn, bsz=bsz, t=t)
    merged = _merge(o_a, o_b, p["w_br_a"], p["w_br_b"], proj)
    y = _out_proj(merged, p["w_out"], x2d, p["final_g"])
    return y.reshape(bsz, t, d), s_gla_new, s_gdn_new, conv_new


def _prep_params(ln_in_g, w_in, w_alpha2, b_alpha, conv_w, a_log, dt_bias, gla_norm_g,
                 gdn_norm_g, w_br_a, w_br_b, w_out, final_norm_g):
    wt = jnp.transpose(w_in)
    w_main = _prep_w_main(wt)
    n_small = GLA_RANK + 2 * GDN_HEADS
    w_small = jnp.concatenate(
        [wt[W_IN_OFFS[3]:W_IN_OFFS[4]], wt[W_IN_OFFS[6]:W_IN_OFFS[8]],
         jnp.zeros((LANES - n_small, D_MODEL), F32)], axis=0)
    lane_pad = lambda v: jnp.zeros((1, LANES), F32).at[0, SM_DEC:SM_DEC + GDN_HEADS].set(v)
    return {
        "ln_g": ln_in_g.reshape(1, D_MODEL),
        "w_main": w_main,
        "w_small": w_small,
        "w_alpha2": w_alpha2,
        "b_alpha": b_alpha.reshape(1, GLA_QK),
        "conv_w": conv_w,
        "alog_pad": lane_pad(a_log),
        "dtb_pad": lane_pad(dt_bias),
        "gla_norm_g": gla_norm_g.reshape(1, GLA_DV),
        "gdn_norm_g": gdn_norm_g.reshape(1, GDN_DV),
        "w_br_a": w_br_a.astype(BF16),
        "w_br_b": w_br_b.astype(BF16),
        "w_out": w_out.astype(BF16),
        "final_g": final_norm_g.reshape(1, D_MODEL),
    }


def kernel(x_prompt, x_sample, state_gla, state_gdn, state_conv, ln_in_g, w_in, w_alpha2,
           b_alpha, conv_w, a_log, dt_bias, gla_norm_g, gdn_norm_g, w_br_a, w_br_b, w_out,
           final_norm_g):
    assert ln_in_g.shape[0] == 1, "single layer"
    p = _prep_params(ln_in_g[0], w_in[0], w_alpha2[0], b_alpha[0], conv_w[0], a_log[0],
                     dt_bias[0], gla_norm_g[0], gdn_norm_g[0], w_br_a[0], w_br_b[0], w_out[0],
                     final_norm_g)
    yp, gla_p, gdn_p, conv_p = _layer(x_prompt, None, None, None, p, gla_rows=GLA_PROMPT_ROWS)
    ys, gla_s, gdn_s, conv_s = _layer(x_sample, state_gla[0], state_gdn[0], state_conv[0], p,
                                      gla_rows=GLA_SAMPLE_ROWS)
    return (yp, ys, gla_p[None], gdn_p[None], conv_p[None], gla_s[None], gdn_s[None],
            conv_s[None])
```
